```python
import jax, jax.numpy as jnp
from jax import lax
import numpy as np

D_MODEL = 1024
BATCH = 2
SEQ = 8192
DEPTH = 2

CHUNK = 64
PLE_DIM = 256
N_A_LAYERS = DEPTH // 2
N_B_LAYERS = DEPTH - N_A_LAYERS
N_DENSE = (DEPTH + 1) // 2
N_MOE = DEPTH // 2

RET_HEADS = 4
RET_QK_DIM = D_MODEL // RET_HEADS
RET_V_DIM = 2 * RET_QK_DIM
RET_QK_WIDTH = RET_HEADS * RET_QK_DIM
RET_V_WIDTH = RET_HEADS * RET_V_DIM
ROPE_BASE = 10000.0

ATT_HEADS = 16
ATT_HEAD_DIM = D_MODEL // ATT_HEADS
ATT_WIDTH = ATT_HEADS * ATT_HEAD_DIM
LEFT_CHUNKS = 8
BAND = (LEFT_CHUNKS + 1) * CHUNK
REL_CLIP = 256
N_REL = REL_CLIP + CHUNK

D_FF_DENSE = 2816
N_EXPERTS = 8
TOP_K = 2
D_FF_EXPERT = 3584
EPS = 1e-6

kernel_name = 'yoco_retention_chunkband_moe_ple'


def rmsnorm(x, g):
    xf = x.astype(jnp.float32)
    y = xf * lax.rsqrt(jnp.mean(xf * xf, axis=-1, keepdims=True) + EPS)
    return (y * g.astype(jnp.float32)).astype(x.dtype)


def swiglu(x, w_gate, w_up, w_down):
    return (jax.nn.silu(x @ w_gate) * (x @ w_up)) @ w_down


def rotary(x, positions):
    half = x.shape[-1] // 2
    inv_freq = 1.0 / (ROPE_BASE ** jnp.linspace(0.0, 1.0, half, dtype=jnp.float32))
    ang = positions.astype(jnp.float32)[..., None] * inv_freq
    cos = jnp.cos(ang)[:, :, None, :]
    sin = jnp.sin(ang)[:, :, None, :]
    x1 = x[..., :half].astype(jnp.float32)
    x2 = x[..., half:].astype(jnp.float32)
    return jnp.concatenate([x1 * cos - x2 * sin, x1 * sin + x2 * cos], axis=-1).astype(x.dtype)


def chunkwise_retention(q, k, v):
    B, S, H, dk = q.shape
    dv = v.shape[-1]
    nc = S // CHUNK
    lg = jnp.log1p(-jnp.exp2(-5.0 - jnp.arange(H, dtype=jnp.float32)))
    idx = jnp.arange(CHUNK, dtype=jnp.float32)
    diff = idx[:, None] - idx[None, :]
    causal = diff >= 0
    dmask = jnp.where(causal, jnp.exp(jnp.where(causal, diff, 0.0)[None] * lg[:, None, None]), 0.0)
    xi = jnp.exp((idx[:, None] + 1.0) * lg[None, :])
    zeta = jnp.exp((CHUNK - 1.0 - idx)[:, None] * lg[None, :])
    g_chunk = jnp.exp(CHUNK * lg)

    def to_chunks(t):
        return jnp.moveaxis(t.astype(jnp.float32).reshape(B, nc, CHUNK, H, t.shape[-1]), 1, 0)

    def step(state, inp):
        qn, kn, vn = inp
        s = jnp.einsum('bihd,bjhd->bhij', qn, kn) * dmask
        inner = jnp.einsum('bhij,bjhe->bihe', s, vn)
        cross = jnp.einsum('bihd,bhde->bihe', qn, state) * xi[None, :, :, None]
        state = state * g_chunk[None, :, None, None] + jnp.einsum(
            'bjhd,bjhe->bhde', kn * zeta[None, :, :, None], vn)
        return state, inner + cross

    state0 = jnp.zeros((B, H, dk, dv), jnp.float32)
    _, out = lax.scan(step, state0, (to_chunks(q), to_chunks(k), to_chunks(v)))
    return jnp.moveaxis(out, 0, 1).reshape(B, S, H, dv)


def retention_mixer(xn, positions, w_in, gn_gain, w_o):
    B, S, _ = xn.shape
    proj = xn @ w_in
    q, k, v, g = jnp.split(proj, [RET_QK_WIDTH, 2 * RET_QK_WIDTH, 2 * RET_QK_WIDTH + RET_V_WIDTH], axis=-1)
    q = rotary(q.reshape(B, S, RET_HEADS, RET_QK_DIM), positions)
    k = rotary(k.reshape(B, S, RET_HEADS, RET_QK_DIM), positions) * (RET_QK_DIM ** -0.5)
    v = v.reshape(B, S, RET_HEADS, RET_V_DIM)
    o = chunkwise_retention(q, k, v)
    o = rmsnorm(o, gn_gain.reshape(RET_HEADS, RET_V_DIM)).astype(xn.dtype)
    return (jax.nn.silu(g) * o.reshape(B, S, RET_V_WIDTH)) @ w_o


def shared_band_kv(h, norm_kv, w_kv):
    B, S, _ = h.shape
    kv = rmsnorm(h, norm_kv) @ w_kv
    k, v = jnp.split(kv, 2, axis=-1)
    pad = ((0, 0), (LEFT_CHUNKS * CHUNK, 0), (0, 0), (0, 0))
    k = jnp.pad(k.reshape(B, S, ATT_HEADS, ATT_HEAD_DIM), pad)
    v = jnp.pad(v.reshape(B, S, ATT_HEADS, ATT_HEAD_DIM), pad)
    return k, v


def band_attention_mixer(xn, kp, vp, w_q, rel_table, w_o):
    B, S, _ = xn.shape
    nc = S // CHUNK
    q = (xn @ w_q).reshape(B, nc, CHUNK, ATT_HEADS, ATT_HEAD_DIM) * (ATT_HEAD_DIM ** -0.5)
    q = jnp.moveaxis(q, 1, 0)
    qi = np.arange(CHUNK)[:, None]
    kj = np.arange(BAND)[None, :]
    rel = np.clip(qi - kj + LEFT_CHUNKS * CHUNK, -(CHUNK - 1), REL_CLIP) + (CHUNK - 1)
    bias = rel_table[:, rel].astype(jnp.float32)
    k_offset = jnp.arange(BAND) - LEFT_CHUNKS * CHUNK

    def one_chunk(args):
        c, q_c = args
        start = c * CHUNK
        k_band = lax.dynamic_slice_in_dim(kp, start, BAND, axis=1)
        v_band = lax.dynamic_slice_in_dim(vp, start, BAND, axis=1)
        s = jnp.einsum('bihd,bjhd->bhij', q_c, k_band).astype(jnp.float32) + bias
        valid = (start + k_offset) >= 0
        s = jnp.where(valid, s, -jnp.inf)
        a = jax.nn.softmax(s, axis=-1).astype(v_band.dtype)
        return jnp.einsum('bhij,bjhd->bihd', a, v_band)

    o = lax.map(one_chunk, (jnp.arange(nc), q))
    o = jnp.moveaxis(o, 0, 1).reshape(B, S, ATT_WIDTH)
    return o @ w_o


def moe_swiglu(xn, w_router, w_gate, w_up, w_down):
    logits = (xn @ w_router).astype(jnp.float32)
    top_v, top_i = lax.top_k(logits, TOP_K)
    top_w = jax.nn.softmax(top_v, axis=-1)
    gates = jnp.sum(jax.nn.one_hot(top_i, N_EXPERTS, dtype=jnp.float32) * top_w[..., None], axis=-2)
    gates = gates.astype(xn.dtype)
    y = jnp.zeros_like(xn)
    for e in range(N_EXPERTS):
        y = y + gates[..., e:e + 1] * swiglu(xn, w_gate[e], w_up[e], w_down[e])
    return y


def per_layer_embedding(h, p_i, g, w_up, w_gate):
    gate = jax.nn.sigmoid(rmsnorm(h, g) @ w_gate)
    return (p_i @ w_up) * gate


def setup_inputs(seed: int = 0) -> dict:
    key = jax.random.key(seed)
    ks = jax.random.split(key, 26)
    f32 = jnp.float32

    def nrm(k, shape, fan_in):
        return jax.random.normal(k, shape, f32) * (fan_in ** -0.5)

    def gain(k, shape):
        return 1.0 + 0.05 * jax.random.normal(k, shape, f32)

    offsets = jax.random.randint(ks[2], (BATCH, 1), 0, 4096, dtype=jnp.int32)
    positions = offsets + jnp.arange(SEQ, dtype=jnp.int32)[None, :]
    return {
        'x': jax.random.normal(ks[0], (BATCH, SEQ, D_MODEL), f32),
        'p': jax.random.normal(ks[1], (DEPTH, BATCH, SEQ, PLE_DIM), f32),
        'positions': positions,
        'norm_mix': gain(ks[3], (DEPTH, D_MODEL)),
        'norm_ffn': gain(ks[4], (DEPTH, D_MODEL)),
        'norm_ple': gain(ks[5], (DEPTH, D_MODEL)),
        'w_in_a': nrm(ks[6], (N_A_LAYERS, D_MODEL, 2 * RET_QK_WIDTH + 2 * RET_V_WIDTH), D_MODEL),
        'ret_gn': gain(ks[7], (N_A_LAYERS, RET_V_WIDTH)),
        'w_out_a': nrm(ks[8], (N_A_LAYERS, RET_V_WIDTH, D_MODEL), RET_V_WIDTH),
        'norm_kv': gain(ks[9], (D_MODEL,)),
        'w_kv': nrm(ks[10], (D_MODEL, 2 * ATT_WIDTH), D_MODEL),
        'w_q_b': nrm(ks[11], (N_B_LAYERS, D_MODEL, ATT_WIDTH), D_MODEL),
        'rel_bias': 0.2 * jax.random.normal(ks[12], (N_B_LAYERS, ATT_HEADS, N_REL), f32),
        'w_out_b': nrm(ks[13], (N_B_LAYERS, ATT_WIDTH, D_MODEL), ATT_WIDTH),
        'w_gate_dense': nrm(ks[14], (N_DENSE, D_MODEL, D_FF_DENSE), D_MODEL),
        'w_up_dense': nrm(ks[15], (N_DENSE, D_MODEL, D_FF_DENSE), D_MODEL),
        'w_down_dense': nrm(ks[16], (N_DENSE, D_FF_DENSE, D_MODEL), D_FF_DENSE),
        'w_router': nrm(ks[17], (N_MOE, D_MODEL, N_EXPERTS), D_MODEL),
        'w_gate_moe': nrm(ks[18], (N_MOE, N_EXPERTS, D_MODEL, D_FF_EXPERT), D_MODEL),
        'w_up_moe': nrm(ks[19], (N_MOE, N_EXPERTS, D_MODEL, D_FF_EXPERT), D_MODEL),
        'w_down_moe': nrm(ks[20], (N_MOE, N_EXPERTS, D_FF_EXPERT, D_MODEL), D_FF_EXPERT),
        'w_ple_up': nrm(ks[21], (DEPTH, PLE_DIM, D_MODEL), PLE_DIM),
        'w_ple_gate': nrm(ks[22], (DEPTH, D_MODEL, D_MODEL), D_MODEL),
        'norm_final': gain(ks[23], (D_MODEL,)),
    }


def reference(x, p, positions, norm_mix, norm_ffn, norm_ple, w_in_a, ret_gn, w_out_a,
              norm_kv, w_kv, w_q_b, rel_bias, w_out_b, w_gate_dense, w_up_dense, w_down_dense,
              w_router, w_gate_moe, w_up_moe, w_down_moe, w_ple_up, w_ple_gate, norm_final):
    h = x
    kp = None
    vp = None
    for i in range(DEPTH):
        xn = rmsnorm(h, norm_mix[i])
        if i < N_A_LAYERS:
            h = h + retention_mixer(xn, positions, w_in_a[i], ret_gn[i], w_out_a[i])
        else:
            if i == N_A_LAYERS:
                kp, vp = shared_band_kv(h, norm_kv, w_kv)
            j = i - N_A_LAYERS
            h = h + band_attention_mixer(xn, kp, vp, w_q_b[j], rel_bias[j], w_out_b[j])
        xn = rmsnorm(h, norm_ffn[i])
        m = i // 2
        if i % 2 == 0:
            h = h + swiglu(xn, w_gate_dense[m], w_up_dense[m], w_down_dense[m])
        else:
            h = h + moe_swiglu(xn, w_router[m], w_gate_moe[m], w_up_moe[m], w_down_moe[m])
        h = h + per_layer_embedding(h, p[i], norm_ple[i], w_ple_up[i], w_ple_gate[i])
    return rmsnorm(h, norm_final)
```

```python
import functools

import numpy as np
import jax
import jax.numpy as jnp
from jax import lax
from jax.experimental import pallas as pl
from jax.experimental.pallas import tpu as pltpu

F32 = jnp.float32
BF16 = jnp.bfloat16

D_MODEL = 1024
BATCH = 2
SEQ = 8192
TOKENS = BATCH * SEQ
CHUNK = 64
PLE_DIM = 256

RET_HEADS = 4
RET_QK_DIM = 256
RET_V_DIM = 512
RET_QK_WIDTH = RET_HEADS * RET_QK_DIM
RET_V_WIDTH = RET_HEADS * RET_V_DIM
ROPE_BASE = 10000.0
ROPE_HALF = RET_QK_DIM // 2

ATT_HEADS = 16
ATT_HEAD_DIM = 64
LEFT_CHUNKS = 8
LEFT_PAD = LEFT_CHUNKS * CHUNK
REL_CLIP = 256

D_FF_DENSE = 2816
N_EXPERTS = 8
D_FF_EXPERT = 3584
EPS = 1e-6

LANES = 128
MXU_DIM = 256
VMEM_LIMIT = 56 * 1024 * 1024

RET_SUB = MXU_DIM
RET_BLOCK = 1024
ATT_QB = 128
ATT_KB = ATT_QB + LEFT_PAD
ATT_QS = 2048
ATT_HG = MXU_DIM // ATT_HEAD_DIM
NEG_BIG = -1e30


def _dot(a, b):
    return jnp.dot(a, b, preferred_element_type=F32)


def _rms(x, g):
    return x * lax.rsqrt(jnp.mean(x * x, axis=-1, keepdims=True) + EPS) * g


def _silu(x):
    return x * jax.nn.sigmoid(x)


def _const_spec(shape):
    nd = len(shape)
    return pl.BlockSpec(shape, lambda *_: (0,) * nd, pipeline_mode=pl.Buffered(1))


def _params(sem):
    return pltpu.CompilerParams(dimension_semantics=sem, vmem_limit_bytes=VMEM_LIMIT)


def _inproj_kernel(x_ref, pos_ref, g_ref, invf_ref, w_ref, q_ref, k_ref, v_ref, gate_ref):
    xn = _rms(x_ref[...], g_ref[...]).astype(BF16)
    ang = pos_ref[...].astype(F32) * invf_ref[...]
    cos = jnp.cos(ang)
    sin = jnp.sin(ang)
    k_scale = RET_QK_DIM ** -0.5
    for h in range(RET_HEADS):
        lo = h * RET_QK_DIM
        mid = lo + ROPE_HALF
        hi = lo + RET_QK_DIM
        pq = _dot(xn, w_ref[:, lo:hi])
        x1, x2 = pq[:, :ROPE_HALF], pq[:, ROPE_HALF:]
        q_ref[:, lo:mid] = (x1 * cos - x2 * sin).astype(BF16)
        q_ref[:, mid:hi] = (x1 * sin + x2 * cos).astype(BF16)
        pk = _dot(xn, w_ref[:, RET_QK_WIDTH + lo:RET_QK_WIDTH + hi])
        x1, x2 = pk[:, :ROPE_HALF], pk[:, ROPE_HALF:]
        k_ref[:, lo:mid] = ((x1 * cos - x2 * sin) * k_scale).astype(BF16)
        k_ref[:, mid:hi] = ((x1 * sin + x2 * cos) * k_scale).astype(BF16)
    v0 = 2 * RET_QK_WIDTH
    g0 = v0 + RET_V_WIDTH
    for c in range(RET_HEADS):
        lo, hi = c * RET_V_DIM, (c + 1) * RET_V_DIM
        v_ref[:, lo:hi] = _dot(xn, w_ref[:, v0 + lo:v0 + hi]).astype(BF16)
        gate_ref[:, lo:hi] = _silu(_dot(xn, w_ref[:, g0 + lo:g0 + hi])).astype(BF16)


def _inproj(x, pos, g, invf, w, tm=512):
    n = TOKENS // tm
    tile = lambda width: pl.BlockSpec((tm, width), lambda i: (i, 0))
    return pl.pallas_call(
        _inproj_kernel,
        grid=(n,),
        in_specs=[tile(D_MODEL), tile(1), _const_spec((1, D_MODEL)), _const_spec((1, ROPE_HALF)),
                  _const_spec(w.shape)],
        out_specs=[tile(RET_QK_WIDTH), tile(RET_QK_WIDTH), tile(RET_V_WIDTH), tile(RET_V_WIDTH)],
        out_shape=[jax.ShapeDtypeStruct((TOKENS, RET_QK_WIDTH), BF16),
                   jax.ShapeDtypeStruct((TOKENS, RET_QK_WIDTH), BF16),
                   jax.ShapeDtypeStruct((TOKENS, RET_V_WIDTH), BF16),
                   jax.ShapeDtypeStruct((TOKENS, RET_V_WIDTH), BF16)],
        compiler_params=_params(("parallel",)),
        name="inproj",
    )(x, pos, g, invf, w)


def _ret_kernel(gc_ref, q_ref, k_ref, v_ref, gate_ref, dm_ref, xi_ref, zeta_ref, gn_ref,
                o_ref, state_ref):
    h = pl.program_id(1)

    @pl.when(pl.program_id(2) == 0)
    def _():
        state_ref[...] = jnp.zeros_like(state_ref)

    g_chunk = gc_ref[h]
    dm = dm_ref[0]
    xi = xi_ref[0]
    zeta = zeta_ref[0]
    gn = gn_ref[...]
    for j in range(RET_BLOCK // RET_SUB):
        sl = slice(j * RET_SUB, (j + 1) * RET_SUB)
        qj = q_ref[sl, :]
        kj = k_ref[sl, :]
        vj = v_ref[sl, :]
        st = state_ref[...]
        s = lax.dot_general(qj, kj, (((1,), (1,)), ((), ())), preferred_element_type=F32) * dm
        inner = _dot(s.astype(BF16), vj)
        cross = _dot(qj, st.astype(BF16)) * xi
        kz = (kj.astype(F32) * zeta).astype(BF16)
        upd = lax.dot_general(kz, vj, (((0,), (0,)), ((), ())), preferred_element_type=F32)
        state_ref[...] = st * g_chunk + upd
        o = inner + cross
        on = _rms(o, gn)
        o_ref[sl, :] = (on * gate_ref[sl, :].astype(F32)).astype(BF16)


def _retention(q, k, v, gate, gn):
    lg = jnp.log1p(-jnp.exp2(-5.0 - jnp.arange(RET_HEADS, dtype=F32)))
    idx = jnp.arange(RET_SUB, dtype=F32)
    diff = idx[:, None] - idx[None, :]
    causal = diff >= 0
    dmask = jnp.where(causal, jnp.exp(jnp.where(causal, diff, 0.0)[None] * lg[:, None, None]), 0.0)
    xi = jnp.exp((idx[None, :] + 1.0) * lg[:, None])[:, :, None]
    zeta = jnp.exp((RET_SUB - 1.0 - idx)[None, :] * lg[:, None])[:, :, None]
    g_chunk = jnp.exp(RET_SUB * lg)
    nb = SEQ // RET_BLOCK
    row = lambda b, h, n: (b * nb + n, h)
    per_head = lambda shape: pl.BlockSpec((1,) + shape, lambda b, h, n: (h, 0, 0))
    return pl.pallas_call(
        _ret_kernel,
        grid=(BATCH, RET_HEADS, nb),
        in_specs=[pl.BlockSpec(memory_space=pltpu.SMEM),
                  pl.BlockSpec((RET_BLOCK, RET_QK_DIM), row),
                  pl.BlockSpec((RET_BLOCK, RET_QK_DIM), row),
                  pl.BlockSpec((RET_BLOCK, RET_V_DIM), row),
                  pl.BlockSpec((RET_BLOCK, RET_V_DIM), row),
                  per_head((RET_SUB, RET_SUB)), per_head((RET_SUB, 1)), per_head((RET_SUB, 1)),
                  pl.BlockSpec((1, RET_V_DIM), lambda b, h, n: (0, h))],
        out_specs=pl.BlockSpec((RET_BLOCK, RET_V_DIM), row),
        out_shape=jax.ShapeDtypeStruct((TOKENS, RET_V_WIDTH), BF16),
        scratch_shapes=[pltpu.VMEM((RET_QK_DIM, RET_V_DIM), F32)],
        compiler_params=_params(("parallel", "parallel", "arbitrary")),
        name="retention",
    )(g_chunk, q, k, v, gate, dmask, xi, zeta, gn)


def _ple(h, p_ref, np_ref, wpu_ref, wpg_ref):
    gate = jax.nn.sigmoid(_dot(_rms(h, np_ref[...]).astype(BF16), wpg_ref[...]))
    return h + _dot(p_ref[...].astype(BF16), wpu_ref[...]) * gate


def _post_a_kernel(h_ref, og_ref, p_ref, wo_ref, nf_ref, wg_ref, wu_ref, wd_ref, np_ref, wpu_ref,
                   wpg_ref, out_ref):
    h1 = h_ref[...] + _dot(og_ref[...], wo_ref[...])
    xn = _rms(h1, nf_ref[...]).astype(BF16)
    a = (_silu(_dot(xn, wg_ref[...])) * _dot(xn, wu_ref[...])).astype(BF16)
    h2 = h1 + _dot(a, wd_ref[...])
    out_ref[...] = _ple(h2, p_ref, np_ref, wpu_ref, wpg_ref)


def _post_a(h, og, p, wo, nf, wg, wu, wd, npl, wpu, wpg, tm=256):
    n = TOKENS // tm
    tile = lambda width: pl.BlockSpec((tm, width), lambda i: (i, 0))
    consts = [wo, nf, wg, wu, wd, npl, wpu, wpg]
    return pl.pallas_call(
        _post_a_kernel,
        grid=(n,),
        in_specs=[tile(D_MODEL), tile(RET_V_WIDTH), tile(PLE_DIM)] + [_const_spec(c.shape) for c in consts],
        out_specs=tile(D_MODEL),
        out_shape=jax.ShapeDtypeStruct((TOKENS, D_MODEL), F32),
        compiler_params=_params(("parallel",)),
        name="post_a",
    )(h, og, p, *consts)


def _qkv_kernel(h_ref, gq_ref, gkv_ref, wq_ref, wkv_ref, q_ref, k_ref, v_ref):
    h = h_ref[...]
    hn = h * lax.rsqrt(jnp.mean(h * h, axis=-1, keepdims=True) + EPS)
    q = _dot((hn * gq_ref[...]).astype(BF16), wq_ref[...]) * (ATT_HEAD_DIM ** -0.5)
    q_ref[...] = q.astype(BF16)
    kv = _dot((hn * gkv_ref[...]).astype(BF16), wkv_ref[...])
    keep = (pl.program_id(1) > 0).astype(F32)
    k_ref[...] = (kv[:, :D_MODEL] * keep).astype(BF16)
    v_ref[...] = (kv[:, D_MODEL:] * keep).astype(BF16)


def _qkv(h, gq, gkv, wq, wkv):
    tm = LEFT_PAD
    nb = SEQ // tm
    src = lambda b, j: (b * nb + jnp.maximum(j - 1, 0), 0)
    pad_spec = pl.BlockSpec((None, tm, D_MODEL), lambda b, j: (b, j, 0))
    pad_shape = jax.ShapeDtypeStruct((BATCH, SEQ + LEFT_PAD, D_MODEL), BF16)
    return pl.pallas_call(
        _qkv_kernel,
        grid=(BATCH, nb + 1),
        in_specs=[pl.BlockSpec((tm, D_MODEL), src), _const_spec((1, D_MODEL)), _const_spec((1, D_MODEL)),
                  _const_spec(wq.shape), _const_spec(wkv.shape)],
        out_specs=[pl.BlockSpec((tm, D_MODEL), src), pad_spec, pad_spec],
        out_shape=[jax.ShapeDtypeStruct((TOKENS, D_MODEL), BF16), pad_shape, pad_shape],
        compiler_params=_params(("parallel", "arbitrary")),
        name="qkv_b",
    )(h, gq, gkv, wq, wkv)


def _attn_kernel(q_ref, k_ref, v_ref, bias_ref, o_ref):
    n = pl.program_id(2)
    lane = lax.broadcasted_iota(jnp.int32, (ATT_QB, MXU_DIM), 1)
    head_of_lane = lane // ATT_HEAD_DIM
    col = lax.broadcasted_iota(jnp.int32, (ATT_HG * ATT_QB, ATT_KB), 1)
    bias = bias_ref[...].reshape(ATT_HG * ATT_QB, ATT_KB)

    def body(qi, carry):
        q0 = pl.multiple_of(qi * ATT_QB, ATT_QB)
        kstart = pl.multiple_of(n * ATT_QS + qi * ATT_QB, ATT_QB)
        qb = q_ref[pl.ds(q0, ATT_QB), :]
        kb = k_ref[pl.ds(kstart, ATT_KB), :]
        vb = v_ref[pl.ds(kstart, ATT_KB), :]
        zero = jnp.zeros_like(qb)
        qs = jnp.concatenate([jnp.where(head_of_lane == g, qb, zero) for g in range(ATT_HG)], axis=0)
        s = lax.dot_general(qs, kb, (((1,), (1,)), ((), ())), preferred_element_type=F32) + bias
        s = jnp.where(col + kstart >= LEFT_PAD, s, NEG_BIG)
        m = jnp.max(s, axis=-1, keepdims=True)
        p = jnp.exp(s - m)
        l = jnp.sum(p, axis=-1, keepdims=True)
        pv = _dot(p.astype(BF16), vb) / l
        o = pv[0:ATT_QB]
        for g in range(1, ATT_HG):
            o = jnp.where(head_of_lane == g, pv[g * ATT_QB:(g + 1) * ATT_QB], o)
        o_ref[pl.ds(q0, ATT_QB), :] = o.astype(BF16)
        return carry

    lax.fori_loop(0, ATT_QS // ATT_QB, body, 0)


def _band_bias(rel_table):
    qi = np.arange(ATT_QB)[:, None]
    kj = np.arange(ATT_KB)[None, :]
    dist = qi - (kj - LEFT_PAD)
    rel = np.clip(dist, -(CHUNK - 1), REL_CLIP) + (CHUNK - 1)
    q_chunk = qi // CHUNK
    k_chunk = kj // CHUNK - LEFT_CHUNKS
    allowed = (k_chunk <= q_chunk) & (k_chunk >= q_chunk - LEFT_CHUNKS)
    return jnp.where(jnp.asarray(allowed)[None], rel_table[:, rel].astype(F32), NEG_BIG)


def _band_attention(q, kp, vp, bias):
    ns = SEQ // ATT_QS
    qspec = pl.BlockSpec((ATT_QS, MXU_DIM), lambda b, g, n: (b * ns + n, g))
    kvspec = pl.BlockSpec((None, SEQ + LEFT_PAD, MXU_DIM), lambda b, g, n: (b, 0, g))
    return pl.pallas_call(
        _attn_kernel,
        grid=(BATCH, ATT_HEADS // ATT_HG, ns),
        in_specs=[qspec, kvspec, kvspec,
                  pl.BlockSpec((ATT_HG, ATT_QB, ATT_KB), lambda b, g, n: (g, 0, 0))],
        out_specs=qspec,
        out_shape=jax.ShapeDtypeStruct((TOKENS, D_MODEL), BF16),
        compiler_params=_params(("parallel", "parallel", "arbitrary")),
        name="band_attn",
    )(q, kp, vp, bias)


def _post_b_kernel(h_ref, o_ref, wo_ref, nf_ref, wr_ref, h_out_ref, xn_ref, gates_ref):
    h = h_ref[...] + _dot(o_ref[...], wo_ref[...])
    h_out_ref[...] = h
    xn = _rms(h, nf_ref[...])
    xn_ref[...] = xn.astype(BF16)
    logits = jnp.dot(xn, wr_ref[...], preferred_element_type=F32, precision=lax.Precision.HIGHEST)
    lane = lax.broadcasted_iota(jnp.int32, logits.shape, 1)
    logits = jnp.where(lane < N_EXPERTS, logits, -jnp.inf)
    m1 = jnp.max(logits, axis=-1, keepdims=True)
    i1 = jnp.min(jnp.where(logits == m1, lane, LANES), axis=-1, keepdims=True)
    rest = jnp.where(lane == i1, -jnp.inf, logits)
    m2 = jnp.max(rest, axis=-1, keepdims=True)
    i2 = jnp.min(jnp.where(rest == m2, lane, LANES), axis=-1, keepdims=True)
    e2 = jnp.exp(m2 - m1)
    w1 = 1.0 / (1.0 + e2)
    w2 = e2 / (1.0 + e2)
    gates_ref[...] = jnp.where(lane == i1, w1, 0.0) + jnp.where(lane == i2, w2, 0.0)


def _post_b(h, o, wo, nf, wr, tm=512):
    n = TOKENS // tm
    tile = lambda width: pl.BlockSpec((tm, width), lambda i: (i, 0))
    return pl.pallas_call(
        _post_b_kernel,
        grid=(n,),
        in_specs=[tile(D_MODEL), tile(D_MODEL), _const_spec(wo.shape), _const_spec((1, D_MODEL)),
                  _const_spec(wr.shape)],
        out_specs=[tile(D_MODEL), tile(D_MODEL), tile(LANES)],
        out_shape=[jax.ShapeDtypeStruct((TOKENS, D_MODEL), F32),
                   jax.ShapeDtypeStruct((TOKENS, D_MODEL), BF16),
                   jax.ShapeDtypeStruct((TOKENS, LANES), F32)],
        compiler_params=_params(("parallel",)),
        name="post_b",
    )(h, o, wo, nf, wr)


def _moe_kernel(x_ref, gates_ref, wg_ref, wu_ref, wd_ref, y_ref):
    e = pl.program_id(1)
    f = pl.program_id(2)

    @pl.when((e == 0) & (f == 0))
    def _():
        y_ref[...] = jnp.zeros_like(y_ref)

    x = x_ref[...]
    gates = gates_ref[...]
    lane = lax.broadcasted_iota(jnp.int32, gates.shape, 1)
    gate_e = jnp.sum(jnp.where(lane == e, gates, 0.0), axis=-1, keepdims=True)
    a = _silu(_dot(x, wg_ref[...])) * _dot(x, wu_ref[...]) * gate_e
    y_ref[...] += _dot(a.astype(BF16), wd_ref[...])


def _moe(xn, gates, wg, wu, wd, tm=512, tf=1792):
    n = TOKENS // tm
    nf = D_FF_EXPERT // tf
    return pl.pallas_call(
        _moe_kernel,
        grid=(n, N_EXPERTS, nf),
        in_specs=[pl.BlockSpec((tm, D_MODEL), lambda i, e, f: (i, 0)),
                  pl.BlockSpec((tm, LANES), lambda i, e, f: (i, 0)),
                  pl.BlockSpec((None, D_MODEL, tf), lambda i, e, f: (e, 0, f)),
                  pl.BlockSpec((None, D_MODEL, tf), lambda i, e, f: (e, 0, f)),
                  pl.BlockSpec((None, tf, D_MODEL), lambda i, e, f: (e, f, 0))],
        out_specs=pl.BlockSpec((tm, D_MODEL), lambda i, e, f: (i, 0)),
        out_shape=jax.ShapeDtypeStruct((TOKENS, D_MODEL), F32),
        compiler_params=_params(("parallel", "arbitrary", "arbitrary")),
        name="moe",
    )(xn, gates, wg, wu, wd)


def _final_kernel(h_ref, y_ref, p_ref, np_ref, wpu_ref, wpg_ref, nfin_ref, out_ref):
    h = _ple(h_ref[...] + y_ref[...], p_ref, np_ref, wpu_ref, wpg_ref)
    out_ref[...] = _rms(h, nfin_ref[...])


def _final(h, y, p, npl, wpu, wpg, nfin, tm=512):
    n = TOKENS // tm
    tile = lambda width: pl.BlockSpec((tm, width), lambda i: (i, 0))
    consts = [npl, wpu, wpg, nfin]
    return pl.pallas_call(
        _final_kernel,
        grid=(n,),
        in_specs=[tile(D_MODEL), tile(D_MODEL), tile(PLE_DIM)] + [_const_spec(c.shape) for c in consts],
        out_specs=tile(D_MODEL),
        out_shape=jax.ShapeDtypeStruct((TOKENS, D_MODEL), F32),
        compiler_params=_params(("parallel",)),
        name="final",
    )(h, y, p, *consts)


def kernel(x, p, positions, norm_mix, norm_ffn, norm_ple, w_in_a, ret_gn, w_out_a, norm_kv, w_kv, w_q_b, rel_bias, w_out_b, w_gate_dense, w_up_dense, w_down_dense, w_router, w_gate_moe, w_up_moe, w_down_moe, w_ple_up, w_ple_gate, norm_final):
    bf = lambda w: w.astype(BF16)
    row = lambda g: g.reshape(1, -1).astype(F32)
    h0 = x.reshape(TOKENS, D_MODEL)
    p2 = p.reshape(2, TOKENS, PLE_DIM)
    pos = positions.reshape(TOKENS, 1)
    inv_freq = (1.0 / (ROPE_BASE ** jnp.linspace(0.0, 1.0, ROPE_HALF, dtype=F32))).reshape(1, ROPE_HALF)

    q, k, v, gate = _inproj(h0, pos, row(norm_mix[0]), inv_freq, bf(w_in_a[0]))
    og = _retention(q, k, v, gate, row(ret_gn[0]))
    h1 = _post_a(h0, og, p2[0], bf(w_out_a[0]), row(norm_ffn[0]), bf(w_gate_dense[0]), bf(w_up_dense[0]),
                 bf(w_down_dense[0]), row(norm_ple[0]), bf(w_ple_up[0]), bf(w_ple_gate[0]))

    qb, kp, vp = _qkv(h1, row(norm_mix[1]), row(norm_kv), bf(w_q_b[0]), bf(w_kv))
    ob = _band_attention(qb, kp, vp, _band_bias(rel_bias[0]))
    w_router_pad = jnp.zeros((D_MODEL, LANES), F32).at[:, :N_EXPERTS].set(w_router[0])
    h2, xn, gates = _post_b(h1, ob, bf(w_out_b[0]), row(norm_ffn[1]), w_router_pad)
    y = _moe(xn, gates, bf(w_gate_moe[0]), bf(w_up_moe[0]), bf(w_down_moe[0]))
    out = _final(h2, y, p2[1], row(norm_ple[1]), bf(w_ple_up[1]), bf(w_ple_gate[1]), row(norm_final))
    return out.reshape(BATCH, SEQ, D_MODEL)
```

```python
import functools

import numpy as np
import jax
import jax.numpy as jnp
from jax import lax
from jax.experimental import pallas as pl
from jax.experimental.pallas import tpu as pltpu

F32 = jnp.float32
BF16 = jnp.bfloat16

D_MODEL = 1024
BATCH = 2
SEQ = 8192
TOKENS = BATCH * SEQ
CHUNK = 64
PLE_DIM = 256

RET_HEADS = 4
RET_QK_DIM = 256
RET_V_DIM = 512
RET_QK_WIDTH = RET_HEADS * RET_QK_DIM
RET_V_WIDTH = RET_HEADS * RET_V_DIM
ROPE_BASE = 10000.0
ROPE_HALF = RET_QK_DIM // 2

ATT_HEADS = 16
ATT_HEAD_DIM = 64
LEFT_CHUNKS = 8
LEFT_PAD = LEFT_CHUNKS * CHUNK
REL_CLIP = 256

D_FF_DENSE = 2816
N_EXPERTS = 8
D_FF_EXPERT = 3584
EPS = 1e-6

LANES = 128
MXU_DIM = 256
VMEM_LIMIT = 56 * 1024 * 1024

RET_SUB = MXU_DIM
RET_BLOCK = 1024
ATT_QB = 128
ATT_KB = ATT_QB + LEFT_PAD
ATT_QS = 2048
ATT_HG = MXU_DIM // ATT_HEAD_DIM
NEG_BIG = -1e30

TOP_K = 2
PACKED = D_MODEL // 2
MOE_TM = 512
MOE_CAP = TOKENS + 2 * MOE_TM
MOE_REGION_TILES = MOE_CAP // MOE_TM
MOE_MAX_TILES = TOP_K * TOKENS // MOE_TM + N_EXPERTS
SUBLANES = 8


def _dot(a, b):
    return jnp.dot(a, b, preferred_element_type=F32)


def _rms(x, g):
    return x * lax.rsqrt(jnp.mean(x * x, axis=-1, keepdims=True) + EPS) * g


def _silu(x):
    return x * jax.nn.sigmoid(x)


def _const_spec(shape):
    nd = len(shape)
    return pl.BlockSpec(shape, lambda *_: (0,) * nd, pipeline_mode=pl.Buffered(1))


def _params(sem):
    return pltpu.CompilerParams(dimension_semantics=sem, vmem_limit_bytes=VMEM_LIMIT)


def _inproj_kernel(x_ref, pos_ref, g_ref, invf_ref, w_ref, q_ref, k_ref, v_ref, gate_ref):
    xn = _rms(x_ref[...], g_ref[...]).astype(BF16)
    ang = pos_ref[...].astype(F32) * invf_ref[...]
    cos = jnp.cos(ang)
    sin = jnp.sin(ang)
    k_scale = RET_QK_DIM ** -0.5
    for h in range(RET_HEADS):
        lo = h * RET_QK_DIM
        mid = lo + ROPE_HALF
        hi = lo + RET_QK_DIM
        pq = _dot(xn, w_ref[:, lo:hi])
        x1, x2 = pq[:, :ROPE_HALF], pq[:, ROPE_HALF:]
        q_ref[:, lo:mid] = (x1 * cos - x2 * sin).astype(BF16)
        q_ref[:, mid:hi] = (x1 * sin + x2 * cos).astype(BF16)
        pk = _dot(xn, w_ref[:, RET_QK_WIDTH + lo:RET_QK_WIDTH + hi])
        x1, x2 = pk[:, :ROPE_HALF], pk[:, ROPE_HALF:]
        k_ref[:, lo:mid] = ((x1 * cos - x2 * sin) * k_scale).astype(BF16)
        k_ref[:, mid:hi] = ((x1 * sin + x2 * cos) * k_scale).astype(BF16)
    v0 = 2 * RET_QK_WIDTH
    g0 = v0 + RET_V_WIDTH
    for c in range(RET_HEADS):
        lo, hi = c * RET_V_DIM, (c + 1) * RET_V_DIM
        v_ref[:, lo:hi] = _dot(xn, w_ref[:, v0 + lo:v0 + hi]).astype(BF16)
        gate_ref[:, lo:hi] = _silu(_dot(xn, w_ref[:, g0 + lo:g0 + hi])).astype(BF16)


def _inproj(x, pos, g, invf, w, tm=512):
    n = TOKENS // tm
    tile = lambda width: pl.BlockSpec((tm, width), lambda i: (i, 0))
    return pl.pallas_call(
        _inproj_kernel,
        grid=(n,),
        in_specs=[tile(D_MODEL), tile(1), _const_spec((1, D_MODEL)), _const_spec((1, ROPE_HALF)),
                  _const_spec(w.shape)],
        out_specs=[tile(RET_QK_WIDTH), tile(RET_QK_WIDTH), tile(RET_V_WIDTH), tile(RET_V_WIDTH)],
        out_shape=[jax.ShapeDtypeStruct((TOKENS, RET_QK_WIDTH), BF16),
                   jax.ShapeDtypeStruct((TOKENS, RET_QK_WIDTH), BF16),
                   jax.ShapeDtypeStruct((TOKENS, RET_V_WIDTH), BF16),
                   jax.ShapeDtypeStruct((TOKENS, RET_V_WIDTH), BF16)],
        compiler_params=_params(("parallel",)),
        name="inproj",
    )(x, pos, g, invf, w)


def _ret_kernel(gc_ref, q_ref, k_ref, v_ref, gate_ref, dm_ref, xi_ref, zeta_ref, gn_ref,
                o_ref, state_ref):
    h = pl.program_id(1)

    @pl.when(pl.program_id(2) == 0)
    def _():
        state_ref[...] = jnp.zeros_like(state_ref)

    g_chunk = gc_ref[h]
    dm = dm_ref[0]
    xi = xi_ref[0]
    zeta = zeta_ref[0]
    gn = gn_ref[...]
    for j in range(RET_BLOCK // RET_SUB):
        sl = slice(j * RET_SUB, (j + 1) * RET_SUB)
        qj = q_ref[sl, :]
        kj = k_ref[sl, :]
        vj = v_ref[sl, :]
        st = state_ref[...]
        s = lax.dot_general(qj, kj, (((1,), (1,)), ((), ())), preferred_element_type=F32) * dm
        inner = _dot(s.astype(BF16), vj)
        cross = _dot(qj, st.astype(BF16)) * xi
        kz = (kj.astype(F32) * zeta).astype(BF16)
        upd = lax.dot_general(kz, vj, (((0,), (0,)), ((), ())), preferred_element_type=F32)
        state_ref[...] = st * g_chunk + upd
        o = inner + cross
        on = _rms(o, gn)
        o_ref[sl, :] = (on * gate_ref[sl, :].astype(F32)).astype(BF16)


def _retention(q, k, v, gate, gn):
    lg = jnp.log1p(-jnp.exp2(-5.0 - jnp.arange(RET_HEADS, dtype=F32)))
    idx = jnp.arange(RET_SUB, dtype=F32)
    diff = idx[:, None] - idx[None, :]
    causal = diff >= 0
    dmask = jnp.where(causal, jnp.exp(jnp.where(causal, diff, 0.0)[None] * lg[:, None, None]), 0.0)
    xi = jnp.exp((idx[None, :] + 1.0) * lg[:, None])[:, :, None]
    zeta = jnp.exp((RET_SUB - 1.0 - idx)[None, :] * lg[:, None])[:, :, None]
    g_chunk = jnp.exp(RET_SUB * lg)
    nb = SEQ // RET_BLOCK
    row = lambda b, h, n: (b * nb + n, h)
    per_head = lambda shape: pl.BlockSpec((1,) + shape, lambda b, h, n: (h, 0, 0))
    return pl.pallas_call(
        _ret_kernel,
        grid=(BATCH, RET_HEADS, nb),
        in_specs=[pl.BlockSpec(memory_space=pltpu.SMEM),
                  pl.BlockSpec((RET_BLOCK, RET_QK_DIM), row),
                  pl.BlockSpec((RET_BLOCK, RET_QK_DIM), row),
                  pl.BlockSpec((RET_BLOCK, RET_V_DIM), row),
                  pl.BlockSpec((RET_BLOCK, RET_V_DIM), row),
                  per_head((RET_SUB, RET_SUB)), per_head((RET_SUB, 1)), per_head((RET_SUB, 1)),
                  pl.BlockSpec((1, RET_V_DIM), lambda b, h, n: (0, h))],
        out_specs=pl.BlockSpec((RET_BLOCK, RET_V_DIM), row),
        out_shape=jax.ShapeDtypeStruct((TOKENS, RET_V_WIDTH), BF16),
        scratch_shapes=[pltpu.VMEM((RET_QK_DIM, RET_V_DIM), F32)],
        compiler_params=_params(("parallel", "parallel", "arbitrary")),
        name="retention",
    )(g_chunk, q, k, v, gate, dmask, xi, zeta, gn)


def _ple(h, p_ref, np_ref, wpu_ref, wpg_ref):
    gate = jax.nn.sigmoid(_dot(_rms(h, np_ref[...]).astype(BF16), wpg_ref[...]))
    return h + _dot(p_ref[...].astype(BF16), wpu_ref[...]) * gate


def _post_a_kernel(h_ref, og_ref, p_ref, wo_ref, nf_ref, wg_ref, wu_ref, wd_ref, np_ref, wpu_ref,
                   wpg_ref, out_ref):
    h1 = h_ref[...] + _dot(og_ref[...], wo_ref[...])
    xn = _rms(h1, nf_ref[...]).astype(BF16)
    a = (_silu(_dot(xn, wg_ref[...])) * _dot(xn, wu_ref[...])).astype(BF16)
    h2 = h1 + _dot(a, wd_ref[...])
    out_ref[...] = _ple(h2, p_ref, np_ref, wpu_ref, wpg_ref)


def _post_a(h, og, p, wo, nf, wg, wu, wd, npl, wpu, wpg, tm=256):
    n = TOKENS // tm
    tile = lambda width: pl.BlockSpec((tm, width), lambda i: (i, 0))
    consts = [wo, nf, wg, wu, wd, npl, wpu, wpg]
    return pl.pallas_call(
        _post_a_kernel,
        grid=(n,),
        in_specs=[tile(D_MODEL), tile(RET_V_WIDTH), tile(PLE_DIM)] + [_const_spec(c.shape) for c in consts],
        out_specs=tile(D_MODEL),
        out_shape=jax.ShapeDtypeStruct((TOKENS, D_MODEL), F32),
        compiler_params=_params(("parallel",)),
        name="post_a",
    )(h, og, p, *consts)


def _qkv_kernel(h_ref, gq_ref, gkv_ref, wq_ref, wkv_ref, q_ref, k_ref, v_ref):
    h = h_ref[...]
    hn = h * lax.rsqrt(jnp.mean(h * h, axis=-1, keepdims=True) + EPS)
    q = _dot((hn * gq_ref[...]).astype(BF16), wq_ref[...]) * (ATT_HEAD_DIM ** -0.5)
    q_ref[...] = q.astype(BF16)
    kv = _dot((hn * gkv_ref[...]).astype(BF16), wkv_ref[...])
    keep = (pl.program_id(1) > 0).astype(F32)
    k_ref[...] = (kv[:, :D_MODEL] * keep).astype(BF16)
    v_ref[...] = (kv[:, D_MODEL:] * keep).astype(BF16)


def _qkv(h, gq, gkv, wq, wkv):
    tm = LEFT_PAD
    nb = SEQ // tm
    src = lambda b, j: (b * nb + jnp.maximum(j - 1, 0), 0)
    pad_spec = pl.BlockSpec((None, tm, D_MODEL), lambda b, j: (b, j, 0))
    pad_shape = jax.ShapeDtypeStruct((BATCH, SEQ + LEFT_PAD, D_MODEL), BF16)
    return pl.pallas_call(
        _qkv_kernel,
        grid=(BATCH, nb + 1),
        in_specs=[pl.BlockSpec((tm, D_MODEL), src), _const_spec((1, D_MODEL)), _const_spec((1, D_MODEL)),
                  _const_spec(wq.shape), _const_spec(wkv.shape)],
        out_specs=[pl.BlockSpec((tm, D_MODEL), src), pad_spec, pad_spec],
        out_shape=[jax.ShapeDtypeStruct((TOKENS, D_MODEL), BF16), pad_shape, pad_shape],
        compiler_params=_params(("parallel", "arbitrary")),
        name="qkv_b",
    )(h, gq, gkv, wq, wkv)


def _attn_kernel(q_ref, k_ref, v_ref, bias_ref, o_ref):
    n = pl.program_id(2)
    lane = lax.broadcasted_iota(jnp.int32, (ATT_QB, MXU_DIM), 1)
    head_of_lane = lane // ATT_HEAD_DIM
    col = lax.broadcasted_iota(jnp.int32, (ATT_HG * ATT_QB, ATT_KB), 1)
    bias = bias_ref[...].reshape(ATT_HG * ATT_QB, ATT_KB)

    def body(qi, carry):
        q0 = pl.multiple_of(qi * ATT_QB, ATT_QB)
        kstart = pl.multiple_of(n * ATT_QS + qi * ATT_QB, ATT_QB)
        qb = q_ref[pl.ds(q0, ATT_QB), :]
        kb = k_ref[pl.ds(kstart, ATT_KB), :]
        vb = v_ref[pl.ds(kstart, ATT_KB), :]
        zero = jnp.zeros_like(qb)
        qs = jnp.concatenate([jnp.where(head_of_lane == g, qb, zero) for g in range(ATT_HG)], axis=0)
        s = lax.dot_general(qs, kb, (((1,), (1,)), ((), ())), preferred_element_type=F32) + bias
        s = jnp.where(col + kstart >= LEFT_PAD, s, NEG_BIG)
        m = jnp.max(s, axis=-1, keepdims=True)
        p = jnp.exp(s - m)
        l = jnp.sum(p, axis=-1, keepdims=True)
        pv = _dot(p.astype(BF16), vb) / l
        o = pv[0:ATT_QB]
        for g in range(1, ATT_HG):
            o = jnp.where(head_of_lane == g, pv[g * ATT_QB:(g + 1) * ATT_QB], o)
        o_ref[pl.ds(q0, ATT_QB), :] = o.astype(BF16)
        return carry

    lax.fori_loop(0, ATT_QS // ATT_QB, body, 0)


def _band_bias(rel_table):
    heads, n_rel = rel_table.shape
    width = 8 * LANES
    far = jnp.broadcast_to(rel_table[:, n_rel - 1:], (heads, LEFT_PAD - REL_CLIP + 1))
    falling = rel_table[:, n_rel - 2::-1]
    near = jnp.broadcast_to(rel_table[:, :1], (heads, width - ATT_QB - far.shape[1] - falling.shape[1]))
    wrap = jnp.broadcast_to(rel_table[:, n_rel - 1:], (heads, ATT_QB + 1))
    row0 = jnp.concatenate([far, falling, near, wrap], axis=1).astype(F32)
    skew = jnp.broadcast_to(row0[:, None, :], (heads, ATT_QB, width + 1)).reshape(heads, -1)
    table = skew[:, :ATT_QB * width].reshape(heads, ATT_QB, width)[:, :, :ATT_KB]
    qi = np.arange(ATT_QB)[:, None]
    kj = np.arange(ATT_KB)[None, :]
    q_chunk = qi // CHUNK
    k_chunk = kj // CHUNK - LEFT_CHUNKS
    allowed = (k_chunk <= q_chunk) & (k_chunk >= q_chunk - LEFT_CHUNKS)
    return jnp.where(jnp.asarray(allowed)[None], table, NEG_BIG)


def _band_attention(q, kp, vp, bias):
    ns = SEQ // ATT_QS
    qspec = pl.BlockSpec((ATT_QS, MXU_DIM), lambda b, g, n: (b * ns + n, g))
    kvspec = pl.BlockSpec((None, SEQ + LEFT_PAD, MXU_DIM), lambda b, g, n: (b, 0, g))
    return pl.pallas_call(
        _attn_kernel,
        grid=(BATCH, ATT_HEADS // ATT_HG, ns),
        in_specs=[qspec, kvspec, kvspec,
                  pl.BlockSpec((ATT_HG, ATT_QB, ATT_KB), lambda b, g, n: (g, 0, 0))],
        out_specs=qspec,
        out_shape=jax.ShapeDtypeStruct((TOKENS, D_MODEL), BF16),
        compiler_params=_params(("parallel", "parallel", "arbitrary")),
        name="band_attn",
    )(q, kp, vp, bias)


def _pack_pairs(x):
    lo = lax.bitcast_convert_type(x[:, :PACKED].astype(BF16).astype(F32), jnp.int32)
    hi = lax.bitcast_convert_type(x[:, PACKED:].astype(BF16).astype(F32), jnp.int32)
    return lax.shift_right_logical(lo, 16) | hi


def _unpack_pairs(w):
    lo = lax.bitcast_convert_type(lax.shift_left(w, 16), F32)
    hi = lax.bitcast_convert_type(w & jnp.int32(-65536), F32)
    return lo, hi


def _row_copy(src, src_row, dst, dst_row, sem):
    return pltpu.make_async_copy(src.at[pl.ds(src_row, 1)], dst.at[pl.ds(dst_row, 1)], sem)


def _tile_wait(hbm, vmem_tile, sem):
    pltpu.make_async_copy(hbm.at[pl.ds(0, MOE_TM)], vmem_tile, sem).wait()


def _slot(rs, which, t):
    return rs[which, t] * MOE_CAP + rs[TOP_K + which, t]


def _post_b_kernel(h_ref, o_ref, wo_ref, nf_ref, wrh_ref, wrl_ref, tri_ref,
                   h_out_ref, route_ref, gw_ref, cnt_ref, xs_hbm,
                   xbuf, carry_ref, rs_smem, cnt_smem, zero_ref, sems, tail_sem):
    i = pl.program_id(0)
    last = pl.num_programs(0) - 1
    slot = i % 2

    @pl.when(i == 0)
    def _():
        carry_ref[...] = jnp.zeros_like(carry_ref)

    h = h_ref[...] + _dot(o_ref[...], wo_ref[...])
    h_out_ref[...] = h
    xn = _rms(h, nf_ref[...])

    xh = xn.astype(BF16)
    xl = (xn - xh.astype(F32)).astype(BF16)
    logits = _dot(xh, wrh_ref[...]) + _dot(xl, wrh_ref[...]) + _dot(xh, wrl_ref[...])
    lt = logits.T[:N_EXPERTS]
    row = lax.broadcasted_iota(jnp.int32, lt.shape, 0)
    m1 = jnp.max(lt, axis=0, keepdims=True)
    i1 = jnp.min(jnp.where(lt == m1, row, N_EXPERTS), axis=0, keepdims=True)
    rest = jnp.where(row == i1, -jnp.inf, lt)
    m2 = jnp.max(rest, axis=0, keepdims=True)
    i2 = jnp.min(jnp.where(rest == m2, row, N_EXPERTS), axis=0, keepdims=True)
    e2 = jnp.exp(m2 - m1)
    w1 = 1.0 / (1.0 + e2)
    w2 = e2 / (1.0 + e2)

    member = ((row == i1) | (row == i2)).astype(F32)
    within = _dot(member.astype(BF16), tri_ref[...])
    carry = carry_ref[...]
    rank = within + carry[:, 0:1]
    r1 = jnp.sum(jnp.where(row == i1, rank, 0.0), axis=0, keepdims=True).astype(jnp.int32)
    r2 = jnp.sum(jnp.where(row == i2, rank, 0.0), axis=0, keepdims=True).astype(jnp.int32)
    carry_ref[...] = carry + jnp.sum(member, axis=1, keepdims=True)
    cnt_ref[...] = carry_ref[...].astype(jnp.int32)

    route_ref[...] = jnp.where(row == 0, i1, jnp.where(row == 1, i2, jnp.where(row == 2, r1,
                               jnp.where(row == 3, r2, 0))))
    gwt = jnp.where(row == 0, w1, jnp.where(row == 1, w2, 0.0))
    pad = jnp.zeros((LANES - N_EXPERTS, gwt.shape[1]), F32)
    gw_ref[...] = jnp.concatenate([gwt, pad], axis=0).T

    @pl.when(i >= 2)
    def _():
        for _ in range(TOP_K):
            _tile_wait(xs_hbm, xbuf.at[slot], sems.at[slot])

    xbuf[slot] = _pack_pairs(xn)
    pltpu.sync_copy(route_ref, rs_smem)

    def issue(t, carry_):
        for which in range(TOP_K):
            _row_copy(xbuf.at[slot], t, xs_hbm, _slot(rs_smem, which, t), sems.at[slot]).start()
        return carry_

    lax.fori_loop(0, MOE_TM, issue, 0, unroll=8)

    @pl.when(i == last)
    def _():
        for _ in range(TOP_K):
            _tile_wait(xs_hbm, xbuf.at[slot], sems.at[slot])

        @pl.when(last >= 1)
        def _():
            for _ in range(TOP_K):
                _tile_wait(xs_hbm, xbuf.at[1 - slot], sems.at[1 - slot])

        pltpu.sync_copy(cnt_ref, cnt_smem)
        zero_ref[...] = jnp.zeros_like(zero_ref)
        for e in range(N_EXPERTS):
            count = cnt_smem[e, 0]
            aligned = ((count + (SUBLANES - 1)) // SUBLANES) * SUBLANES
            start = pl.multiple_of(e * MOE_CAP + aligned, SUBLANES)
            pltpu.make_async_copy(zero_ref, xs_hbm.at[pl.ds(start, MOE_TM)], tail_sem).start()
            for k in range(SUBLANES - 1):
                @pl.when(count + k < aligned)
                def _():
                    _row_copy(zero_ref, 0, xs_hbm, e * MOE_CAP + count + k, tail_sem).start()
        for e in range(N_EXPERTS):
            count = cnt_smem[e, 0]
            aligned = ((count + (SUBLANES - 1)) // SUBLANES) * SUBLANES
            _tile_wait(xs_hbm, zero_ref, tail_sem)
            for k in range(SUBLANES - 1):
                @pl.when(count + k < aligned)
                def _():
                    _row_copy(zero_ref, 0, xs_hbm, 0, tail_sem).wait()


def _post_b(h, o, wo, nf, wr_hi, wr_lo):
    tm = MOE_TM
    n = TOKENS // tm
    tile = lambda width: pl.BlockSpec((tm, width), lambda i: (i, 0))
    tri = jnp.asarray(np.triu(np.ones((tm, tm), np.float32), k=1), dtype=BF16)
    return pl.pallas_call(
        _post_b_kernel,
        grid=(n,),
        in_specs=[tile(D_MODEL), tile(D_MODEL), _const_spec(wo.shape), _const_spec((1, D_MODEL)),
                  _const_spec(wr_hi.shape), _const_spec(wr_lo.shape), _const_spec(tri.shape)],
        out_specs=[tile(D_MODEL),
                   pl.BlockSpec((SUBLANES, tm), lambda i: (0, i)),
                   tile(LANES),
                   pl.BlockSpec((SUBLANES, LANES), lambda i: (0, 0)),
                   pl.BlockSpec(memory_space=pl.ANY)],
        out_shape=[jax.ShapeDtypeStruct((TOKENS, D_MODEL), F32),
                   jax.ShapeDtypeStruct((SUBLANES, TOKENS), jnp.int32),
                   jax.ShapeDtypeStruct((TOKENS, LANES), F32),
                   jax.ShapeDtypeStruct((SUBLANES, LANES), jnp.int32),
                   jax.ShapeDtypeStruct((N_EXPERTS * MOE_CAP, PACKED), jnp.int32)],
        scratch_shapes=[pltpu.VMEM((2, tm, PACKED), jnp.int32),
                        pltpu.VMEM((SUBLANES, LANES), F32),
                        pltpu.SMEM((SUBLANES, tm), jnp.int32),
                        pltpu.SMEM((SUBLANES, LANES), jnp.int32),
                        pltpu.VMEM((tm, PACKED), jnp.int32),
                        pltpu.SemaphoreType.DMA((2,)),
                        pltpu.SemaphoreType.DMA(())],
        compiler_params=_params(("arbitrary",)),
        name="post_b",
    )(h, o, wo, nf, wr_hi, wr_lo, tri)


def _moe_kernel(blk_ref, exp_ref, nact_ref, x_ref, wg_ref, wu_ref, wd_ref, y_ref, xb_ref, acc_ref):
    f = pl.program_id(1)

    @pl.when(pl.program_id(0) < nact_ref[0])
    def _():
        @pl.when(f == 0)
        def _():
            lo, hi = _unpack_pairs(x_ref[...])
            xb_ref[:, :PACKED] = lo.astype(BF16)
            xb_ref[:, PACKED:] = hi.astype(BF16)
            acc_ref[...] = jnp.zeros_like(acc_ref)

        x = xb_ref[...]
        a = (_silu(_dot(x, wg_ref[...])) * _dot(x, wu_ref[...])).astype(BF16)
        acc_ref[...] += _dot(a, wd_ref[...])

        @pl.when(f == pl.num_programs(1) - 1)
        def _():
            y_ref[...] = _pack_pairs(acc_ref[...])


def _moe(xs, tile_block, tile_expert, n_active, wg, wu, wd, tf=1792):
    tm = MOE_TM
    nf = D_FF_EXPERT // tf
    fidx = lambda j, f, nact: jnp.where(j < nact[0], f, nf - 1)
    rows = pl.BlockSpec((tm, PACKED), lambda j, f, blk, exp, nact: (blk[j], 0))
    grid_spec = pltpu.PrefetchScalarGridSpec(
        num_scalar_prefetch=3,
        grid=(MOE_MAX_TILES, nf),
        in_specs=[rows,
                  pl.BlockSpec((None, D_MODEL, tf), lambda j, f, blk, exp, nact: (exp[j], 0, fidx(j, f, nact))),
                  pl.BlockSpec((None, D_MODEL, tf), lambda j, f, blk, exp, nact: (exp[j], 0, fidx(j, f, nact))),
                  pl.BlockSpec((None, tf, D_MODEL), lambda j, f, blk, exp, nact: (exp[j], fidx(j, f, nact), 0))],
        out_specs=rows,
        scratch_shapes=[pltpu.VMEM((tm, D_MODEL), BF16), pltpu.VMEM((tm, D_MODEL), F32)],
    )
    return pl.pallas_call(
        _moe_kernel,
        grid_spec=grid_spec,
        out_shape=jax.ShapeDtypeStruct((N_EXPERTS * MOE_CAP, PACKED), jnp.int32),
        compiler_params=_params(("arbitrary", "arbitrary")),
        name="moe",
    )(tile_block, tile_expert, n_active, xs, wg, wu, wd)


def _tile_plan(counts):
    tiles = (counts + (MOE_TM - 1)) // MOE_TM
    ends = jnp.cumsum(tiles)
    n_active = ends[-1]
    j = jnp.minimum(jnp.arange(MOE_MAX_TILES, dtype=jnp.int32), n_active - 1)
    expert = jnp.sum((j[:, None] >= ends[None, :]).astype(jnp.int32), axis=1)
    first = jnp.sum(jnp.where(expert[:, None] == jnp.arange(N_EXPERTS)[None, :], (ends - tiles)[None, :], 0), axis=1)
    block = expert * MOE_REGION_TILES + (j - first)
    return block.astype(jnp.int32), expert.astype(jnp.int32), n_active.reshape(1).astype(jnp.int32)


def _final_kernel(h_ref, p_ref, route_ref, gw_ref, np_ref, wpu_ref, wpg_ref, nfin_ref, ys_hbm,
                  out_ref, ybuf, rs_smem, sem):
    pltpu.sync_copy(route_ref, rs_smem)

    def issue(t, carry_):
        for which in range(TOP_K):
            _row_copy(ys_hbm, _slot(rs_smem, which, t), ybuf.at[which], t, sem).start()
        return carry_

    lax.fori_loop(0, MOE_TM, issue, 0, unroll=8)
    for which in range(TOP_K):
        _tile_wait(ys_hbm, ybuf.at[which], sem)

    gw = gw_ref[...]
    w1, w2 = gw[:, 0:1], gw[:, 1:2]
    lo1, hi1 = _unpack_pairs(ybuf[0])
    lo2, hi2 = _unpack_pairs(ybuf[1])
    y = jnp.concatenate([w1 * lo1 + w2 * lo2, w1 * hi1 + w2 * hi2], axis=1)
    h = _ple(h_ref[...] + y, p_ref, np_ref, wpu_ref, wpg_ref)
    out_ref[...] = _rms(h, nfin_ref[...])


def _final(h, p, route, gw, npl, wpu, wpg, nfin, ys):
    tm = MOE_TM
    n = TOKENS // tm
    tile = lambda width: pl.BlockSpec((tm, width), lambda i: (i, 0))
    consts = [npl, wpu, wpg, nfin]
    return pl.pallas_call(
        _final_kernel,
        grid=(n,),
        in_specs=[tile(D_MODEL), tile(PLE_DIM), pl.BlockSpec((SUBLANES, tm), lambda i: (0, i)), tile(LANES)]
                 + [_const_spec(c.shape) for c in consts] + [pl.BlockSpec(memory_space=pl.ANY)],
        out_specs=tile(D_MODEL),
        out_shape=jax.ShapeDtypeStruct((TOKENS, D_MODEL), F32),
        scratch_shapes=[pltpu.VMEM((TOP_K, tm, PACKED), jnp.int32),
                        pltpu.SMEM((SUBLANES, tm), jnp.int32),
                        pltpu.SemaphoreType.DMA(())],
        compiler_params=_params(("arbitrary",)),
        name="final",
    )(h, p, route, gw, *consts, ys)


def kernel(x, p, positions, norm_mix, norm_ffn, norm_ple, w_in_a, ret_gn, w_out_a, norm_kv, w_kv, w_q_b, rel_bias, w_out_b, w_gate_dense, w_up_dense, w_down_dense, w_router, w_gate_moe, w_up_moe, w_down_moe, w_ple_up, w_ple_gate, norm_final):
    bf = lambda w: w.astype(BF16)
    row = lambda g: g.reshape(1, -1).astype(F32)
    h0 = x.reshape(TOKENS, D_MODEL)
    p2 = p.reshape(2, TOKENS, PLE_DIM)
    pos = positions.reshape(TOKENS, 1)
    inv_freq = (1.0 / (ROPE_BASE ** jnp.linspace(0.0, 1.0, ROPE_HALF, dtype=F32))).reshape(1, ROPE_HALF)

    q, k, v, gate = _inproj(h0, pos, row(norm_mix[0]), inv_freq, bf(w_in_a[0]))
    og = _retention(q, k, v, gate, row(ret_gn[0]))
    h1 = _post_a(h0, og, p2[0], bf(w_out_a[0]), row(norm_ffn[0]), bf(w_gate_dense[0]), bf(w_up_dense[0]),
                 bf(w_down_dense[0]), row(norm_ple[0]), bf(w_ple_up[0]), bf(w_ple_gate[0]))

    qb, kp, vp = _qkv(h1, row(norm_mix[1]), row(norm_kv), bf(w_q_b[0]), bf(w_kv))
    ob = _band_attention(qb, kp, vp, _band_bias(rel_bias[0]))
    w_router_pad = jnp.zeros((D_MODEL, LANES), F32).at[:, :N_EXPERTS].set(w_router[0])
    wr_hi = bf(w_router_pad)
    wr_lo = bf(w_router_pad - wr_hi.astype(F32))
    h2, route, gw, counts, xs = _post_b(h1, ob, bf(w_out_b[0]), row(norm_ffn[1]), wr_hi, wr_lo)
    tile_block, tile_expert, n_active = _tile_plan(counts[:, 0])
    ys = _moe(xs, tile_block, tile_expert, n_active, bf(w_gate_moe[0]), bf(w_up_moe[0]), bf(w_down_moe[0]))
    out = _final(h2, p2[1], route, gw, row(norm_ple[1]), bf(w_ple_up[1]), bf(w_ple_gate[1]), row(norm_final), ys)
    return out.reshape(BATCH, SEQ, D_MODEL)
```

```python
import functools

import numpy as np
import jax
import jax.numpy as jnp
from jax import lax
from jax.experimental import pallas as pl
from jax.experimental.pallas import tpu as pltpu

F32 = jnp.float32
BF16 = jnp.bfloat16

D_MODEL = 1024
BATCH = 2
SEQ = 8192
TOKENS = BATCH * SEQ
CHUNK = 64
PLE_DIM = 256

RET_HEADS = 4
RET_QK_DIM = 256
RET_V_DIM = 512
RET_QK_WIDTH = RET_HEADS * RET_QK_DIM
RET_V_WIDTH = RET_HEADS * RET_V_DIM
ROPE_BASE = 10000.0
ROPE_HALF = RET_QK_DIM // 2

ATT_HEADS = 16
ATT_HEAD_DIM = 64
LEFT_CHUNKS = 8
LEFT_PAD = LEFT_CHUNKS * CHUNK
REL_CLIP = 256

D_FF_DENSE = 2816
N_EXPERTS = 8
D_FF_EXPERT = 3584
EPS = 1e-6

LANES = 128
MXU_DIM = 256
VMEM_LIMIT = 56 * 1024 * 1024

RET_SUB = MXU_DIM
RET_BLOCK = 1024
ATT_QB = 128
ATT_KB = ATT_QB + LEFT_PAD
ATT_QS = 2048
ATT_HG = MXU_DIM // ATT_HEAD_DIM
NEG_BIG = -1e30

TOP_K = 2
MOE_TM = 512
MOE_CAP = TOKENS + 2 * MOE_TM
MOE_REGION_TILES = MOE_CAP // MOE_TM
MOE_MAX_TILES = TOP_K * TOKENS // MOE_TM + N_EXPERTS
MOE_SPARE_ROW = N_EXPERTS * MOE_CAP
MOE_ROWS = MOE_SPARE_ROW + TOP_K * MOE_TM
SUBLANES = 8


def _dot(a, b):
    return jnp.dot(a, b, preferred_element_type=F32)


def _rms(x, g):
    return x * lax.rsqrt(jnp.mean(x * x, axis=-1, keepdims=True) + EPS) * g


def _silu(x):
    return x * jax.nn.sigmoid(x)


def _const_spec(shape):
    nd = len(shape)
    return pl.BlockSpec(shape, lambda *_: (0,) * nd, pipeline_mode=pl.Buffered(1))


def _params(sem):
    return pltpu.CompilerParams(dimension_semantics=sem, vmem_limit_bytes=VMEM_LIMIT)


def _inproj_kernel(x_ref, pos_ref, g_ref, invf_ref, w_ref, q_ref, k_ref, v_ref, gate_ref):
    xn = _rms(x_ref[...], g_ref[...]).astype(BF16)
    ang = pos_ref[...].astype(F32) * invf_ref[...]
    cos = jnp.cos(ang)
    sin = jnp.sin(ang)
    k_scale = RET_QK_DIM ** -0.5
    for h in range(RET_HEADS):
        lo = h * RET_QK_DIM
        mid = lo + ROPE_HALF
        hi = lo + RET_QK_DIM
        pq = _dot(xn, w_ref[:, lo:hi])
        x1, x2 = pq[:, :ROPE_HALF], pq[:, ROPE_HALF:]
        q_ref[:, lo:mid] = (x1 * cos - x2 * sin).astype(BF16)
        q_ref[:, mid:hi] = (x1 * sin + x2 * cos).astype(BF16)
        pk = _dot(xn, w_ref[:, RET_QK_WIDTH + lo:RET_QK_WIDTH + hi])
        x1, x2 = pk[:, :ROPE_HALF], pk[:, ROPE_HALF:]
        k_ref[:, lo:mid] = ((x1 * cos - x2 * sin) * k_scale).astype(BF16)
        k_ref[:, mid:hi] = ((x1 * sin + x2 * cos) * k_scale).astype(BF16)
    v0 = 2 * RET_QK_WIDTH
    g0 = v0 + RET_V_WIDTH
    for c in range(RET_HEADS):
        lo, hi = c * RET_V_DIM, (c + 1) * RET_V_DIM
        v_ref[:, lo:hi] = _dot(xn, w_ref[:, v0 + lo:v0 + hi]).astype(BF16)
        gate_ref[:, lo:hi] = _silu(_dot(xn, w_ref[:, g0 + lo:g0 + hi])).astype(BF16)


def _inproj(x, pos, g, invf, w, tm=512):
    n = TOKENS // tm
    tile = lambda width: pl.BlockSpec((tm, width), lambda i: (i, 0))
    return pl.pallas_call(
        _inproj_kernel,
        grid=(n,),
        in_specs=[tile(D_MODEL), tile(1), _const_spec((1, D_MODEL)), _const_spec((1, ROPE_HALF)),
                  _const_spec(w.shape)],
        out_specs=[tile(RET_QK_WIDTH), tile(RET_QK_WIDTH), tile(RET_V_WIDTH), tile(RET_V_WIDTH)],
        out_shape=[jax.ShapeDtypeStruct((TOKENS, RET_QK_WIDTH), BF16),
                   jax.ShapeDtypeStruct((TOKENS, RET_QK_WIDTH), BF16),
                   jax.ShapeDtypeStruct((TOKENS, RET_V_WIDTH), BF16),
                   jax.ShapeDtypeStruct((TOKENS, RET_V_WIDTH), BF16)],
        compiler_params=_params(("parallel",)),
        name="inproj",
    )(x, pos, g, invf, w)


def _ret_kernel(gc_ref, q_ref, k_ref, v_ref, gate_ref, dm_ref, xi_ref, zeta_ref, gn_ref,
                o_ref, state_ref):
    h = pl.program_id(1)

    @pl.when(pl.program_id(2) == 0)
    def _():
        state_ref[...] = jnp.zeros_like(state_ref)

    g_chunk = gc_ref[h]
    dm = dm_ref[0]
    xi = xi_ref[0]
    zeta = zeta_ref[0]
    gn = gn_ref[...]
    for j in range(RET_BLOCK // RET_SUB):
        sl = slice(j * RET_SUB, (j + 1) * RET_SUB)
        qj = q_ref[sl, :]
        kj = k_ref[sl, :]
        vj = v_ref[sl, :]
        st = state_ref[...]
        s = lax.dot_general(qj, kj, (((1,), (1,)), ((), ())), preferred_element_type=F32) * dm
        inner = _dot(s.astype(BF16), vj)
        cross = _dot(qj, st.astype(BF16)) * xi
        kz = (kj.astype(F32) * zeta).astype(BF16)
        upd = lax.dot_general(kz, vj, (((0,), (0,)), ((), ())), preferred_element_type=F32)
        state_ref[...] = st * g_chunk + upd
        o = inner + cross
        on = _rms(o, gn)
        o_ref[sl, :] = (on * gate_ref[sl, :].astype(F32)).astype(BF16)


def _retention(q, k, v, gate, gn):
    lg = jnp.log1p(-jnp.exp2(-5.0 - jnp.arange(RET_HEADS, dtype=F32)))
    idx = jnp.arange(RET_SUB, dtype=F32)
    diff = idx[:, None] - idx[None, :]
    causal = diff >= 0
    dmask = jnp.where(causal, jnp.exp(jnp.where(causal, diff, 0.0)[None] * lg[:, None, None]), 0.0)
    xi = jnp.exp((idx[None, :] + 1.0) * lg[:, None])[:, :, None]
    zeta = jnp.exp((RET_SUB - 1.0 - idx)[None, :] * lg[:, None])[:, :, None]
    g_chunk = jnp.exp(RET_SUB * lg)
    nb = SEQ // RET_BLOCK
    row = lambda b, h, n: (b * nb + n, h)
    per_head = lambda shape: pl.BlockSpec((1,) + shape, lambda b, h, n: (h, 0, 0))
    return pl.pallas_call(
        _ret_kernel,
        grid=(BATCH, RET_HEADS, nb),
        in_specs=[pl.BlockSpec(memory_space=pltpu.SMEM),
                  pl.BlockSpec((RET_BLOCK, RET_QK_DIM), row),
                  pl.BlockSpec((RET_BLOCK, RET_QK_DIM), row),
                  pl.BlockSpec((RET_BLOCK, RET_V_DIM), row),
                  pl.BlockSpec((RET_BLOCK, RET_V_DIM), row),
                  per_head((RET_SUB, RET_SUB)), per_head((RET_SUB, 1)), per_head((RET_SUB, 1)),
                  pl.BlockSpec((1, RET_V_DIM), lambda b, h, n: (0, h))],
        out_specs=pl.BlockSpec((RET_BLOCK, RET_V_DIM), row),
        out_shape=jax.ShapeDtypeStruct((TOKENS, RET_V_WIDTH), BF16),
        scratch_shapes=[pltpu.VMEM((RET_QK_DIM, RET_V_DIM), F32)],
        compiler_params=_params(("parallel", "parallel", "arbitrary")),
        name="retention",
    )(g_chunk, q, k, v, gate, dmask, xi, zeta, gn)


def _ple(h, p_ref, np_ref, wpu_ref, wpg_ref):
    gate = jax.nn.sigmoid(_dot(_rms(h, np_ref[...]).astype(BF16), wpg_ref[...]))
    return h + _dot(p_ref[...].astype(BF16), wpu_ref[...]) * gate


def _post_a_kernel(h_ref, og_ref, p_ref, wo_ref, nf_ref, wg_ref, wu_ref, wd_ref, np_ref, wpu_ref,
                   wpg_ref, out_ref):
    h1 = h_ref[...] + _dot(og_ref[...], wo_ref[...])
    xn = _rms(h1, nf_ref[...]).astype(BF16)
    a = (_silu(_dot(xn, wg_ref[...])) * _dot(xn, wu_ref[...])).astype(BF16)
    h2 = h1 + _dot(a, wd_ref[...])
    out_ref[...] = _ple(h2, p_ref, np_ref, wpu_ref, wpg_ref)


def _post_a(h, og, p, wo, nf, wg, wu, wd, npl, wpu, wpg, tm=256):
    n = TOKENS // tm
    tile = lambda width: pl.BlockSpec((tm, width), lambda i: (i, 0))
    consts = [wo, nf, wg, wu, wd, npl, wpu, wpg]
    return pl.pallas_call(
        _post_a_kernel,
        grid=(n,),
        in_specs=[tile(D_MODEL), tile(RET_V_WIDTH), tile(PLE_DIM)] + [_const_spec(c.shape) for c in consts],
        out_specs=tile(D_MODEL),
        out_shape=jax.ShapeDtypeStruct((TOKENS, D_MODEL), F32),
        compiler_params=_params(("parallel",)),
        name="post_a",
    )(h, og, p, *consts)


def _qkv_kernel(h_ref, gq_ref, gkv_ref, wq_ref, wkv_ref, q_ref, k_ref, v_ref):
    h = h_ref[...]
    hn = h * lax.rsqrt(jnp.mean(h * h, axis=-1, keepdims=True) + EPS)
    q = _dot((hn * gq_ref[...]).astype(BF16), wq_ref[...]) * (ATT_HEAD_DIM ** -0.5)
    q_ref[...] = q.astype(BF16)
    kv = _dot((hn * gkv_ref[...]).astype(BF16), wkv_ref[...])
    keep = (pl.program_id(1) > 0).astype(F32)
    k_ref[...] = (kv[:, :D_MODEL] * keep).astype(BF16)
    v_ref[...] = (kv[:, D_MODEL:] * keep).astype(BF16)


def _qkv(h, gq, gkv, wq, wkv):
    tm = LEFT_PAD
    nb = SEQ // tm
    src = lambda b, j: (b * nb + jnp.maximum(j - 1, 0), 0)
    pad_spec = pl.BlockSpec((None, tm, D_MODEL), lambda b, j: (b, j, 0))
    pad_shape = jax.ShapeDtypeStruct((BATCH, SEQ + LEFT_PAD, D_MODEL), BF16)
    return pl.pallas_call(
        _qkv_kernel,
        grid=(BATCH, nb + 1),
        in_specs=[pl.BlockSpec((tm, D_MODEL), src), _const_spec((1, D_MODEL)), _const_spec((1, D_MODEL)),
                  _const_spec(wq.shape), _const_spec(wkv.shape)],
        out_specs=[pl.BlockSpec((tm, D_MODEL), src), pad_spec, pad_spec],
        out_shape=[jax.ShapeDtypeStruct((TOKENS, D_MODEL), BF16), pad_shape, pad_shape],
        compiler_params=_params(("parallel", "arbitrary")),
        name="qkv_b",
    )(h, gq, gkv, wq, wkv)


def _attn_kernel(q_ref, k_ref, v_ref, bias_ref, o_ref):
    n = pl.program_id(2)
    lane = lax.broadcasted_iota(jnp.int32, (ATT_QB, MXU_DIM), 1)
    head_of_lane = lane // ATT_HEAD_DIM
    col = lax.broadcasted_iota(jnp.int32, (ATT_HG * ATT_QB, ATT_KB), 1)
    bias = bias_ref[...].reshape(ATT_HG * ATT_QB, ATT_KB)

    def body(qi, carry):
        q0 = pl.multiple_of(qi * ATT_QB, ATT_QB)
        kstart = pl.multiple_of(n * ATT_QS + qi * ATT_QB, ATT_QB)
        qb = q_ref[pl.ds(q0, ATT_QB), :]
        kb = k_ref[pl.ds(kstart, ATT_KB), :]
        vb = v_ref[pl.ds(kstart, ATT_KB), :]
        zero = jnp.zeros_like(qb)
        qs = jnp.concatenate([jnp.where(head_of_lane == g, qb, zero) for g in range(ATT_HG)], axis=0)
        s = lax.dot_general(qs, kb, (((1,), (1,)), ((), ())), preferred_element_type=F32) + bias
        s = jnp.where(col + kstart >= LEFT_PAD, s, NEG_BIG)
        m = jnp.max(s, axis=-1, keepdims=True)
        p = jnp.exp(s - m)
        l = jnp.sum(p, axis=-1, keepdims=True)
        pv = _dot(p.astype(BF16), vb) / l
        o = pv[0:ATT_QB]
        for g in range(1, ATT_HG):
            o = jnp.where(head_of_lane == g, pv[g * ATT_QB:(g + 1) * ATT_QB], o)
        o_ref[pl.ds(q0, ATT_QB), :] = o.astype(BF16)
        return carry

    lax.fori_loop(0, ATT_QS // ATT_QB, body, 0)


def _band_bias(rel_table):
    heads, n_rel = rel_table.shape
    width = 8 * LANES
    far = jnp.broadcast_to(rel_table[:, n_rel - 1:], (heads, LEFT_PAD - REL_CLIP + 1))
    falling = rel_table[:, n_rel - 2::-1]
    near = jnp.broadcast_to(rel_table[:, :1], (heads, width - ATT_QB - far.shape[1] - falling.shape[1]))
    wrap = jnp.broadcast_to(rel_table[:, n_rel - 1:], (heads, ATT_QB + 1))
    row0 = jnp.concatenate([far, falling, near, wrap], axis=1).astype(F32)
    skew = jnp.broadcast_to(row0[:, None, :], (heads, ATT_QB, width + 1)).reshape(heads, -1)
    table = skew[:, :ATT_QB * width].reshape(heads, ATT_QB, width)[:, :, :ATT_KB]
    qi = np.arange(ATT_QB)[:, None]
    kj = np.arange(ATT_KB)[None, :]
    q_chunk = qi // CHUNK
    k_chunk = kj // CHUNK - LEFT_CHUNKS
    allowed = (k_chunk <= q_chunk) & (k_chunk >= q_chunk - LEFT_CHUNKS)
    return jnp.where(jnp.asarray(allowed)[None], table, NEG_BIG)


def _band_attention(q, kp, vp, bias):
    ns = SEQ // ATT_QS
    qspec = pl.BlockSpec((ATT_QS, MXU_DIM), lambda b, g, n: (b * ns + n, g))
    kvspec = pl.BlockSpec((None, SEQ + LEFT_PAD, MXU_DIM), lambda b, g, n: (b, 0, g))
    return pl.pallas_call(
        _attn_kernel,
        grid=(BATCH, ATT_HEADS // ATT_HG, ns),
        in_specs=[qspec, kvspec, kvspec,
                  pl.BlockSpec((ATT_HG, ATT_QB, ATT_KB), lambda b, g, n: (g, 0, 0))],
        out_specs=qspec,
        out_shape=jax.ShapeDtypeStruct((TOKENS, D_MODEL), BF16),
        compiler_params=_params(("parallel", "parallel", "arbitrary")),
        name="band_attn",
    )(q, kp, vp, bias)


def _row_copy(src, src_row, dst, dst_row, sem):
    return pltpu.make_async_copy(src.at[pl.ds(src_row, 1)], dst.at[pl.ds(dst_row, 1)], sem)


def _tile_wait(hbm, vmem_tile, sem):
    pltpu.make_async_copy(hbm.at[pl.ds(0, MOE_TM)], vmem_tile, sem).wait()


def _scatter_tile(xbuf, rs_smem, xs_hbm, sem):
    for t in range(MOE_TM):
        for which in range(TOP_K):
            _row_copy(xbuf, t, xs_hbm, rs_smem[which, t], sem).start()


def _route_tile(rows, h_ref, o_ref, wo_ref, nf_ref, wrh_ref, wrl_ref, tri_ref,
                h_out_ref, route_ref, gw_ref, cnt_ref, carry_ref):
    cols = rows
    h = h_ref[rows, :] + _dot(o_ref[rows, :], wo_ref[...])
    h_out_ref[rows, :] = h
    xn = _rms(h, nf_ref[...])

    xh = xn.astype(BF16)
    xl = (xn - xh.astype(F32)).astype(BF16)
    logits = _dot(xh, wrh_ref[...]) + _dot(xl, wrh_ref[...]) + _dot(xh, wrl_ref[...])
    lt = logits.T[:N_EXPERTS]
    row = lax.broadcasted_iota(jnp.int32, lt.shape, 0)
    m1 = jnp.max(lt, axis=0, keepdims=True)
    i1 = jnp.min(jnp.where(lt == m1, row, N_EXPERTS), axis=0, keepdims=True)
    rest = jnp.where(row == i1, -jnp.inf, lt)
    m2 = jnp.max(rest, axis=0, keepdims=True)
    i2 = jnp.min(jnp.where(rest == m2, row, N_EXPERTS), axis=0, keepdims=True)
    e2 = jnp.exp(m2 - m1)
    w1 = 1.0 / (1.0 + e2)
    w2 = e2 / (1.0 + e2)

    member = ((row == i1) | (row == i2)).astype(F32)
    within = _dot(member.astype(BF16), tri_ref[...])
    carry = carry_ref[...]
    rank = within + carry[:, 0:1]
    r1 = jnp.sum(jnp.where(row == i1, rank, 0.0), axis=0, keepdims=True).astype(jnp.int32)
    r2 = jnp.sum(jnp.where(row == i2, rank, 0.0), axis=0, keepdims=True).astype(jnp.int32)
    carry_ref[...] = carry + jnp.sum(member, axis=1, keepdims=True)
    cnt_ref[...] = carry_ref[...].astype(jnp.int32)

    s1 = i1 * MOE_CAP + r1
    s2 = i2 * MOE_CAP + r2
    route_ref[:, cols] = jnp.where(row == 0, s1, jnp.where(row == 1, s2, 0))
    gwt = jnp.where(row == 0, w1, jnp.where(row == 1, w2, 0.0))
    pad = jnp.zeros((LANES - N_EXPERTS, gwt.shape[1]), F32)
    gw_ref[rows, :] = jnp.concatenate([gwt, pad], axis=0).T
    return xn


def _post_b_kernel(h_ref, o_ref, wo_ref, nf_ref, wrh_ref, wrl_ref, tri_ref,
                   h_out_ref, route_ref, gw_ref, cnt_ref, xs_hbm,
                   xbuf0, xbuf1, carry_ref, rs0, rs1, cnt_smem, zero_ref, sems, tail_sem):
    g = pl.program_id(0)
    tile_args = (h_ref, o_ref, wo_ref, nf_ref, wrh_ref, wrl_ref, tri_ref,
                 h_out_ref, route_ref, gw_ref, cnt_ref, carry_ref)
    first = slice(0, MOE_TM)
    second = slice(MOE_TM, 2 * MOE_TM)

    @pl.when(g == 0)
    def _():
        carry_ref[...] = jnp.zeros_like(carry_ref)
        xbuf1[...] = jnp.zeros_like(xbuf1)
        which = lax.broadcasted_iota(jnp.int32, (SUBLANES, MOE_TM), 0)
        token = lax.broadcasted_iota(jnp.int32, (SUBLANES, MOE_TM), 1)
        route_ref[:, first] = MOE_SPARE_ROW + jnp.minimum(which, TOP_K - 1) * MOE_TM + token
        pltpu.sync_copy(route_ref.at[:, first], rs1)

    _scatter_tile(xbuf1, rs1, xs_hbm, sems.at[1])
    xn = _route_tile(first, *tile_args)

    @pl.when(g >= 1)
    def _():
        for _ in range(TOP_K):
            _tile_wait(xs_hbm, xbuf0, sems.at[0])

    xbuf0[...] = xn
    pltpu.sync_copy(route_ref.at[:, first], rs0)

    _scatter_tile(xbuf0, rs0, xs_hbm, sems.at[0])
    xn = _route_tile(second, *tile_args)
    for _ in range(TOP_K):
        _tile_wait(xs_hbm, xbuf1, sems.at[1])
    xbuf1[...] = xn
    pltpu.sync_copy(route_ref.at[:, second], rs1)

    @pl.when(g == pl.num_programs(0) - 1)
    def _():
        _scatter_tile(xbuf1, rs1, xs_hbm, sems.at[1])
        for buf, sem in ((xbuf0, sems.at[0]), (xbuf1, sems.at[1])):
            for _ in range(TOP_K):
                _tile_wait(xs_hbm, buf, sem)

        pltpu.sync_copy(cnt_ref, cnt_smem)
        zero_ref[...] = jnp.zeros_like(zero_ref)
        for e in range(N_EXPERTS):
            count = cnt_smem[e, 0]
            aligned = ((count + (SUBLANES - 1)) // SUBLANES) * SUBLANES
            start = pl.multiple_of(e * MOE_CAP + aligned, SUBLANES)
            pltpu.make_async_copy(zero_ref, xs_hbm.at[pl.ds(start, MOE_TM)], tail_sem).start()
            for k in range(SUBLANES - 1):
                @pl.when(count + k < aligned)
                def _():
                    _row_copy(zero_ref, 0, xs_hbm, e * MOE_CAP + count + k, tail_sem).start()
        for e in range(N_EXPERTS):
            count = cnt_smem[e, 0]
            aligned = ((count + (SUBLANES - 1)) // SUBLANES) * SUBLANES
            _tile_wait(xs_hbm, zero_ref, tail_sem)
            for k in range(SUBLANES - 1):
                @pl.when(count + k < aligned)
                def _():
                    _row_copy(zero_ref, 0, xs_hbm, 0, tail_sem).wait()


def _post_b(h, o, wo, nf, wr_hi, wr_lo):
    tm = MOE_TM
    pair = 2 * tm
    n = TOKENS // pair
    tile = lambda width: pl.BlockSpec((pair, width), lambda i: (i, 0))
    tri = jnp.asarray(np.triu(np.ones((tm, tm), np.float32), k=1), dtype=BF16)
    return pl.pallas_call(
        _post_b_kernel,
        grid=(n,),
        in_specs=[tile(D_MODEL), tile(D_MODEL), _const_spec(wo.shape), _const_spec((1, D_MODEL)),
                  _const_spec(wr_hi.shape), _const_spec(wr_lo.shape), _const_spec(tri.shape)],
        out_specs=[tile(D_MODEL),
                   pl.BlockSpec((SUBLANES, pair), lambda i: (0, i)),
                   tile(LANES),
                   pl.BlockSpec((SUBLANES, LANES), lambda i: (0, 0)),
                   pl.BlockSpec(memory_space=pl.ANY)],
        out_shape=[jax.ShapeDtypeStruct((TOKENS, D_MODEL), F32),
                   jax.ShapeDtypeStruct((SUBLANES, TOKENS), jnp.int32),
                   jax.ShapeDtypeStruct((TOKENS, LANES), F32),
                   jax.ShapeDtypeStruct((SUBLANES, LANES), jnp.int32),
                   jax.ShapeDtypeStruct((MOE_ROWS, D_MODEL), F32)],
        scratch_shapes=[pltpu.VMEM((tm, D_MODEL), F32),
                        pltpu.VMEM((tm, D_MODEL), F32),
                        pltpu.VMEM((SUBLANES, LANES), F32),
                        pltpu.SMEM((SUBLANES, tm), jnp.int32),
                        pltpu.SMEM((SUBLANES, tm), jnp.int32),
                        pltpu.SMEM((SUBLANES, LANES), jnp.int32),
                        pltpu.VMEM((tm, D_MODEL), F32),
                        pltpu.SemaphoreType.DMA((2,)),
                        pltpu.SemaphoreType.DMA(())],
        compiler_params=_params(("arbitrary",)),
        name="post_b",
    )(h, o, wo, nf, wr_hi, wr_lo, tri)


def _moe_kernel(blk_ref, exp_ref, nact_ref, x_ref, wg_ref, wu_ref, wd_ref, y_ref, xb_ref):
    f = pl.program_id(1)

    @pl.when(pl.program_id(0) < nact_ref[0])
    def _():
        @pl.when(f == 0)
        def _():
            xb_ref[...] = x_ref[...].astype(BF16)
            y_ref[...] = jnp.zeros_like(y_ref)

        x = xb_ref[...]
        a = (_silu(_dot(x, wg_ref[...])) * _dot(x, wu_ref[...])).astype(BF16)
        y_ref[...] += _dot(a, wd_ref[...])


def _moe(xs, tile_block, tile_expert, n_active, wg, wu, wd, tf=1792):
    tm = MOE_TM
    nf = D_FF_EXPERT // tf
    fidx = lambda j, f, nact: jnp.where(j < nact[0], f, nf - 1)
    rows = pl.BlockSpec((tm, D_MODEL), lambda j, f, blk, exp, nact: (blk[j], 0))
    grid_spec = pltpu.PrefetchScalarGridSpec(
        num_scalar_prefetch=3,
        grid=(MOE_MAX_TILES, nf),
        in_specs=[rows,
                  pl.BlockSpec((None, D_MODEL, tf), lambda j, f, blk, exp, nact: (exp[j], 0, fidx(j, f, nact))),
                  pl.BlockSpec((None, D_MODEL, tf), lambda j, f, blk, exp, nact: (exp[j], 0, fidx(j, f, nact))),
                  pl.BlockSpec((None, tf, D_MODEL), lambda j, f, blk, exp, nact: (exp[j], fidx(j, f, nact), 0))],
        out_specs=rows,
        scratch_shapes=[pltpu.VMEM((tm, D_MODEL), BF16)],
    )
    return pl.pallas_call(
        _moe_kernel,
        grid_spec=grid_spec,
        out_shape=jax.ShapeDtypeStruct((MOE_ROWS, D_MODEL), F32),
        compiler_params=_params(("arbitrary", "arbitrary")),
        name="moe",
    )(tile_block, tile_expert, n_active, xs, wg, wu, wd)


def _tile_plan(counts):
    tiles = (counts + (MOE_TM - 1)) // MOE_TM
    ends = jnp.cumsum(tiles)
    n_active = ends[-1]
    j = jnp.minimum(jnp.arange(MOE_MAX_TILES, dtype=jnp.int32), n_active - 1)
    expert = jnp.sum((j[:, None] >= ends[None, :]).astype(jnp.int32), axis=1)
    first = jnp.sum(jnp.where(expert[:, None] == jnp.arange(N_EXPERTS)[None, :], (ends - tiles)[None, :], 0), axis=1)
    block = expert * MOE_REGION_TILES + (j - first)
    return block.astype(jnp.int32), expert.astype(jnp.int32), n_active.reshape(1).astype(jnp.int32)


def _gather_tile(ys_hbm, rs_smem, ybuf, sem):
    for t in range(MOE_TM):
        for which in range(TOP_K):
            _row_copy(ys_hbm, rs_smem[which, t], ybuf.at[which], t, sem).start()


def _final_kernel(h_ref, p_ref, route_ref, route_next_ref, gw_ref, np_ref, wpu_ref, wpg_ref, nfin_ref,
                  ys_hbm, out_ref, ybuf0, ybuf1, rs_smem, sems):
    first = slice(0, MOE_TM)
    second = slice(MOE_TM, 2 * MOE_TM)

    def combine(rows, ybuf):
        gw = gw_ref[rows, :]
        y = gw[:, 0:1] * ybuf[0] + gw[:, 1:2] * ybuf[1]
        hp = h_ref[rows, :] + y
        gate = jax.nn.sigmoid(_dot(_rms(hp, np_ref[...]).astype(BF16), wpg_ref[...]))
        hp = hp + _dot(p_ref[rows, :].astype(BF16), wpu_ref[...]) * gate
        out_ref[rows, :] = _rms(hp, nfin_ref[...])

    @pl.when(pl.program_id(0) == 0)
    def _():
        pltpu.sync_copy(route_ref.at[:, first], rs_smem)
        _gather_tile(ys_hbm, rs_smem, ybuf0, sems.at[0])

    for which in range(TOP_K):
        _tile_wait(ys_hbm, ybuf0.at[which], sems.at[0])
    pltpu.sync_copy(route_ref.at[:, second], rs_smem)
    _gather_tile(ys_hbm, rs_smem, ybuf1, sems.at[1])
    combine(first, ybuf0)

    for which in range(TOP_K):
        _tile_wait(ys_hbm, ybuf1.at[which], sems.at[1])
    pltpu.sync_copy(route_next_ref, rs_smem)
    _gather_tile(ys_hbm, rs_smem, ybuf0, sems.at[0])
    combine(second, ybuf1)

    @pl.when(pl.program_id(0) == pl.num_programs(0) - 1)
    def _():
        for which in range(TOP_K):
            _tile_wait(ys_hbm, ybuf0.at[which], sems.at[0])


def _final(h, p, route, gw, npl, wpu, wpg, nfin, ys):
    tm = MOE_TM
    pair = 2 * tm
    n = TOKENS // pair
    tile = lambda width: pl.BlockSpec((pair, width), lambda i: (i, 0))
    consts = [npl, wpu, wpg, nfin]
    next_tile = lambda i: (0, jnp.minimum(2 * i + 2, TOKENS // tm - 1))
    return pl.pallas_call(
        _final_kernel,
        grid=(n,),
        in_specs=[tile(D_MODEL), tile(PLE_DIM),
                  pl.BlockSpec((SUBLANES, pair), lambda i: (0, i)),
                  pl.BlockSpec((SUBLANES, tm), next_tile),
                  tile(LANES)]
                 + [_const_spec(c.shape) for c in consts] + [pl.BlockSpec(memory_space=pl.ANY)],
        out_specs=tile(D_MODEL),
        out_shape=jax.ShapeDtypeStruct((TOKENS, D_MODEL), F32),
        scratch_shapes=[pltpu.VMEM((TOP_K, tm, D_MODEL), F32),
                        pltpu.VMEM((TOP_K, tm, D_MODEL), F32),
                        pltpu.SMEM((SUBLANES, tm), jnp.int32),
                        pltpu.SemaphoreType.DMA((2,))],
        compiler_params=_params(("arbitrary",)),
        name="final",
    )(h, p, route, route, gw, *consts, ys)


def kernel(x, p, positions, norm_mix, norm_ffn, norm_ple, w_in_a, ret_gn, w_out_a, norm_kv, w_kv, w_q_b, rel_bias, w_out_b, w_gate_dense, w_up_dense, w_down_dense, w_router, w_gate_moe, w_up_moe, w_down_moe, w_ple_up, w_ple_gate, norm_final):
    bf = lambda w: w.astype(BF16)
    row = lambda g: g.reshape(1, -1).astype(F32)
    h0 = x.reshape(TOKENS, D_MODEL)
    p2 = p.reshape(2, TOKENS, PLE_DIM)
    pos = positions.reshape(TOKENS, 1)
    inv_freq = (1.0 / (ROPE_BASE ** jnp.linspace(0.0, 1.0, ROPE_HALF, dtype=F32))).reshape(1, ROPE_HALF)

    q, k, v, gate = _inproj(h0, pos, row(norm_mix[0]), inv_freq, bf(w_in_a[0]))
    og = _retention(q, k, v, gate, row(ret_gn[0]))
    h1 = _post_a(h0, og, p2[0], bf(w_out_a[0]), row(norm_ffn[0]), bf(w_gate_dense[0]), bf(w_up_dense[0]),
                 bf(w_down_dense[0]), row(norm_ple[0]), bf(w_ple_up[0]), bf(w_ple_gate[0]))

    qb, kp, vp = _qkv(h1, row(norm_mix[1]), row(norm_kv), bf(w_q_b[0]), bf(w_kv))
    ob = _band_attention(qb, kp, vp, _band_bias(rel_bias[0]))
    w_router_pad = jnp.zeros((D_MODEL, LANES), F32).at[:, :N_EXPERTS].set(w_router[0])
    wr_hi = bf(w_router_pad)
    wr_lo = bf(w_router_pad - wr_hi.astype(F32))
    h2, route, gw, counts, xs = _post_b(h1, ob, bf(w_out_b[0]), row(norm_ffn[1]), wr_hi, wr_lo)
    tile_block, tile_expert, n_active = _tile_plan(counts[:, 0])
    ys = _moe(xs, tile_block, tile_expert, n_active, bf(w_gate_moe[0]), bf(w_up_moe[0]), bf(w_down_moe[0]))
    out = _final(h2, p2[1], route, gw, row(norm_ple[1]), bf(w_ple_up[1]), bf(w_ple_gate[1]), row(norm_final), ys)
    return out.reshape(BATCH, SEQ, D_MODEL)
```

```python
import functools

import numpy as np
import jax
import jax.numpy as jnp
from jax import lax
from jax.experimental import pallas as pl
from jax.experimental.pallas import tpu as pltpu

F32 = jnp.float32
BF16 = jnp.bfloat16

D_MODEL = 1024
BATCH = 2
SEQ = 8192
TOKENS = BATCH * SEQ
CHUNK = 64
PLE_DIM = 256

RET_HEADS = 4
RET_QK_DIM = 256
RET_V_DIM = 512
RET_QK_WIDTH = RET_HEADS * RET_QK_DIM
RET_V_WIDTH = RET_HEADS * RET_V_DIM
ROPE_BASE = 10000.0
ROPE_HALF = RET_QK_DIM // 2

ATT_HEADS = 16
ATT_HEAD_DIM = 64
LEFT_CHUNKS = 8
LEFT_PAD = LEFT_CHUNKS * CHUNK
REL_CLIP = 256

D_FF_DENSE = 2816
N_EXPERTS = 8
D_FF_EXPERT = 3584
EPS = 1e-6

LANES = 128
MXU_DIM = 256
VMEM_LIMIT = 56 * 1024 * 1024

RET_SUB = MXU_DIM
RET_BLOCK = 1024
ATT_QB = MXU_DIM
ATT_KB = ATT_QB + LEFT_PAD
ATT_QS = 2048
ATT_REL_WIDTH = 1024
ATT_PAD_BLOCKS = LEFT_PAD // ATT_QB
LOG2E = 1.4426950408889634
ATT_HG = MXU_DIM // ATT_HEAD_DIM
NEG_BIG = -1e30

TOP_K = 2
MOE_TM = 512
MOE_CAP = TOKENS + 2 * MOE_TM
MOE_REGION_TILES = MOE_CAP // MOE_TM
MOE_MAX_TILES = TOP_K * TOKENS // MOE_TM + N_EXPERTS
MOE_SPARE_ROW = N_EXPERTS * MOE_CAP
MOE_ROWS = MOE_SPARE_ROW + TOP_K * MOE_TM
SUBLANES = 8


def _dot(a, b):
    return jnp.dot(a, b, preferred_element_type=F32)


def _rms(x, g):
    return x * lax.rsqrt(jnp.mean(x * x, axis=-1, keepdims=True) + EPS) * g


def _silu(x):
    return x * jax.nn.sigmoid(x)


def _const_spec(shape):
    nd = len(shape)
    return pl.BlockSpec(shape, lambda *_: (0,) * nd, pipeline_mode=pl.Buffered(1))


def _params(sem):
    return pltpu.CompilerParams(dimension_semantics=sem, vmem_limit_bytes=VMEM_LIMIT)


def _inproj_kernel(x_ref, pos_ref, g_ref, invf_ref, w_ref, q_ref, k_ref, v_ref, gate_ref):
    xn = _rms(x_ref[...], g_ref[...]).astype(BF16)
    ang = pos_ref[...].astype(F32) * invf_ref[...]
    cos = jnp.cos(ang)
    sin = jnp.sin(ang)
    k_scale = RET_QK_DIM ** -0.5
    for h in range(RET_HEADS):
        lo = h * RET_QK_DIM
        mid = lo + ROPE_HALF
        hi = lo + RET_QK_DIM
        pq = _dot(xn, w_ref[:, lo:hi])
        x1, x2 = pq[:, :ROPE_HALF], pq[:, ROPE_HALF:]
        q_ref[:, lo:mid] = (x1 * cos - x2 * sin).astype(BF16)
        q_ref[:, mid:hi] = (x1 * sin + x2 * cos).astype(BF16)
        pk = _dot(xn, w_ref[:, RET_QK_WIDTH + lo:RET_QK_WIDTH + hi])
        x1, x2 = pk[:, :ROPE_HALF], pk[:, ROPE_HALF:]
        k_ref[:, lo:mid] = ((x1 * cos - x2 * sin) * k_scale).astype(BF16)
        k_ref[:, mid:hi] = ((x1 * sin + x2 * cos) * k_scale).astype(BF16)
    v0 = 2 * RET_QK_WIDTH
    g0 = v0 + RET_V_WIDTH
    for c in range(RET_HEADS):
        lo, hi = c * RET_V_DIM, (c + 1) * RET_V_DIM
        v_ref[:, lo:hi] = _dot(xn, w_ref[:, v0 + lo:v0 + hi]).astype(BF16)
        gate_ref[:, lo:hi] = _silu(_dot(xn, w_ref[:, g0 + lo:g0 + hi])).astype(BF16)


def _inproj(x, pos, g, invf, w, tm=512):
    n = TOKENS // tm
    tile = lambda width: pl.BlockSpec((tm, width), lambda i: (i, 0))
    return pl.pallas_call(
        _inproj_kernel,
        grid=(n,),
        in_specs=[tile(D_MODEL), tile(1), _const_spec((1, D_MODEL)), _const_spec((1, ROPE_HALF)),
                  _const_spec(w.shape)],
        out_specs=[tile(RET_QK_WIDTH), tile(RET_QK_WIDTH), tile(RET_V_WIDTH), tile(RET_V_WIDTH)],
        out_shape=[jax.ShapeDtypeStruct((TOKENS, RET_QK_WIDTH), BF16),
                   jax.ShapeDtypeStruct((TOKENS, RET_QK_WIDTH), BF16),
                   jax.ShapeDtypeStruct((TOKENS, RET_V_WIDTH), BF16),
                   jax.ShapeDtypeStruct((TOKENS, RET_V_WIDTH), BF16)],
        compiler_params=_params(("parallel",)),
        name="inproj",
    )(x, pos, g, invf, w)


def _ret_kernel(gc_ref, q_ref, k_ref, v_ref, gate_ref, dm_ref, xi_ref, zeta_ref, gn_ref,
                o_ref, state_ref):
    h = pl.program_id(1)

    @pl.when(pl.program_id(2) == 0)
    def _():
        state_ref[...] = jnp.zeros_like(state_ref)

    g_chunk = gc_ref[h]
    dm = dm_ref[0]
    xi = xi_ref[0]
    zeta = zeta_ref[0]
    gn = gn_ref[...]
    for j in range(RET_BLOCK // RET_SUB):
        sl = slice(j * RET_SUB, (j + 1) * RET_SUB)
        qj = q_ref[sl, :]
        kj = k_ref[sl, :]
        vj = v_ref[sl, :]
        st = state_ref[...]
        s = lax.dot_general(qj, kj, (((1,), (1,)), ((), ())), preferred_element_type=F32) * dm
        inner = _dot(s.astype(BF16), vj)
        cross = _dot(qj, st.astype(BF16)) * xi
        kz = (kj.astype(F32) * zeta).astype(BF16)
        upd = lax.dot_general(kz, vj, (((0,), (0,)), ((), ())), preferred_element_type=F32)
        state_ref[...] = st * g_chunk + upd
        o = inner + cross
        on = _rms(o, gn)
        o_ref[sl, :] = (on * gate_ref[sl, :].astype(F32)).astype(BF16)


def _retention(q, k, v, gate, gn):
    lg = jnp.log1p(-jnp.exp2(-5.0 - jnp.arange(RET_HEADS, dtype=F32)))
    idx = jnp.arange(RET_SUB, dtype=F32)
    diff = idx[:, None] - idx[None, :]
    causal = diff >= 0
    dmask = jnp.where(causal, jnp.exp(jnp.where(causal, diff, 0.0)[None] * lg[:, None, None]), 0.0)
    xi = jnp.exp((idx[None, :] + 1.0) * lg[:, None])[:, :, None]
    zeta = jnp.exp((RET_SUB - 1.0 - idx)[None, :] * lg[:, None])[:, :, None]
    g_chunk = jnp.exp(RET_SUB * lg)
    nb = SEQ // RET_BLOCK
    row = lambda b, h, n: (b * nb + n, h)
    per_head = lambda shape: pl.BlockSpec((1,) + shape, lambda b, h, n: (h, 0, 0))
    return pl.pallas_call(
        _ret_kernel,
        grid=(BATCH, RET_HEADS, nb),
        in_specs=[pl.BlockSpec(memory_space=pltpu.SMEM),
                  pl.BlockSpec((RET_BLOCK, RET_QK_DIM), row),
                  pl.BlockSpec((RET_BLOCK, RET_QK_DIM), row),
                  pl.BlockSpec((RET_BLOCK, RET_V_DIM), row),
                  pl.BlockSpec((RET_BLOCK, RET_V_DIM), row),
                  per_head((RET_SUB, RET_SUB)), per_head((RET_SUB, 1)), per_head((RET_SUB, 1)),
                  pl.BlockSpec((1, RET_V_DIM), lambda b, h, n: (0, h))],
        out_specs=pl.BlockSpec((RET_BLOCK, RET_V_DIM), row),
        out_shape=jax.ShapeDtypeStruct((TOKENS, RET_V_WIDTH), BF16),
        scratch_shapes=[pltpu.VMEM((RET_QK_DIM, RET_V_DIM), F32)],
        compiler_params=_params(("parallel", "parallel", "arbitrary")),
        name="retention",
    )(g_chunk, q, k, v, gate, dmask, xi, zeta, gn)


def _ple(h, p_ref, np_ref, wpu_ref, wpg_ref):
    gate = jax.nn.sigmoid(_dot(_rms(h, np_ref[...]).astype(BF16), wpg_ref[...]))
    return h + _dot(p_ref[...].astype(BF16), wpu_ref[...]) * gate


def _post_a_kernel(h_ref, og_ref, p_ref, wo_ref, nf_ref, wg_ref, wu_ref, wd_ref, np_ref, wpu_ref,
                   wpg_ref, out_ref):
    h1 = h_ref[...] + _dot(og_ref[...], wo_ref[...])
    xn = _rms(h1, nf_ref[...]).astype(BF16)
    a = (_silu(_dot(xn, wg_ref[...])) * _dot(xn, wu_ref[...])).astype(BF16)
    h2 = h1 + _dot(a, wd_ref[...])
    out_ref[...] = _ple(h2, p_ref, np_ref, wpu_ref, wpg_ref)


def _post_a(h, og, p, wo, nf, wg, wu, wd, npl, wpu, wpg, tm=256):
    n = TOKENS // tm
    tile = lambda width: pl.BlockSpec((tm, width), lambda i: (i, 0))
    consts = [wo, nf, wg, wu, wd, npl, wpu, wpg]
    return pl.pallas_call(
        _post_a_kernel,
        grid=(n,),
        in_specs=[tile(D_MODEL), tile(RET_V_WIDTH), tile(PLE_DIM)] + [_const_spec(c.shape) for c in consts],
        out_specs=tile(D_MODEL),
        out_shape=jax.ShapeDtypeStruct((TOKENS, D_MODEL), F32),
        compiler_params=_params(("parallel",)),
        name="post_a",
    )(h, og, p, *consts)


def _qkv_kernel(h_ref, gq_ref, gkv_ref, wq_ref, wkv_ref, qt_ref, k_ref, vt_ref):
    h = h_ref[...]
    hn = h * lax.rsqrt(jnp.mean(h * h, axis=-1, keepdims=True) + EPS)
    q = _dot((hn * gq_ref[...]).astype(BF16), wq_ref[...]) * (ATT_HEAD_DIM ** -0.5 * LOG2E)
    qt_ref[...] = q.T.astype(BF16)
    kv = _dot((hn * gkv_ref[...]).astype(BF16), wkv_ref[...])
    keep = (pl.program_id(1) > 0).astype(F32)
    k_ref[...] = (kv[:, :D_MODEL] * keep).astype(BF16)
    vt_ref[...] = (kv[:, D_MODEL:] * keep).T.astype(BF16)


def _qkv(h, gq, gkv, wq, wkv):
    tm = LEFT_PAD
    nb = SEQ // tm
    src = lambda b, j: (b * nb + jnp.maximum(j - 1, 0), 0)
    return pl.pallas_call(
        _qkv_kernel,
        grid=(BATCH, nb + 1),
        in_specs=[pl.BlockSpec((tm, D_MODEL), src), _const_spec((1, D_MODEL)), _const_spec((1, D_MODEL)),
                  _const_spec(wq.shape), _const_spec(wkv.shape)],
        out_specs=[pl.BlockSpec((D_MODEL, tm), lambda b, j: (0, b * nb + jnp.maximum(j - 1, 0))),
                   pl.BlockSpec((None, tm, D_MODEL), lambda b, j: (b, j, 0)),
                   pl.BlockSpec((None, D_MODEL, tm), lambda b, j: (b, 0, j))],
        out_shape=[jax.ShapeDtypeStruct((D_MODEL, TOKENS), BF16),
                   jax.ShapeDtypeStruct((BATCH, SEQ + LEFT_PAD, D_MODEL), BF16),
                   jax.ShapeDtypeStruct((BATCH, D_MODEL, SEQ + LEFT_PAD), BF16)],
        compiler_params=_params(("parallel", "arbitrary")),
        name="qkv_b",
    )(h, gq, gkv, wq, wkv)


def _attn_kernel(rel_ref, qt_ref, k_ref, vt_ref, o_ref, bias_ref, s0_ref, s1_ref, p0_ref, p1_ref):
    n = pl.program_id(2)

    @pl.when(n == 0)
    def _():
        kj = lax.broadcasted_iota(jnp.int32, (ATT_KB, ATT_QB), 0)
        qi = lax.broadcasted_iota(jnp.int32, (ATT_KB, ATT_QB), 1)
        k_chunk = kj // CHUNK - LEFT_CHUNKS
        q_chunk = qi // CHUNK
        allowed = (k_chunk <= q_chunk) & (k_chunk >= q_chunk - LEFT_CHUNKS)
        for g in range(ATT_HG):
            base = jnp.broadcast_to(rel_ref[g], (ATT_KB, ATT_REL_WIDTH))
            rolled = pltpu.roll(base, 0, 1, stride=1, stride_axis=0)
            table = jnp.where(allowed, rolled[:, :ATT_QB], NEG_BIG)
            bias_ref[0, g] = table
            for v in range(1, ATT_PAD_BLOCKS + 1):
                bias_ref[v, g] = jnp.where(kj >= LEFT_PAD - (v - 1) * ATT_QB, table, NEG_BIG)

    head_of_row = lax.broadcasted_iota(jnp.int32, (MXU_DIM, ATT_QB), 0) // ATT_HEAD_DIM
    n_blocks = ATT_QS // ATT_QB

    def window(qb):
        qb = jnp.minimum(qb, n_blocks - 1)
        q0 = pl.multiple_of(qb * ATT_QB, ATT_QB)
        kstart = pl.multiple_of(n * ATT_QS + qb * ATT_QB, ATT_QB)
        return q0, kstart

    def scores(qb, g, s_ref):
        q0, kstart = window(qb)
        qt = qt_ref[:, pl.ds(q0, ATT_QB)]
        kb = k_ref[pl.ds(kstart, ATT_KB), :]
        qg = jnp.where(head_of_row == g, qt, jnp.zeros_like(qt))
        block_in_seq = kstart // ATT_QB
        variant = jnp.where(block_in_seq < ATT_PAD_BLOCKS, block_in_seq + 1, 0)
        s = _dot(kb, qg) + bias_ref[variant, g]
        s_ref[...] = s
        return jnp.max(s, axis=0, keepdims=True)

    def probabilities(s_ref, p_ref, m):
        p = jnp.exp2(s_ref[...] - m)
        p_ref[...] = p.astype(BF16)
        return jnp.sum(p, axis=0, keepdims=True)

    def values(qb, g, p_ref, l):
        _, kstart = window(qb)
        vg = vt_ref[g * ATT_HEAD_DIM:(g + 1) * ATT_HEAD_DIM, pl.ds(kstart, ATT_KB)]
        return _dot(vg, p_ref[...]) / l

    s_bufs = (s0_ref, s1_ref)
    p_bufs = (p0_ref, p1_ref)
    m_first = scores(0, 0, s_bufs[0])
    m_second = scores(0, 1, s_bufs[1])
    l_first = probabilities(s_bufs[0], p_bufs[0], m_first)

    def body(qb, carry):
        l_cur, m_next = carry
        outs = []
        for g in range(ATT_HG):
            slot = g % 2
            outs.append(values(qb, g, p_bufs[slot], l_cur))
            l_cur = probabilities(s_bufs[1 - slot], p_bufs[1 - slot], m_next)
            ahead = g + 2
            m_next = scores(qb + ahead // ATT_HG, ahead % ATT_HG, s_bufs[slot])
        q0, _ = window(qb)
        o_ref[pl.ds(q0, ATT_QB), :] = jnp.concatenate(outs, axis=0).T.astype(BF16)
        return l_cur, m_next

    lax.fori_loop(0, n_blocks, body, (l_first, m_second))


def _rel_vectors(rel_table):
    heads, n_rel = rel_table.shape
    far = rel_table[:, n_rel - 1:]
    near = rel_table[:, :1]
    n_far_front = ATT_QB + 1
    n_near = ATT_REL_WIDTH - LEFT_PAD - (CHUNK - 1) - n_far_front
    n_far_back = ATT_REL_WIDTH - n_far_front - n_near - n_rel
    g = jnp.concatenate([jnp.broadcast_to(far, (heads, n_far_front)),
                         jnp.broadcast_to(near, (heads, n_near)),
                         rel_table,
                         jnp.broadcast_to(far, (heads, n_far_back))], axis=1)
    return (g.astype(F32) * LOG2E).reshape(heads, 1, ATT_REL_WIDTH)


def _band_attention(qt, kp, vt, rel):
    ns = SEQ // ATT_QS
    return pl.pallas_call(
        _attn_kernel,
        grid=(BATCH, ATT_HEADS // ATT_HG, ns),
        in_specs=[pl.BlockSpec((ATT_HG, 1, ATT_REL_WIDTH), lambda b, g, n: (g, 0, 0)),
                  pl.BlockSpec((MXU_DIM, ATT_QS), lambda b, g, n: (g, b * ns + n)),
                  pl.BlockSpec((None, SEQ + LEFT_PAD, MXU_DIM), lambda b, g, n: (b, 0, g)),
                  pl.BlockSpec((None, MXU_DIM, SEQ + LEFT_PAD), lambda b, g, n: (b, g, 0))],
        out_specs=pl.BlockSpec((ATT_QS, MXU_DIM), lambda b, g, n: (b * ns + n, g)),
        out_shape=jax.ShapeDtypeStruct((TOKENS, D_MODEL), BF16),
        scratch_shapes=[pltpu.VMEM((ATT_PAD_BLOCKS + 1, ATT_HG, ATT_KB, ATT_QB), F32),
                        pltpu.VMEM((ATT_KB, ATT_QB), F32), pltpu.VMEM((ATT_KB, ATT_QB), F32),
                        pltpu.VMEM((ATT_KB, ATT_QB), BF16), pltpu.VMEM((ATT_KB, ATT_QB), BF16)],
        compiler_params=_params(("parallel", "parallel", "arbitrary")),
        name="band_attn",
    )(rel, qt, kp, vt)


def _row_copy(src, src_row, dst, dst_row, sem):
    return pltpu.make_async_copy(src.at[pl.ds(src_row, 1)], dst.at[pl.ds(dst_row, 1)], sem)


def _tile_wait(hbm, vmem_tile, sem):
    pltpu.make_async_copy(hbm.at[pl.ds(0, MOE_TM)], vmem_tile, sem).wait()


def _scatter_tile(xbuf, rs_smem, xs_hbm, sem):
    for t in range(MOE_TM):
        for which in range(TOP_K):
            _row_copy(xbuf, t, xs_hbm, rs_smem[which, t], sem).start(priority=which)


def _route_tile(rows, h_ref, o_ref, wo_ref, nf_ref, wrh_ref, wrl_ref, tri_ref,
                h_out_ref, route_ref, gw_ref, cnt_ref, carry_ref):
    cols = rows
    h = h_ref[rows, :] + _dot(o_ref[rows, :], wo_ref[...])
    h_out_ref[rows, :] = h
    xn = _rms(h, nf_ref[...])

    xh = xn.astype(BF16)
    xl = (xn - xh.astype(F32)).astype(BF16)
    logits = _dot(xh, wrh_ref[...]) + _dot(xl, wrh_ref[...]) + _dot(xh, wrl_ref[...])
    lt = logits.T[:N_EXPERTS]
    row = lax.broadcasted_iota(jnp.int32, lt.shape, 0)
    m1 = jnp.max(lt, axis=0, keepdims=True)
    i1 = jnp.min(jnp.where(lt == m1, row, N_EXPERTS), axis=0, keepdims=True)
    rest = jnp.where(row == i1, -jnp.inf, lt)
    m2 = jnp.max(rest, axis=0, keepdims=True)
    i2 = jnp.min(jnp.where(rest == m2, row, N_EXPERTS), axis=0, keepdims=True)
    e2 = jnp.exp(m2 - m1)
    w1 = 1.0 / (1.0 + e2)
    w2 = e2 / (1.0 + e2)

    member = ((row == i1) | (row == i2)).astype(F32)
    within = _dot(member.astype(BF16), tri_ref[...])
    carry = carry_ref[...]
    rank = within + carry[:, 0:1]
    r1 = jnp.sum(jnp.where(row == i1, rank, 0.0), axis=0, keepdims=True).astype(jnp.int32)
    r2 = jnp.sum(jnp.where(row == i2, rank, 0.0), axis=0, keepdims=True).astype(jnp.int32)
    carry_ref[...] = carry + jnp.sum(member, axis=1, keepdims=True)
    cnt_ref[...] = carry_ref[...].astype(jnp.int32)

    s1 = i1 * MOE_CAP + r1
    s2 = i2 * MOE_CAP + r2
    route_ref[:, cols] = jnp.where(row == 0, s1, jnp.where(row == 1, s2, 0))
    gwt = jnp.where(row == 0, w1, jnp.where(row == 1, w2, 0.0))
    pad = jnp.zeros((LANES - N_EXPERTS, gwt.shape[1]), F32)
    gw_ref[rows, :] = jnp.concatenate([gwt, pad], axis=0).T
    return xn


def _post_b_kernel(h_ref, o_ref, wo_ref, nf_ref, wrh_ref, wrl_ref, tri_ref,
                   h_out_ref, route_ref, gw_ref, cnt_ref, xs_hbm,
                   xbuf0, xbuf1, carry_ref, rs0, rs1, cnt_smem, zero_ref, sems, tail_sem):
    g = pl.program_id(0)
    tile_args = (h_ref, o_ref, wo_ref, nf_ref, wrh_ref, wrl_ref, tri_ref,
                 h_out_ref, route_ref, gw_ref, cnt_ref, carry_ref)
    first = slice(0, MOE_TM)
    second = slice(MOE_TM, 2 * MOE_TM)

    @pl.when(g == 0)
    def _():
        carry_ref[...] = jnp.zeros_like(carry_ref)
        xbuf1[...] = jnp.zeros_like(xbuf1)
        which = lax.broadcasted_iota(jnp.int32, (SUBLANES, MOE_TM), 0)
        token = lax.broadcasted_iota(jnp.int32, (SUBLANES, MOE_TM), 1)
        route_ref[:, first] = MOE_SPARE_ROW + jnp.minimum(which, TOP_K - 1) * MOE_TM + token
        pltpu.sync_copy(route_ref.at[:, first], rs1)

    _scatter_tile(xbuf1, rs1, xs_hbm, sems.at[1])
    xn = _route_tile(first, *tile_args)

    @pl.when(g >= 1)
    def _():
        for _ in range(TOP_K):
            _tile_wait(xs_hbm, xbuf0, sems.at[0])

    xbuf0[...] = xn
    pltpu.sync_copy(route_ref.at[:, first], rs0)

    _scatter_tile(xbuf0, rs0, xs_hbm, sems.at[0])
    xn = _route_tile(second, *tile_args)
    for _ in range(TOP_K):
        _tile_wait(xs_hbm, xbuf1, sems.at[1])
    xbuf1[...] = xn
    pltpu.sync_copy(route_ref.at[:, second], rs1)

    @pl.when(g == pl.num_programs(0) - 1)
    def _():
        _scatter_tile(xbuf1, rs1, xs_hbm, sems.at[1])
        for buf, sem in ((xbuf0, sems.at[0]), (xbuf1, sems.at[1])):
            for _ in range(TOP_K):
                _tile_wait(xs_hbm, buf, sem)

        pltpu.sync_copy(cnt_ref, cnt_smem)
        zero_ref[...] = jnp.zeros_like(zero_ref)
        for e in range(N_EXPERTS):
            count = cnt_smem[e, 0]
            aligned = ((count + (SUBLANES - 1)) // SUBLANES) * SUBLANES
            start = pl.multiple_of(e * MOE_CAP + aligned, SUBLANES)
            pltpu.make_async_copy(zero_ref, xs_hbm.at[pl.ds(start, MOE_TM)], tail_sem).start()
            for k in range(SUBLANES - 1):
                @pl.when(count + k < aligned)
                def _():
                    _row_copy(zero_ref, 0, xs_hbm, e * MOE_CAP + count + k, tail_sem).start()
        for e in range(N_EXPERTS):
            count = cnt_smem[e, 0]
            aligned = ((count + (SUBLANES - 1)) // SUBLANES) * SUBLANES
            _tile_wait(xs_hbm, zero_ref, tail_sem)
            for k in range(SUBLANES - 1):
                @pl.when(count + k < aligned)
                def _():
                    _row_copy(zero_ref, 0, xs_hbm, 0, tail_sem).wait()


def _post_b(h, o, wo, nf, wr_hi, wr_lo):
    tm = MOE_TM
    pair = 2 * tm
    n = TOKENS // pair
    tile = lambda width: pl.BlockSpec((pair, width), lambda i: (i, 0))
    tri = jnp.asarray(np.triu(np.ones((tm, tm), np.float32), k=1), dtype=BF16)
    return pl.pallas_call(
        _post_b_kernel,
        grid=(n,),
        in_specs=[tile(D_MODEL), tile(D_MODEL), _const_spec(wo.shape), _const_spec((1, D_MODEL)),
                  _const_spec(wr_hi.shape), _const_spec(wr_lo.shape), _const_spec(tri.shape)],
        out_specs=[tile(D_MODEL),
                   pl.BlockSpec((SUBLANES, pair), lambda i: (0, i)),
                   tile(LANES),
                   pl.BlockSpec((SUBLANES, LANES), lambda i: (0, 0)),
                   pl.BlockSpec(memory_space=pl.ANY)],
        out_shape=[jax.ShapeDtypeStruct((TOKENS, D_MODEL), F32),
                   jax.ShapeDtypeStruct((SUBLANES, TOKENS), jnp.int32),
                   jax.ShapeDtypeStruct((TOKENS, LANES), F32),
                   jax.ShapeDtypeStruct((SUBLANES, LANES), jnp.int32),
                   jax.ShapeDtypeStruct((MOE_ROWS, D_MODEL), F32)],
        scratch_shapes=[pltpu.VMEM((tm, D_MODEL), F32),
                        pltpu.VMEM((tm, D_MODEL), F32),
                        pltpu.VMEM((SUBLANES, LANES), F32),
                        pltpu.SMEM((SUBLANES, tm), jnp.int32),
                        pltpu.SMEM((SUBLANES, tm), jnp.int32),
                        pltpu.SMEM((SUBLANES, LANES), jnp.int32),
                        pltpu.VMEM((tm, D_MODEL), F32),
                        pltpu.SemaphoreType.DMA((2,)),
                        pltpu.SemaphoreType.DMA(())],
        compiler_params=_params(("arbitrary",)),
        name="post_b",
    )(h, o, wo, nf, wr_hi, wr_lo, tri)


def _moe_kernel(blk_ref, exp_ref, nact_ref, x_ref, wg_ref, wu_ref, wd_ref, y_ref, xb_ref):
    f = pl.program_id(1)

    @pl.when(pl.program_id(0) < nact_ref[0])
    def _():
        @pl.when(f == 0)
        def _():
            xb_ref[...] = x_ref[...].astype(BF16)
            y_ref[...] = jnp.zeros_like(y_ref)

        x = xb_ref[...]
        a = (_silu(_dot(x, wg_ref[...])) * _dot(x, wu_ref[...])).astype(BF16)
        y_ref[...] += _dot(a, wd_ref[...])


def _moe(xs, tile_block, tile_expert, n_active, wg, wu, wd, tf=1792):
    tm = MOE_TM
    nf = D_FF_EXPERT // tf
    fidx = lambda j, f, nact: jnp.where(j < nact[0], f, nf - 1)
    rows = pl.BlockSpec((tm, D_MODEL), lambda j, f, blk, exp, nact: (blk[j], 0))
    grid_spec = pltpu.PrefetchScalarGridSpec(
        num_scalar_prefetch=3,
        grid=(MOE_MAX_TILES, nf),
        in_specs=[rows,
                  pl.BlockSpec((None, D_MODEL, tf), lambda j, f, blk, exp, nact: (exp[j], 0, fidx(j, f, nact))),
                  pl.BlockSpec((None, D_MODEL, tf), lambda j, f, blk, exp, nact: (exp[j], 0, fidx(j, f, nact))),
                  pl.BlockSpec((None, tf, D_MODEL), lambda j, f, blk, exp, nact: (exp[j], fidx(j, f, nact), 0))],
        out_specs=rows,
        scratch_shapes=[pltpu.VMEM((tm, D_MODEL), BF16)],
    )
    return pl.pallas_call(
        _moe_kernel,
        grid_spec=grid_spec,
        out_shape=jax.ShapeDtypeStruct((MOE_ROWS, D_MODEL), F32),
        compiler_params=_params(("arbitrary", "arbitrary")),
        name="moe",
    )(tile_block, tile_expert, n_active, xs, wg, wu, wd)


def _tile_plan(counts):
    tiles = (counts + (MOE_TM - 1)) // MOE_TM
    ends = jnp.cumsum(tiles)
    n_active = ends[-1]
    j = jnp.minimum(jnp.arange(MOE_MAX_TILES, dtype=jnp.int32), n_active - 1)
    expert = jnp.sum((j[:, None] >= ends[None, :]).astype(jnp.int32), axis=1)
    first = jnp.sum(jnp.where(expert[:, None] == jnp.arange(N_EXPERTS)[None, :], (ends - tiles)[None, :], 0), axis=1)
    block = expert * MOE_REGION_TILES + (j - first)
    return block.astype(jnp.int32), expert.astype(jnp.int32), n_active.reshape(1).astype(jnp.int32)


def _gather_tile(ys_hbm, rs_smem, ybuf, sem):
    for t in range(MOE_TM):
        for which in range(TOP_K):
            _row_copy(ys_hbm, rs_smem[which, t], ybuf.at[which], t, sem).start(priority=which)


def _final_kernel(h_ref, p_ref, route_ref, route_next_ref, gw_ref, np_ref, wpu_ref, wpg_ref, nfin_ref,
                  ys_hbm, out_ref, ybuf0, ybuf1, rs_smem, sems):
    first = slice(0, MOE_TM)
    second = slice(MOE_TM, 2 * MOE_TM)

    def combine(rows, ybuf):
        gw = gw_ref[rows, :]
        y = gw[:, 0:1] * ybuf[0] + gw[:, 1:2] * ybuf[1]
        hp = h_ref[rows, :] + y
        gate = jax.nn.sigmoid(_dot(_rms(hp, np_ref[...]).astype(BF16), wpg_ref[...]))
        hp = hp + _dot(p_ref[rows, :].astype(BF16), wpu_ref[...]) * gate
        out_ref[rows, :] = _rms(hp, nfin_ref[...])

    @pl.when(pl.program_id(0) == 0)
    def _():
        pltpu.sync_copy(route_ref.at[:, first], rs_smem)
        _gather_tile(ys_hbm, rs_smem, ybuf0, sems.at[0])

    for which in range(TOP_K):
        _tile_wait(ys_hbm, ybuf0.at[which], sems.at[0])
    pltpu.sync_copy(route_ref.at[:, second], rs_smem)
    _gather_tile(ys_hbm, rs_smem, ybuf1, sems.at[1])
    combine(first, ybuf0)

    for which in range(TOP_K):
        _tile_wait(ys_hbm, ybuf1.at[which], sems.at[1])
    pltpu.sync_copy(route_next_ref, rs_smem)
    _gather_tile(ys_hbm, rs_smem, ybuf0, sems.at[0])
    combine(second, ybuf1)

    @pl.when(pl.program_id(0) == pl.num_programs(0) - 1)
    def _():
        for which in range(TOP_K):
            _tile_wait(ys_hbm, ybuf0.at[which], sems.at[0])


def _final(h, p, route, gw, npl, wpu, wpg, nfin, ys):
    tm = MOE_TM
    pair = 2 * tm
    n = TOKENS // pair
    tile = lambda width: pl.BlockSpec((pair, width), lambda i: (i, 0))
    consts = [npl, wpu, wpg, nfin]
    next_tile = lambda i: (0, jnp.minimum(2 * i + 2, TOKENS // tm - 1))
    return pl.pallas_call(
        _final_kernel,
        grid=(n,),
        in_specs=[tile(D_MODEL), tile(PLE_DIM),
                  pl.BlockSpec((SUBLANES, pair), lambda i: (0, i)),
                  pl.BlockSpec((SUBLANES, tm), next_tile),
                  tile(LANES)]
                 + [_const_spec(c.shape) for c in consts] + [pl.BlockSpec(memory_space=pl.ANY)],
        out_specs=tile(D_MODEL),
        out_shape=jax.ShapeDtypeStruct((TOKENS, D_MODEL), F32),
        scratch_shapes=[pltpu.VMEM((TOP_K, tm, D_MODEL), F32),
                        pltpu.VMEM((TOP_K, tm, D_MODEL), F32),
                        pltpu.SMEM((SUBLANES, tm), jnp.int32),
                        pltpu.SemaphoreType.DMA((2,))],
        compiler_params=_params(("arbitrary",)),
        name="final",
    )(h, p, route, route, gw, *consts, ys)


def kernel(x, p, positions, norm_mix, norm_ffn, norm_ple, w_in_a, ret_gn, w_out_a, norm_kv, w_kv, w_q_b, rel_bias, w_out_b, w_gate_dense, w_up_dense, w_down_dense, w_router, w_gate_moe, w_up_moe, w_down_moe, w_ple_up, w_ple_gate, norm_final):
    bf = lambda w: w.astype(BF16)
    row = lambda g: g.reshape(1, -1).astype(F32)
    h0 = x.reshape(TOKENS, D_MODEL)
    p2 = p.reshape(2, TOKENS, PLE_DIM)
    pos = positions.reshape(TOKENS, 1)
    inv_freq = (1.0 / (ROPE_BASE ** jnp.linspace(0.0, 1.0, ROPE_HALF, dtype=F32))).reshape(1, ROPE_HALF)

    q, k, v, gate = _inproj(h0, pos, row(norm_mix[0]), inv_freq, bf(w_in_a[0]))
    og = _retention(q, k, v, gate, row(ret_gn[0]))
    h1 = _post_a(h0, og, p2[0], bf(w_out_a[0]), row(norm_ffn[0]), bf(w_gate_dense[0]), bf(w_up_dense[0]),
                 bf(w_down_dense[0]), row(norm_ple[0]), bf(w_ple_up[0]), bf(w_ple_gate[0]))

    qt, kp, vt = _qkv(h1, row(norm_mix[1]), row(norm_kv), bf(w_q_b[0]), bf(w_kv))
    ob = _band_attention(qt, kp, vt, _rel_vectors(rel_bias[0]))
    w_router_pad = jnp.zeros((D_MODEL, LANES), F32).at[:, :N_EXPERTS].set(w_router[0])
    wr_hi = bf(w_router_pad)
    wr_lo = bf(w_router_pad - wr_hi.astype(F32))
    h2, route, gw, counts, xs = _post_b(h1, ob, bf(w_out_b[0]), row(norm_ffn[1]), wr_hi, wr_lo)
    tile_block, tile_expert, n_active = _tile_plan(counts[:, 0])
    ys = _moe(xs, tile_block, tile_expert, n_active, bf(w_gate_moe[0]), bf(w_up_moe[0]), bf(w_down_moe[0]))
    out = _final(h2, p2[1], route, gw, row(norm_ple[1]), bf(w_ple_up[1]), bf(w_ple_gate[1]), row(norm_final), ys)
    return out.reshape(BATCH, SEQ, D_MODEL)
```

```python
import functools

import numpy as np
import jax
import jax.numpy as jnp
from jax import lax
from jax.experimental import pallas as pl
from jax.experimental.pallas import tpu as pltpu

F32 = jnp.float32
BF16 = jnp.bfloat16

D_MODEL = 1024
BATCH = 2
SEQ = 8192
TOKENS = BATCH * SEQ
CHUNK = 64
PLE_DIM = 256

RET_HEADS = 4
RET_QK_DIM = 256
RET_V_DIM = 512
RET_QK_WIDTH = RET_HEADS * RET_QK_DIM
RET_V_WIDTH = RET_HEADS * RET_V_DIM
ROPE_BASE = 10000.0
ROPE_HALF = RET_QK_DIM // 2

ATT_HEADS = 16
ATT_HEAD_DIM = 64
LEFT_CHUNKS = 8
LEFT_PAD = LEFT_CHUNKS * CHUNK
REL_CLIP = 256

D_FF_DENSE = 2816
N_EXPERTS = 8
D_FF_EXPERT = 3584
EPS = 1e-6

LANES = 128
MXU_DIM = 256
VMEM_LIMIT = 56 * 1024 * 1024

RET_SUB = MXU_DIM
RET_BLOCK = 1024
ATT_QB = MXU_DIM
ATT_KB = ATT_QB + LEFT_PAD
ATT_QS = 2048
ATT_REL_WIDTH = 1024
ATT_PAD_BLOCKS = LEFT_PAD // ATT_QB
ATT_LIVE = tuple((slice(LANES * t, LANES * t + LEFT_PAD + LANES), slice(LANES * t, LANES * (t + 1)))
                 for t in range(ATT_QB // LANES))
LOG2E = 1.4426950408889634
ATT_HG = MXU_DIM // ATT_HEAD_DIM
NEG_BIG = -1e30

TOP_K = 2
MOE_TM = 512
MOE_SUPER = 2 * MOE_TM
MOE_CAP = TOKENS + 2 * MOE_SUPER
MOE_REGION_TILES = MOE_CAP // MOE_SUPER
MOE_MAX_TILES = TOP_K * TOKENS // MOE_SUPER + N_EXPERTS
MOE_SPARE_ROW = N_EXPERTS * MOE_CAP
MOE_ROWS = MOE_SPARE_ROW + TOP_K * MOE_TM
SUBLANES = 8


def _dot(a, b):
    return jnp.dot(a, b, preferred_element_type=F32)


def _rms(x, g):
    return x * lax.rsqrt(jnp.mean(x * x, axis=-1, keepdims=True) + EPS) * g


def _silu(x):
    return x * jax.nn.sigmoid(x)


def _const_spec(shape):
    nd = len(shape)
    return pl.BlockSpec(shape, lambda *_: (0,) * nd, pipeline_mode=pl.Buffered(1))


def _params(sem):
    return pltpu.CompilerParams(dimension_semantics=sem, vmem_limit_bytes=VMEM_LIMIT)


def _inproj_kernel(x_ref, pos_ref, g_ref, invf_ref, w_ref, q_ref, k_ref, v_ref, gate_ref):
    xn = _rms(x_ref[...], g_ref[...]).astype(BF16)
    v0 = 2 * RET_QK_WIDTH
    g0 = v0 + RET_V_WIDTH
    for c in range(RET_HEADS):
        lo, hi = c * RET_V_DIM, (c + 1) * RET_V_DIM
        v_ref[:, lo:hi] = _dot(xn, w_ref[:, v0 + lo:v0 + hi]).astype(BF16)
        gate_ref[:, lo:hi] = _silu(_dot(xn, w_ref[:, g0 + lo:g0 + hi])).astype(BF16)
    ang = pos_ref[...].astype(F32) * invf_ref[...]
    cos = jnp.cos(ang)
    sin = jnp.sin(ang)
    k_scale = RET_QK_DIM ** -0.5
    for h in range(RET_HEADS):
        lo = h * RET_QK_DIM
        mid = lo + ROPE_HALF
        hi = lo + RET_QK_DIM
        pq = _dot(xn, w_ref[:, lo:hi])
        x1, x2 = pq[:, :ROPE_HALF], pq[:, ROPE_HALF:]
        q_ref[:, lo:mid] = (x1 * cos - x2 * sin).astype(BF16)
        q_ref[:, mid:hi] = (x1 * sin + x2 * cos).astype(BF16)
        pk = _dot(xn, w_ref[:, RET_QK_WIDTH + lo:RET_QK_WIDTH + hi])
        x1, x2 = pk[:, :ROPE_HALF], pk[:, ROPE_HALF:]
        k_ref[:, lo:mid] = ((x1 * cos - x2 * sin) * k_scale).astype(BF16)
        k_ref[:, mid:hi] = ((x1 * sin + x2 * cos) * k_scale).astype(BF16)


def _inproj(x, pos, g, invf, w, tm=512):
    n = TOKENS // tm
    tile = lambda width: pl.BlockSpec((tm, width), lambda i: (i, 0))
    return pl.pallas_call(
        _inproj_kernel,
        grid=(n,),
        in_specs=[tile(D_MODEL), tile(1), _const_spec((1, D_MODEL)), _const_spec((1, ROPE_HALF)),
                  _const_spec(w.shape)],
        out_specs=[tile(RET_QK_WIDTH), tile(RET_QK_WIDTH), tile(RET_V_WIDTH), tile(RET_V_WIDTH)],
        out_shape=[jax.ShapeDtypeStruct((TOKENS, RET_QK_WIDTH), BF16),
                   jax.ShapeDtypeStruct((TOKENS, RET_QK_WIDTH), BF16),
                   jax.ShapeDtypeStruct((TOKENS, RET_V_WIDTH), BF16),
                   jax.ShapeDtypeStruct((TOKENS, RET_V_WIDTH), BF16)],
        compiler_params=_params(("parallel",)),
        name="inproj",
    )(x, pos, g, invf, w)


def _ret_kernel(gc_ref, q_ref, k_ref, v_ref, gate_ref, dm_ref, xi_ref, zeta_ref, gn_ref,
                o_ref, state_ref):
    h = pl.program_id(1)

    @pl.when(pl.program_id(2) == 0)
    def _():
        state_ref[...] = jnp.zeros_like(state_ref)

    g_chunk = gc_ref[h]
    dm = dm_ref[0]
    xi = xi_ref[0]
    zeta = zeta_ref[0]
    gn = gn_ref[...]
    for j in range(RET_BLOCK // RET_SUB):
        sl = slice(j * RET_SUB, (j + 1) * RET_SUB)
        qj = q_ref[sl, :]
        kj = k_ref[sl, :]
        vj = v_ref[sl, :]
        st = state_ref[...]
        s = lax.dot_general(qj, kj, (((1,), (1,)), ((), ())), preferred_element_type=F32) * dm
        inner = _dot(s.astype(BF16), vj)
        cross = _dot(qj, st.astype(BF16)) * xi
        kz = (kj.astype(F32) * zeta).astype(BF16)
        upd = lax.dot_general(kz, vj, (((0,), (0,)), ((), ())), preferred_element_type=F32)
        state_ref[...] = st * g_chunk + upd
        o = inner + cross
        on = _rms(o, gn)
        o_ref[sl, :] = (on * gate_ref[sl, :].astype(F32)).astype(BF16)


def _retention(q, k, v, gate, gn):
    lg = jnp.log1p(-jnp.exp2(-5.0 - jnp.arange(RET_HEADS, dtype=F32)))
    idx = jnp.arange(RET_SUB, dtype=F32)
    diff = idx[:, None] - idx[None, :]
    causal = diff >= 0
    dmask = jnp.where(causal, jnp.exp(jnp.where(causal, diff, 0.0)[None] * lg[:, None, None]), 0.0)
    xi = jnp.exp((idx[None, :] + 1.0) * lg[:, None])[:, :, None]
    zeta = jnp.exp((RET_SUB - 1.0 - idx)[None, :] * lg[:, None])[:, :, None]
    g_chunk = jnp.exp(RET_SUB * lg)
    nb = SEQ // RET_BLOCK
    row = lambda b, h, n: (b * nb + n, h)
    per_head = lambda shape: pl.BlockSpec((1,) + shape, lambda b, h, n: (h, 0, 0))
    return pl.pallas_call(
        _ret_kernel,
        grid=(BATCH, RET_HEADS, nb),
        in_specs=[pl.BlockSpec(memory_space=pltpu.SMEM),
                  pl.BlockSpec((RET_BLOCK, RET_QK_DIM), row),
                  pl.BlockSpec((RET_BLOCK, RET_QK_DIM), row),
                  pl.BlockSpec((RET_BLOCK, RET_V_DIM), row),
                  pl.BlockSpec((RET_BLOCK, RET_V_DIM), row),
                  per_head((RET_SUB, RET_SUB)), per_head((RET_SUB, 1)), per_head((RET_SUB, 1)),
                  pl.BlockSpec((1, RET_V_DIM), lambda b, h, n: (0, h))],
        out_specs=pl.BlockSpec((RET_BLOCK, RET_V_DIM), row),
        out_shape=jax.ShapeDtypeStruct((TOKENS, RET_V_WIDTH), BF16),
        scratch_shapes=[pltpu.VMEM((RET_QK_DIM, RET_V_DIM), F32)],
        compiler_params=_params(("parallel", "parallel", "arbitrary")),
        name="retention",
    )(g_chunk, q, k, v, gate, dmask, xi, zeta, gn)


def _ple(h, p_ref, np_ref, wpu_ref, wpg_ref):
    gate = jax.nn.sigmoid(_dot(_rms(h, np_ref[...]).astype(BF16), wpg_ref[...]))
    return h + _dot(p_ref[...].astype(BF16), wpu_ref[...]) * gate


def _post_a_kernel(h_ref, og_ref, p_ref, wo_ref, nf_ref, wg_ref, wu_ref, wd_ref, np_ref, wpu_ref,
                   wpg_ref, out_ref):
    h1 = h_ref[...] + _dot(og_ref[...], wo_ref[...])
    xn = _rms(h1, nf_ref[...]).astype(BF16)
    a = (_silu(_dot(xn, wg_ref[...])) * _dot(xn, wu_ref[...])).astype(BF16)
    h2 = h1 + _dot(a, wd_ref[...])
    out_ref[...] = _ple(h2, p_ref, np_ref, wpu_ref, wpg_ref)


def _post_a(h, og, p, wo, nf, wg, wu, wd, npl, wpu, wpg, tm=512):
    n = TOKENS // tm
    tile = lambda width: pl.BlockSpec((tm, width), lambda i: (i, 0))
    consts = [wo, nf, wg, wu, wd, npl, wpu, wpg]
    return pl.pallas_call(
        _post_a_kernel,
        grid=(n,),
        in_specs=[tile(D_MODEL), tile(RET_V_WIDTH), tile(PLE_DIM)] + [_const_spec(c.shape) for c in consts],
        out_specs=tile(D_MODEL),
        out_shape=jax.ShapeDtypeStruct((TOKENS, D_MODEL), F32),
        compiler_params=_params(("parallel",)),
        name="post_a",
    )(h, og, p, *consts)


def _qkv_kernel(h_ref, gq_ref, gkv_ref, wq_ref, wkv_ref, qt_ref, k_ref, vt_ref):
    h = h_ref[...]
    hn = h * lax.rsqrt(jnp.mean(h * h, axis=-1, keepdims=True) + EPS)
    q = _dot((hn * gq_ref[...]).astype(BF16), wq_ref[...]) * (ATT_HEAD_DIM ** -0.5 * LOG2E)
    qt_ref[...] = q.T.astype(BF16)
    kv = _dot((hn * gkv_ref[...]).astype(BF16), wkv_ref[...])
    keep = (pl.program_id(1) > 0).astype(F32)
    k_ref[...] = (kv[:, :D_MODEL] * keep).astype(BF16)
    vt_ref[...] = (kv[:, D_MODEL:] * keep).T.astype(BF16)


def _qkv(h, gq, gkv, wq, wkv):
    tm = LEFT_PAD
    nb = SEQ // tm
    src = lambda b, j: (b * nb + jnp.maximum(j - 1, 0), 0)
    return pl.pallas_call(
        _qkv_kernel,
        grid=(BATCH, nb + 1),
        in_specs=[pl.BlockSpec((tm, D_MODEL), src), _const_spec((1, D_MODEL)), _const_spec((1, D_MODEL)),
                  _const_spec(wq.shape), _const_spec(wkv.shape)],
        out_specs=[pl.BlockSpec((D_MODEL, tm), lambda b, j: (0, b * nb + jnp.maximum(j - 1, 0))),
                   pl.BlockSpec((None, tm, D_MODEL), lambda b, j: (b, j, 0)),
                   pl.BlockSpec((None, D_MODEL, tm), lambda b, j: (b, 0, j))],
        out_shape=[jax.ShapeDtypeStruct((D_MODEL, TOKENS), BF16),
                   jax.ShapeDtypeStruct((BATCH, SEQ + LEFT_PAD, D_MODEL), BF16),
                   jax.ShapeDtypeStruct((BATCH, D_MODEL, SEQ + LEFT_PAD), BF16)],
        compiler_params=_params(("parallel", "arbitrary")),
        name="qkv_b",
    )(h, gq, gkv, wq, wkv)


def _attn_kernel(rel_ref, qt_ref, k_ref, vt_ref, o_ref, bias_ref, s0_ref, s1_ref, p0_ref, p1_ref):
    n = pl.program_id(2)

    @pl.when(n == 0)
    def _():
        kj = lax.broadcasted_iota(jnp.int32, (ATT_KB, ATT_QB), 0)
        qi = lax.broadcasted_iota(jnp.int32, (ATT_KB, ATT_QB), 1)
        k_chunk = kj // CHUNK - LEFT_CHUNKS
        q_chunk = qi // CHUNK
        allowed = (k_chunk <= q_chunk) & (k_chunk >= q_chunk - LEFT_CHUNKS)
        for g in range(ATT_HG):
            base = jnp.broadcast_to(rel_ref[g], (ATT_KB, ATT_REL_WIDTH))
            rolled = pltpu.roll(base, 0, 1, stride=1, stride_axis=0)
            table = jnp.where(allowed, rolled[:, :ATT_QB], NEG_BIG)
            bias_ref[0, g] = table
            for v in range(1, ATT_PAD_BLOCKS + 1):
                bias_ref[v, g] = jnp.where(kj >= LEFT_PAD - (v - 1) * ATT_QB, table, NEG_BIG)
        p0_ref[...] = jnp.zeros_like(p0_ref)
        p1_ref[...] = jnp.zeros_like(p1_ref)

    head_of_row = lax.broadcasted_iota(jnp.int32, (MXU_DIM, ATT_QB), 0) // ATT_HEAD_DIM
    n_blocks = ATT_QS // ATT_QB

    def window(qb):
        qb = jnp.minimum(qb, n_blocks - 1)
        q0 = pl.multiple_of(qb * ATT_QB, ATT_QB)
        kstart = pl.multiple_of(n * ATT_QS + qb * ATT_QB, ATT_QB)
        return q0, kstart

    def scores(qb, g, s_ref):
        q0, kstart = window(qb)
        qt = qt_ref[:, pl.ds(q0, ATT_QB)]
        kb = k_ref[pl.ds(kstart, ATT_KB), :]
        qg = jnp.where(head_of_row == g, qt, jnp.zeros_like(qt))
        block_in_seq = kstart // ATT_QB
        variant = jnp.where(block_in_seq < ATT_PAD_BLOCKS, block_in_seq + 1, 0)
        s = _dot(kb, qg)
        maxes = []
        for rows, lanes in ATT_LIVE:
            part = s[rows, lanes] + bias_ref[variant, g, rows, lanes]
            s_ref[rows, lanes] = part
            maxes.append(jnp.max(part, axis=0, keepdims=True))
        return jnp.concatenate(maxes, axis=1)

    def probabilities(s_ref, p_ref, m):
        sums = []
        for rows, lanes in ATT_LIVE:
            p = jnp.exp2(s_ref[rows, lanes] - m[:, lanes])
            p_ref[rows, lanes] = p.astype(BF16)
            sums.append(jnp.sum(p, axis=0, keepdims=True))
        return jnp.concatenate(sums, axis=1)

    def values(qb, g, p_ref, l):
        _, kstart = window(qb)
        vg = vt_ref[g * ATT_HEAD_DIM:(g + 1) * ATT_HEAD_DIM, pl.ds(kstart, ATT_KB)]
        return _dot(vg, p_ref[...]) / l

    s_bufs = (s0_ref, s1_ref)
    p_bufs = (p0_ref, p1_ref)
    m_first = scores(0, 0, s_bufs[0])
    m_second = scores(0, 1, s_bufs[1])
    l_first = probabilities(s_bufs[0], p_bufs[0], m_first)

    def body(qb, carry):
        l_cur, m_next = carry
        outs = []
        for g in range(ATT_HG):
            slot = g % 2
            outs.append(values(qb, g, p_bufs[slot], l_cur))
            l_cur = probabilities(s_bufs[1 - slot], p_bufs[1 - slot], m_next)
            ahead = g + 2
            m_next = scores(qb + ahead // ATT_HG, ahead % ATT_HG, s_bufs[slot])
        q0, _ = window(qb)
        o_ref[pl.ds(q0, ATT_QB), :] = jnp.concatenate(outs, axis=0).T.astype(BF16)
        return l_cur, m_next

    lax.fori_loop(0, n_blocks, body, (l_first, m_second))


def _rel_vectors(rel_table):
    heads, n_rel = rel_table.shape
    far = rel_table[:, n_rel - 1:]
    near = rel_table[:, :1]
    n_far_front = ATT_QB + 1
    n_near = ATT_REL_WIDTH - LEFT_PAD - (CHUNK - 1) - n_far_front
    n_far_back = ATT_REL_WIDTH - n_far_front - n_near - n_rel
    g = jnp.concatenate([jnp.broadcast_to(far, (heads, n_far_front)),
                         jnp.broadcast_to(near, (heads, n_near)),
                         rel_table,
                         jnp.broadcast_to(far, (heads, n_far_back))], axis=1)
    return (g.astype(F32) * LOG2E).reshape(heads, 1, ATT_REL_WIDTH)


def _band_attention(qt, kp, vt, rel):
    ns = SEQ // ATT_QS
    return pl.pallas_call(
        _attn_kernel,
        grid=(BATCH, ATT_HEADS // ATT_HG, ns),
        in_specs=[pl.BlockSpec((ATT_HG, 1, ATT_REL_WIDTH), lambda b, g, n: (g, 0, 0)),
                  pl.BlockSpec((MXU_DIM, ATT_QS), lambda b, g, n: (g, b * ns + n)),
                  pl.BlockSpec((None, SEQ + LEFT_PAD, MXU_DIM), lambda b, g, n: (b, 0, g)),
                  pl.BlockSpec((None, MXU_DIM, SEQ + LEFT_PAD), lambda b, g, n: (b, g, 0))],
        out_specs=pl.BlockSpec((ATT_QS, MXU_DIM), lambda b, g, n: (b * ns + n, g)),
        out_shape=jax.ShapeDtypeStruct((TOKENS, D_MODEL), BF16),
        scratch_shapes=[pltpu.VMEM((ATT_PAD_BLOCKS + 1, ATT_HG, ATT_KB, ATT_QB), F32),
                        pltpu.VMEM((ATT_KB, ATT_QB), F32), pltpu.VMEM((ATT_KB, ATT_QB), F32),
                        pltpu.VMEM((ATT_KB, ATT_QB), BF16), pltpu.VMEM((ATT_KB, ATT_QB), BF16)],
        compiler_params=_params(("parallel", "parallel", "arbitrary")),
        name="band_attn",
    )(rel, qt, kp, vt)


def _row_copy(src, src_row, dst, dst_row, sem):
    return pltpu.make_async_copy(src.at[pl.ds(src_row, 1)], dst.at[pl.ds(dst_row, 1)], sem)


def _tile_wait(hbm, vmem_tile, sem):
    pltpu.make_async_copy(hbm.at[pl.ds(0, MOE_TM)], vmem_tile, sem).wait()


def _scatter_tile(xbuf, rs_smem, xs_hbm, sem):
    for t in range(MOE_TM):
        for which in range(TOP_K):
            _row_copy(xbuf, t, xs_hbm, rs_smem[which, t], sem).start(priority=which)


def _route_tile(rows, h_ref, o_ref, wo_ref, nf_ref, wrh_ref, wrl_ref, tri_ref,
                h_out_ref, route_ref, gw_ref, cnt_ref, carry_ref):
    cols = rows
    h = h_ref[rows, :] + _dot(o_ref[rows, :], wo_ref[...])
    h_out_ref[rows, :] = h
    xn = _rms(h, nf_ref[...])

    xh = xn.astype(BF16)
    xl = (xn - xh.astype(F32)).astype(BF16)
    logits = _dot(xh, wrh_ref[...]) + _dot(xl, wrh_ref[...]) + _dot(xh, wrl_ref[...])
    lt = logits.T[:N_EXPERTS]
    row = lax.broadcasted_iota(jnp.int32, lt.shape, 0)
    m1 = jnp.max(lt, axis=0, keepdims=True)
    i1 = jnp.min(jnp.where(lt == m1, row, N_EXPERTS), axis=0, keepdims=True)
    rest = jnp.where(row == i1, -jnp.inf, lt)
    m2 = jnp.max(rest, axis=0, keepdims=True)
    i2 = jnp.min(jnp.where(rest == m2, row, N_EXPERTS), axis=0, keepdims=True)
    e2 = jnp.exp(m2 - m1)
    w1 = 1.0 / (1.0 + e2)
    w2 = e2 / (1.0 + e2)

    member = ((row == i1) | (row == i2)).astype(F32)
    within = _dot(member.astype(BF16), tri_ref[...])
    carry = carry_ref[...]
    rank = within + carry[:, 0:1]
    r1 = jnp.sum(jnp.where(row == i1, rank, 0.0), axis=0, keepdims=True).astype(jnp.int32)
    r2 = jnp.sum(jnp.where(row == i2, rank, 0.0), axis=0, keepdims=True).astype(jnp.int32)
    carry_ref[...] = carry + jnp.sum(member, axis=1, keepdims=True)
    cnt_ref[...] = carry_ref[...].astype(jnp.int32)

    s1 = i1 * MOE_CAP + r1
    s2 = i2 * MOE_CAP + r2
    route_ref[:, cols] = jnp.where(row == 0, s1, jnp.where(row == 1, s2, 0))
    gwt = jnp.where(row == 0, w1, jnp.where(row == 1, w2, 0.0))
    pad = jnp.zeros((LANES - N_EXPERTS, gwt.shape[1]), F32)
    gw_ref[rows, :] = jnp.concatenate([gwt, pad], axis=0).T
    return xn


def _post_b_kernel(h_ref, o_ref, wo_ref, nf_ref, wrh_ref, wrl_ref, tri_ref,
                   h_out_ref, route_ref, gw_ref, cnt_ref, xs_hbm,
                   xbuf0, xbuf1, carry_ref, rs0, rs1, cnt_smem, zero_ref, sems, tail_sem):
    g = pl.program_id(0)
    tile_args = (h_ref, o_ref, wo_ref, nf_ref, wrh_ref, wrl_ref, tri_ref,
                 h_out_ref, route_ref, gw_ref, cnt_ref, carry_ref)
    first = slice(0, MOE_TM)
    second = slice(MOE_TM, 2 * MOE_TM)

    @pl.when(g == 0)
    def _():
        carry_ref[...] = jnp.zeros_like(carry_ref)
        xbuf1[...] = jnp.zeros_like(xbuf1)
        which = lax.broadcasted_iota(jnp.int32, (SUBLANES, MOE_TM), 0)
        token = lax.broadcasted_iota(jnp.int32, (SUBLANES, MOE_TM), 1)
        route_ref[:, first] = MOE_SPARE_ROW + jnp.minimum(which, TOP_K - 1) * MOE_TM + token
        pltpu.sync_copy(route_ref.at[:, first], rs1)

    _scatter_tile(xbuf1, rs1, xs_hbm, sems.at[1])
    xn = _route_tile(first, *tile_args)

    @pl.when(g >= 1)
    def _():
        for _ in range(TOP_K):
            _tile_wait(xs_hbm, xbuf0, sems.at[0])

    xbuf0[...] = xn
    pltpu.sync_copy(route_ref.at[:, first], rs0)

    _scatter_tile(xbuf0, rs0, xs_hbm, sems.at[0])
    xn = _route_tile(second, *tile_args)
    for _ in range(TOP_K):
        _tile_wait(xs_hbm, xbuf1, sems.at[1])
    xbuf1[...] = xn
    pltpu.sync_copy(route_ref.at[:, second], rs1)

    @pl.when(g == pl.num_programs(0) - 1)
    def _():
        _scatter_tile(xbuf1, rs1, xs_hbm, sems.at[1])
        for buf, sem in ((xbuf0, sems.at[0]), (xbuf1, sems.at[1])):
            for _ in range(TOP_K):
                _tile_wait(xs_hbm, buf, sem)

        pltpu.sync_copy(cnt_ref, cnt_smem)
        zero_ref[...] = jnp.zeros_like(zero_ref)
        for e in range(N_EXPERTS):
            count = cnt_smem[e, 0]
            aligned = ((count + (SUBLANES - 1)) // SUBLANES) * SUBLANES
            for part in range(MOE_SUPER // MOE_TM):
                start = pl.multiple_of(e * MOE_CAP + aligned + part * MOE_TM, SUBLANES)
                pltpu.make_async_copy(zero_ref, xs_hbm.at[pl.ds(start, MOE_TM)], tail_sem).start()
            for k in range(SUBLANES - 1):
                @pl.when(count + k < aligned)
                def _():
                    _row_copy(zero_ref, 0, xs_hbm, e * MOE_CAP + count + k, tail_sem).start()
        for e in range(N_EXPERTS):
            count = cnt_smem[e, 0]
            aligned = ((count + (SUBLANES - 1)) // SUBLANES) * SUBLANES
            for part in range(MOE_SUPER // MOE_TM):
                _tile_wait(xs_hbm, zero_ref, tail_sem)
            for k in range(SUBLANES - 1):
                @pl.when(count + k < aligned)
                def _():
                    _row_copy(zero_ref, 0, xs_hbm, 0, tail_sem).wait()


def _post_b(h, o, wo, nf, wr_hi, wr_lo):
    tm = MOE_TM
    pair = 2 * tm
    n = TOKENS // pair
    tile = lambda width: pl.BlockSpec((pair, width), lambda i: (i, 0))
    tri = jnp.asarray(np.triu(np.ones((tm, tm), np.float32), k=1), dtype=BF16)
    return pl.pallas_call(
        _post_b_kernel,
        grid=(n,),
        in_specs=[tile(D_MODEL), tile(D_MODEL), _const_spec(wo.shape), _const_spec((1, D_MODEL)),
                  _const_spec(wr_hi.shape), _const_spec(wr_lo.shape), _const_spec(tri.shape)],
        out_specs=[tile(D_MODEL),
                   pl.BlockSpec((SUBLANES, pair), lambda i: (0, i)),
                   tile(LANES),
                   pl.BlockSpec((SUBLANES, LANES), lambda i: (0, 0)),
                   pl.BlockSpec(memory_space=pl.ANY)],
        out_shape=[jax.ShapeDtypeStruct((TOKENS, D_MODEL), F32),
                   jax.ShapeDtypeStruct((SUBLANES, TOKENS), jnp.int32),
                   jax.ShapeDtypeStruct((TOKENS, LANES), F32),
                   jax.ShapeDtypeStruct((SUBLANES, LANES), jnp.int32),
                   jax.ShapeDtypeStruct((MOE_ROWS, D_MODEL), F32)],
        scratch_shapes=[pltpu.VMEM((tm, D_MODEL), F32),
                        pltpu.VMEM((tm, D_MODEL), F32),
                        pltpu.VMEM((SUBLANES, LANES), F32),
                        pltpu.SMEM((SUBLANES, tm), jnp.int32),
                        pltpu.SMEM((SUBLANES, tm), jnp.int32),
                        pltpu.SMEM((SUBLANES, LANES), jnp.int32),
                        pltpu.VMEM((tm, D_MODEL), F32),
                        pltpu.SemaphoreType.DMA((2,)),
                        pltpu.SemaphoreType.DMA(())],
        compiler_params=_params(("arbitrary",)),
        name="post_b",
    )(h, o, wo, nf, wr_hi, wr_lo, tri)


def _moe_kernel(blk_ref, exp_ref, halves_ref, nact_ref, x_ref, wg_ref, wu_ref, wd_ref, y_ref, xb_ref):
    j = pl.program_id(0)
    f = pl.program_id(1)
    first = slice(0, MOE_TM)
    second = slice(MOE_TM, MOE_SUPER)

    @pl.when(j < nact_ref[0])
    def _():
        def half(rows):
            @pl.when(f == 0)
            def _():
                xb_ref[rows, :] = x_ref[rows, :].astype(BF16)
                y_ref[rows, :] = jnp.zeros((MOE_TM, D_MODEL), F32)

            x = xb_ref[rows, :]
            a = (_silu(_dot(x, wg_ref[...])) * _dot(x, wu_ref[...])).astype(BF16)
            y_ref[rows, :] += _dot(a, wd_ref[...])

        half(first)

        @pl.when(halves_ref[j] == 2)
        def _():
            half(second)

        @pl.when((halves_ref[j] == 1) & (f == 0))
        def _():
            y_ref[second, :] = jnp.zeros((MOE_TM, D_MODEL), F32)


def _moe(xs, tile_block, tile_expert, tile_halves, n_active, wg, wu, wd, tf=D_FF_EXPERT // 2):
    nf = D_FF_EXPERT // tf
    fidx = lambda j, f, nact: jnp.where(j < nact[0], f, nf - 1)
    rows = pl.BlockSpec((MOE_SUPER, D_MODEL), lambda j, f, blk, exp, hv, nact: (blk[j], 0))
    grid_spec = pltpu.PrefetchScalarGridSpec(
        num_scalar_prefetch=4,
        grid=(MOE_MAX_TILES, nf),
        in_specs=[rows,
                  pl.BlockSpec((None, D_MODEL, tf), lambda j, f, blk, exp, hv, nact: (exp[j], 0, fidx(j, f, nact))),
                  pl.BlockSpec((None, D_MODEL, tf), lambda j, f, blk, exp, hv, nact: (exp[j], 0, fidx(j, f, nact))),
                  pl.BlockSpec((None, tf, D_MODEL), lambda j, f, blk, exp, hv, nact: (exp[j], fidx(j, f, nact), 0))],
        out_specs=rows,
        scratch_shapes=[pltpu.VMEM((MOE_SUPER, D_MODEL), BF16)],
    )
    return pl.pallas_call(
        _moe_kernel,
        grid_spec=grid_spec,
        out_shape=jax.ShapeDtypeStruct((MOE_ROWS, D_MODEL), F32),
        compiler_params=_params(("arbitrary", "arbitrary")),
        name="moe",
    )(tile_block, tile_expert, tile_halves, n_active, xs, wg, wu, wd)


def _tile_plan(counts):
    experts = jnp.arange(N_EXPERTS)[None, :]
    halves = (counts + (MOE_TM - 1)) // MOE_TM
    tiles = (halves + 1) // 2
    ends = jnp.cumsum(tiles)
    n_active = ends[-1]
    j = jnp.minimum(jnp.arange(MOE_MAX_TILES, dtype=jnp.int32), n_active - 1)
    expert = jnp.sum((j[:, None] >= ends[None, :]).astype(jnp.int32), axis=1)
    pick = lambda per_expert: jnp.sum(jnp.where(expert[:, None] == experts, per_expert[None, :], 0), axis=1)
    k = j - pick(ends - tiles)
    block = expert * MOE_REGION_TILES + k
    tile_halves = jnp.minimum(pick(halves) - 2 * k, 2)
    i32 = lambda a: a.astype(jnp.int32)
    return i32(block), i32(expert), i32(tile_halves), i32(n_active.reshape(1))


def _gather_tile(ys_hbm, rs_smem, ybuf, sem):
    for t in range(MOE_TM):
        for which in range(TOP_K):
            _row_copy(ys_hbm, rs_smem[which, t], ybuf.at[which], t, sem).start(priority=which)


def _final_kernel(h_ref, p_ref, route_ref, route_next_ref, gw_ref, np_ref, wpu_ref, wpg_ref, nfin_ref,
                  ys_hbm, out_ref, ybuf0, ybuf1, rs_smem, sems):
    first = slice(0, MOE_TM)
    second = slice(MOE_TM, 2 * MOE_TM)

    def combine(rows, ybuf):
        gw = gw_ref[rows, :]
        y = gw[:, 0:1] * ybuf[0] + gw[:, 1:2] * ybuf[1]
        hp = h_ref[rows, :] + y
        gate = jax.nn.sigmoid(_dot(_rms(hp, np_ref[...]).astype(BF16), wpg_ref[...]))
        hp = hp + _dot(p_ref[rows, :].astype(BF16), wpu_ref[...]) * gate
        out_ref[rows, :] = _rms(hp, nfin_ref[...])

    @pl.when(pl.program_id(0) == 0)
    def _():
        pltpu.sync_copy(route_ref.at[:, first], rs_smem)
        _gather_tile(ys_hbm, rs_smem, ybuf0, sems.at[0])

    for which in range(TOP_K):
        _tile_wait(ys_hbm, ybuf0.at[which], sems.at[0])
    pltpu.sync_copy(route_ref.at[:, second], rs_smem)
    _gather_tile(ys_hbm, rs_smem, ybuf1, sems.at[1])
    combine(first, ybuf0)

    for which in range(TOP_K):
        _tile_wait(ys_hbm, ybuf1.at[which], sems.at[1])
    pltpu.sync_copy(route_next_ref, rs_smem)
    _gather_tile(ys_hbm, rs_smem, ybuf0, sems.at[0])
    combine(second, ybuf1)

    @pl.when(pl.program_id(0) == pl.num_programs(0) - 1)
    def _():
        for which in range(TOP_K):
            _tile_wait(ys_hbm, ybuf0.at[which], sems.at[0])


def _final(h, p, route, gw, npl, wpu, wpg, nfin, ys):
    tm = MOE_TM
    pair = 2 * tm
    n = TOKENS // pair
    tile = lambda width: pl.BlockSpec((pair, width), lambda i: (i, 0))
    consts = [npl, wpu, wpg, nfin]
    next_tile = lambda i: (0, jnp.minimum(2 * i + 2, TOKENS // tm - 1))
    return pl.pallas_call(
        _final_kernel,
        grid=(n,),
        in_specs=[tile(D_MODEL), tile(PLE_DIM),
                  pl.BlockSpec((SUBLANES, pair), lambda i: (0, i)),
                  pl.BlockSpec((SUBLANES, tm), next_tile),
                  tile(LANES)]
                 + [_const_spec(c.shape) for c in consts] + [pl.BlockSpec(memory_space=pl.ANY)],
        out_specs=tile(D_MODEL),
        out_shape=jax.ShapeDtypeStruct((TOKENS, D_MODEL), F32),
        scratch_shapes=[pltpu.VMEM((TOP_K, tm, D_MODEL), F32),
                        pltpu.VMEM((TOP_K, tm, D_MODEL), F32),
                        pltpu.SMEM((SUBLANES, tm), jnp.int32),
                        pltpu.SemaphoreType.DMA((2,))],
        compiler_params=_params(("arbitrary",)),
        name="final",
    )(h, p, route, route, gw, *consts, ys)


def kernel(x, p, positions, norm_mix, norm_ffn, norm_ple, w_in_a, ret_gn, w_out_a, norm_kv, w_kv, w_q_b, rel_bias, w_out_b, w_gate_dense, w_up_dense, w_down_dense, w_router, w_gate_moe, w_up_moe, w_down_moe, w_ple_up, w_ple_gate, norm_final):
    bf = lambda w: w.astype(BF16)
    row = lambda g: g.reshape(1, -1).astype(F32)
    h0 = x.reshape(TOKENS, D_MODEL)
    p2 = p.reshape(2, TOKENS, PLE_DIM)
    pos = positions.reshape(TOKENS, 1)
    inv_freq = (1.0 / (ROPE_BASE ** jnp.linspace(0.0, 1.0, ROPE_HALF, dtype=F32))).reshape(1, ROPE_HALF)

    q, k, v, gate = _inproj(h0, pos, row(norm_mix[0]), inv_freq, bf(w_in_a[0]))
    og = _retention(q, k, v, gate, row(ret_gn[0]))
    h1 = _post_a(h0, og, p2[0], bf(w_out_a[0]), row(norm_ffn[0]), bf(w_gate_dense[0]), bf(w_up_dense[0]),
                 bf(w_down_dense[0]), row(norm_ple[0]), bf(w_ple_up[0]), bf(w_ple_gate[0]))

    qt, kp, vt = _qkv(h1, row(norm_mix[1]), row(norm_kv), bf(w_q_b[0]), bf(w_kv))
    ob = _band_attention(qt, kp, vt, _rel_vectors(rel_bias[0]))
    w_router_pad = jnp.zeros((D_MODEL, LANES), F32).at[:, :N_EXPERTS].set(w_router[0])
    wr_hi = bf(w_router_pad)
    wr_lo = bf(w_router_pad - wr_hi.astype(F32))
    h2, route, gw, counts, xs = _post_b(h1, ob, bf(w_out_b[0]), row(norm_ffn[1]), wr_hi, wr_lo)
    tile_block, tile_expert, tile_halves, n_active = _tile_plan(counts[:, 0])
    ys = _moe(xs, tile_block, tile_expert, tile_halves, n_active,
              bf(w_gate_moe[0]), bf(w_up_moe[0]), bf(w_down_moe[0]))
    out = _final(h2, p2[1], route, gw, row(norm_ple[1]), bf(w_ple_up[1]), bf(w_ple_gate[1]), row(norm_final), ys)
    return out.reshape(BATCH, SEQ, D_MODEL)
```

```python
import functools

import numpy as np
import jax
import jax.numpy as jnp
from jax import lax
from jax.experimental import pallas as pl
from jax.experimental.pallas import tpu as pltpu

F32 = jnp.float32
BF16 = jnp.bfloat16

D_MODEL = 1024
BATCH = 2
SEQ = 8192
TOKENS = BATCH * SEQ
CHUNK = 64
PLE_DIM = 256

RET_HEADS = 4
RET_QK_DIM = 256
RET_V_DIM = 512
RET_QK_WIDTH = RET_HEADS * RET_QK_DIM
RET_V_WIDTH = RET_HEADS * RET_V_DIM
ROPE_BASE = 10000.0
ROPE_HALF = RET_QK_DIM // 2

ATT_HEADS = 16
ATT_HEAD_DIM = 64
LEFT_CHUNKS = 8
LEFT_PAD = LEFT_CHUNKS * CHUNK
REL_CLIP = 256

D_FF_DENSE = 2816
N_EXPERTS = 8
D_FF_EXPERT = 3584
EPS = 1e-6

LANES = 128
MXU_DIM = 256
VMEM_LIMIT = 56 * 1024 * 1024

RET_SUB = MXU_DIM
RET_BLOCK = 1024
ATT_QB = MXU_DIM
ATT_KB = ATT_QB + LEFT_PAD
ATT_QS = 2048
ATT_REL_WIDTH = 1024
ATT_PAD_BLOCKS = LEFT_PAD // ATT_QB
ATT_LIVE = tuple((slice(LANES * t, LANES * t + LEFT_PAD + LANES), slice(LANES * t, LANES * (t + 1)))
                 for t in range(ATT_QB // LANES))
LOG2E = 1.4426950408889634
ATT_HG = MXU_DIM // ATT_HEAD_DIM
NEG_BIG = -1e30

TOP_K = 2
MOE_TM = 512
MOE_SUPER = 2 * MOE_TM
MOE_CAP = TOKENS + 2 * MOE_SUPER
MOE_REGION_TILES = MOE_CAP // MOE_SUPER
MOE_MAX_TILES = TOP_K * TOKENS // MOE_SUPER + N_EXPERTS
MOE_SPARE_ROW = N_EXPERTS * MOE_CAP
MOE_ROWS = MOE_SPARE_ROW + TOP_K * MOE_TM
SUBLANES = 8
ROW_TILE = (SUBLANES, LANES)
assert SUBLANES * LANES == D_MODEL


def _dot(a, b):
    return jnp.dot(a, b, preferred_element_type=F32)


def _rms(x, g):
    return x * lax.rsqrt(jnp.mean(x * x, axis=-1, keepdims=True) + EPS) * g


def _silu(x):
    return x * jax.nn.sigmoid(x)


def _const_spec(shape):
    nd = len(shape)
    return pl.BlockSpec(shape, lambda *_: (0,) * nd, pipeline_mode=pl.Buffered(1))


def _params(sem):
    return pltpu.CompilerParams(dimension_semantics=sem, vmem_limit_bytes=VMEM_LIMIT)


def _inproj_kernel(x_ref, pos_ref, g_ref, invf_ref, w_ref, q_ref, k_ref, v_ref, gate_ref):
    xn = _rms(x_ref[...], g_ref[...]).astype(BF16)
    v0 = 2 * RET_QK_WIDTH
    g0 = v0 + RET_V_WIDTH
    for c in range(RET_HEADS):
        lo, hi = c * RET_V_DIM, (c + 1) * RET_V_DIM
        v_ref[:, lo:hi] = _dot(xn, w_ref[:, v0 + lo:v0 + hi]).astype(BF16)
        gate_ref[:, lo:hi] = _silu(_dot(xn, w_ref[:, g0 + lo:g0 + hi])).astype(BF16)
    ang = pos_ref[...].astype(F32) * invf_ref[...]
    cos = jnp.cos(ang)
    sin = jnp.sin(ang)
    k_scale = RET_QK_DIM ** -0.5
    for h in range(RET_HEADS):
        lo = h * RET_QK_DIM
        mid = lo + ROPE_HALF
        hi = lo + RET_QK_DIM
        pq = _dot(xn, w_ref[:, lo:hi])
        x1, x2 = pq[:, :ROPE_HALF], pq[:, ROPE_HALF:]
        q_ref[:, lo:mid] = (x1 * cos - x2 * sin).astype(BF16)
        q_ref[:, mid:hi] = (x1 * sin + x2 * cos).astype(BF16)
        pk = _dot(xn, w_ref[:, RET_QK_WIDTH + lo:RET_QK_WIDTH + hi])
        x1, x2 = pk[:, :ROPE_HALF], pk[:, ROPE_HALF:]
        k_ref[:, lo:mid] = ((x1 * cos - x2 * sin) * k_scale).astype(BF16)
        k_ref[:, mid:hi] = ((x1 * sin + x2 * cos) * k_scale).astype(BF16)


def _inproj(x, pos, g, invf, w, tm=512):
    n = TOKENS // tm
    tile = lambda width: pl.BlockSpec((tm, width), lambda i: (i, 0))
    return pl.pallas_call(
        _inproj_kernel,
        grid=(n,),
        in_specs=[tile(D_MODEL), tile(1), _const_spec((1, D_MODEL)), _const_spec((1, ROPE_HALF)),
                  _const_spec(w.shape)],
        out_specs=[tile(RET_QK_WIDTH), tile(RET_QK_WIDTH), tile(RET_V_WIDTH), tile(RET_V_WIDTH)],
        out_shape=[jax.ShapeDtypeStruct((TOKENS, RET_QK_WIDTH), BF16),
                   jax.ShapeDtypeStruct((TOKENS, RET_QK_WIDTH), BF16),
                   jax.ShapeDtypeStruct((TOKENS, RET_V_WIDTH), BF16),
                   jax.ShapeDtypeStruct((TOKENS, RET_V_WIDTH), BF16)],
        compiler_params=_params(("parallel",)),
        name="inproj",
    )(x, pos, g, invf, w)


def _ret_kernel(gc_ref, q_ref, k_ref, v_ref, gate_ref, dm_ref, xi_ref, zeta_ref, gn_ref,
                o_ref, state_ref):
    h = pl.program_id(1)

    @pl.when(pl.program_id(2) == 0)
    def _():
        state_ref[...] = jnp.zeros_like(state_ref)

    g_chunk = gc_ref[h]
    dm = dm_ref[0]
    xi = xi_ref[0]
    zeta = zeta_ref[0]
    gn = gn_ref[...]
    for j in range(RET_BLOCK // RET_SUB):
        sl = slice(j * RET_SUB, (j + 1) * RET_SUB)
        qj = q_ref[sl, :]
        kj = k_ref[sl, :]
        vj = v_ref[sl, :]
        st = state_ref[...]
        s = lax.dot_general(qj, kj, (((1,), (1,)), ((), ())), preferred_element_type=F32) * dm
        inner = _dot(s.astype(BF16), vj)
        cross = _dot(qj, st.astype(BF16)) * xi
        kz = (kj.astype(F32) * zeta).astype(BF16)
        upd = lax.dot_general(kz, vj, (((0,), (0,)), ((), ())), preferred_element_type=F32)
        state_ref[...] = st * g_chunk + upd
        o = inner + cross
        on = _rms(o, gn)
        o_ref[sl, :] = (on * gate_ref[sl, :].astype(F32)).astype(BF16)


def _retention(q, k, v, gate, gn):
    lg = jnp.log1p(-jnp.exp2(-5.0 - jnp.arange(RET_HEADS, dtype=F32)))
    idx = jnp.arange(RET_SUB, dtype=F32)
    diff = idx[:, None] - idx[None, :]
    causal = diff >= 0
    dmask = jnp.where(causal, jnp.exp(jnp.where(causal, diff, 0.0)[None] * lg[:, None, None]), 0.0)
    xi = jnp.exp((idx[None, :] + 1.0) * lg[:, None])[:, :, None]
    zeta = jnp.exp((RET_SUB - 1.0 - idx)[None, :] * lg[:, None])[:, :, None]
    g_chunk = jnp.exp(RET_SUB * lg)
    nb = SEQ // RET_BLOCK
    row = lambda b, h, n: (b * nb + n, h)
    per_head = lambda shape: pl.BlockSpec((1,) + shape, lambda b, h, n: (h, 0, 0))
    return pl.pallas_call(
        _ret_kernel,
        grid=(BATCH, RET_HEADS, nb),
        in_specs=[pl.BlockSpec(memory_space=pltpu.SMEM),
                  pl.BlockSpec((RET_BLOCK, RET_QK_DIM), row),
                  pl.BlockSpec((RET_BLOCK, RET_QK_DIM), row),
                  pl.BlockSpec((RET_BLOCK, RET_V_DIM), row),
                  pl.BlockSpec((RET_BLOCK, RET_V_DIM), row),
                  per_head((RET_SUB, RET_SUB)), per_head((RET_SUB, 1)), per_head((RET_SUB, 1)),
                  pl.BlockSpec((1, RET_V_DIM), lambda b, h, n: (0, h))],
        out_specs=pl.BlockSpec((RET_BLOCK, RET_V_DIM), row),
        out_shape=jax.ShapeDtypeStruct((TOKENS, RET_V_WIDTH), BF16),
        scratch_shapes=[pltpu.VMEM((RET_QK_DIM, RET_V_DIM), F32)],
        compiler_params=_params(("parallel", "parallel", "arbitrary")),
        name="retention",
    )(g_chunk, q, k, v, gate, dmask, xi, zeta, gn)


def _ple(h, p_ref, np_ref, wpu_ref, wpg_ref):
    gate = jax.nn.sigmoid(_dot(_rms(h, np_ref[...]).astype(BF16), wpg_ref[...]))
    return h + _dot(p_ref[...].astype(BF16), wpu_ref[...]) * gate


def _post_a_kernel(h_ref, og_ref, p_ref, wo_ref, nf_ref, wg_ref, wu_ref, wd_ref, np_ref, wpu_ref,
                   wpg_ref, out_ref):
    h1 = h_ref[...] + _dot(og_ref[...], wo_ref[...])
    xn = _rms(h1, nf_ref[...]).astype(BF16)
    a = (_silu(_dot(xn, wg_ref[...])) * _dot(xn, wu_ref[...])).astype(BF16)
    h2 = h1 + _dot(a, wd_ref[...])
    out_ref[...] = _ple(h2, p_ref, np_ref, wpu_ref, wpg_ref)


def _post_a(h, og, p, wo, nf, wg, wu, wd, npl, wpu, wpg, tm=512):
    n = TOKENS // tm
    tile = lambda width: pl.BlockSpec((tm, width), lambda i: (i, 0))
    consts = [wo, nf, wg, wu, wd, npl, wpu, wpg]
    return pl.pallas_call(
        _post_a_kernel,
        grid=(n,),
        in_specs=[tile(D_MODEL), tile(RET_V_WIDTH), tile(PLE_DIM)] + [_const_spec(c.shape) for c in consts],
        out_specs=tile(D_MODEL),
        out_shape=jax.ShapeDtypeStruct((TOKENS, D_MODEL), F32),
        compiler_params=_params(("parallel",)),
        name="post_a",
    )(h, og, p, *consts)


def _qkv_kernel(h_ref, gq_ref, gkv_ref, wq_ref, wkv_ref, qt_ref, k_ref, vt_ref):
    h = h_ref[...]
    hn = h * lax.rsqrt(jnp.mean(h * h, axis=-1, keepdims=True) + EPS)
    q = _dot((hn * gq_ref[...]).astype(BF16), wq_ref[...]) * (ATT_HEAD_DIM ** -0.5 * LOG2E)
    qt_ref[...] = q.T.astype(BF16)
    kv = _dot((hn * gkv_ref[...]).astype(BF16), wkv_ref[...])
    keep = (pl.program_id(1) > 0).astype(F32)
    k_ref[...] = (kv[:, :D_MODEL] * keep).astype(BF16)
    vt_ref[...] = (kv[:, D_MODEL:] * keep).T.astype(BF16)


def _qkv(h, gq, gkv, wq, wkv):
    tm = LEFT_PAD
    nb = SEQ // tm
    src = lambda b, j: (b * nb + jnp.maximum(j - 1, 0), 0)
    return pl.pallas_call(
        _qkv_kernel,
        grid=(BATCH, nb + 1),
        in_specs=[pl.BlockSpec((tm, D_MODEL), src), _const_spec((1, D_MODEL)), _const_spec((1, D_MODEL)),
                  _const_spec(wq.shape), _const_spec(wkv.shape)],
        out_specs=[pl.BlockSpec((D_MODEL, tm), lambda b, j: (0, b * nb + jnp.maximum(j - 1, 0))),
                   pl.BlockSpec((None, tm, D_MODEL), lambda b, j: (b, j, 0)),
                   pl.BlockSpec((None, D_MODEL, tm), lambda b, j: (b, 0, j))],
        out_shape=[jax.ShapeDtypeStruct((D_MODEL, TOKENS), BF16),
                   jax.ShapeDtypeStruct((BATCH, SEQ + LEFT_PAD, D_MODEL), BF16),
                   jax.ShapeDtypeStruct((BATCH, D_MODEL, SEQ + LEFT_PAD), BF16)],
        compiler_params=_params(("parallel", "arbitrary")),
        name="qkv_b",
    )(h, gq, gkv, wq, wkv)


def _attn_kernel(rel_ref, qt_ref, k_ref, vt_ref, o_ref, bias_ref, s0_ref, s1_ref, p0_ref, p1_ref):
    n = pl.program_id(2)

    @pl.when(n == 0)
    def _():
        kj = lax.broadcasted_iota(jnp.int32, (ATT_KB, ATT_QB), 0)
        qi = lax.broadcasted_iota(jnp.int32, (ATT_KB, ATT_QB), 1)
        k_chunk = kj // CHUNK - LEFT_CHUNKS
        q_chunk = qi // CHUNK
        allowed = (k_chunk <= q_chunk) & (k_chunk >= q_chunk - LEFT_CHUNKS)
        for g in range(ATT_HG):
            base = jnp.broadcast_to(rel_ref[g], (ATT_KB, ATT_REL_WIDTH))
            rolled = pltpu.roll(base, 0, 1, stride=1, stride_axis=0)
            table = jnp.where(allowed, rolled[:, :ATT_QB], NEG_BIG)
            bias_ref[0, g] = table
            for v in range(1, ATT_PAD_BLOCKS + 1):
                bias_ref[v, g] = jnp.where(kj >= LEFT_PAD - (v - 1) * ATT_QB, table, NEG_BIG)
        p0_ref[...] = jnp.zeros_like(p0_ref)
        p1_ref[...] = jnp.zeros_like(p1_ref)

    head_of_row = lax.broadcasted_iota(jnp.int32, (MXU_DIM, ATT_QB), 0) // ATT_HEAD_DIM
    n_blocks = ATT_QS // ATT_QB

    def window(qb):
        qb = jnp.minimum(qb, n_blocks - 1)
        q0 = pl.multiple_of(qb * ATT_QB, ATT_QB)
        kstart = pl.multiple_of(n * ATT_QS + qb * ATT_QB, ATT_QB)
        return q0, kstart

    def scores(qb, g, s_ref):
        q0, kstart = window(qb)
        qt = qt_ref[:, pl.ds(q0, ATT_QB)]
        kb = k_ref[pl.ds(kstart, ATT_KB), :]
        qg = jnp.where(head_of_row == g, qt, jnp.zeros_like(qt))
        block_in_seq = kstart // ATT_QB
        variant = jnp.where(block_in_seq < ATT_PAD_BLOCKS, block_in_seq + 1, 0)
        s = _dot(kb, qg)
        maxes = []
        for rows, lanes in ATT_LIVE:
            part = s[rows, lanes] + bias_ref[variant, g, rows, lanes]
            s_ref[rows, lanes] = part
            maxes.append(jnp.max(part, axis=0, keepdims=True))
        return jnp.concatenate(maxes, axis=1)

    def probabilities(s_ref, p_ref, m):
        sums = []
        for rows, lanes in ATT_LIVE:
            p = jnp.exp2(s_ref[rows, lanes] - m[:, lanes])
            p_ref[rows, lanes] = p.astype(BF16)
            sums.append(jnp.sum(p, axis=0, keepdims=True))
        return jnp.concatenate(sums, axis=1)

    def values(qb, g, p_ref, l):
        _, kstart = window(qb)
        vg = vt_ref[g * ATT_HEAD_DIM:(g + 1) * ATT_HEAD_DIM, pl.ds(kstart, ATT_KB)]
        return _dot(vg, p_ref[...]) / l

    s_bufs = (s0_ref, s1_ref)
    p_bufs = (p0_ref, p1_ref)
    m_first = scores(0, 0, s_bufs[0])
    m_second = scores(0, 1, s_bufs[1])
    l_first = probabilities(s_bufs[0], p_bufs[0], m_first)

    def body(qb, carry):
        l_cur, m_next = carry
        outs = []
        for g in range(ATT_HG):
            slot = g % 2
            outs.append(values(qb, g, p_bufs[slot], l_cur))
            l_cur = probabilities(s_bufs[1 - slot], p_bufs[1 - slot], m_next)
            ahead = g + 2
            m_next = scores(qb + ahead // ATT_HG, ahead % ATT_HG, s_bufs[slot])
        q0, _ = window(qb)
        o_ref[pl.ds(q0, ATT_QB), :] = jnp.concatenate(outs, axis=0).T.astype(BF16)
        return l_cur, m_next

    lax.fori_loop(0, n_blocks, body, (l_first, m_second))


def _rel_vectors(rel_table):
    heads, n_rel = rel_table.shape
    far = rel_table[:, n_rel - 1:]
    near = rel_table[:, :1]
    n_far_front = ATT_QB + 1
    n_near = ATT_REL_WIDTH - LEFT_PAD - (CHUNK - 1) - n_far_front
    n_far_back = ATT_REL_WIDTH - n_far_front - n_near - n_rel
    g = jnp.concatenate([jnp.broadcast_to(far, (heads, n_far_front)),
                         jnp.broadcast_to(near, (heads, n_near)),
                         rel_table,
                         jnp.broadcast_to(far, (heads, n_far_back))], axis=1)
    return (g.astype(F32) * LOG2E).reshape(heads, 1, ATT_REL_WIDTH)


def _band_attention(qt, kp, vt, rel):
    ns = SEQ // ATT_QS
    return pl.pallas_call(
        _attn_kernel,
        grid=(BATCH, ATT_HEADS // ATT_HG, ns),
        in_specs=[pl.BlockSpec((ATT_HG, 1, ATT_REL_WIDTH), lambda b, g, n: (g, 0, 0)),
                  pl.BlockSpec((MXU_DIM, ATT_QS), lambda b, g, n: (g, b * ns + n)),
                  pl.BlockSpec((None, SEQ + LEFT_PAD, MXU_DIM), lambda b, g, n: (b, 0, g)),
                  pl.BlockSpec((None, MXU_DIM, SEQ + LEFT_PAD), lambda b, g, n: (b, g, 0))],
        out_specs=pl.BlockSpec((ATT_QS, MXU_DIM), lambda b, g, n: (b * ns + n, g)),
        out_shape=jax.ShapeDtypeStruct((TOKENS, D_MODEL), BF16),
        scratch_shapes=[pltpu.VMEM((ATT_PAD_BLOCKS + 1, ATT_HG, ATT_KB, ATT_QB), F32),
                        pltpu.VMEM((ATT_KB, ATT_QB), F32), pltpu.VMEM((ATT_KB, ATT_QB), F32),
                        pltpu.VMEM((ATT_KB, ATT_QB), BF16), pltpu.VMEM((ATT_KB, ATT_QB), BF16)],
        compiler_params=_params(("parallel", "parallel", "arbitrary")),
        name="band_attn",
    )(rel, qt, kp, vt)


def _row_copy(src, src_row, dst, dst_row, sem):
    return pltpu.make_async_copy(src.at[pl.ds(src_row, 1)], dst.at[pl.ds(dst_row, 1)], sem)


def _tile_wait(hbm, vmem_tile, sem):
    pltpu.make_async_copy(hbm.at[pl.ds(0, MOE_TM)], vmem_tile, sem).wait()


def _scatter_tile(xbuf, rs_smem, xs_hbm, sem):
    for t in range(MOE_TM):
        for which in range(TOP_K):
            _row_copy(xbuf, t, xs_hbm, rs_smem[which, t], sem).start(priority=which)


def _route_tile(rows, h_ref, o_ref, wo_ref, nf_ref, wrh_ref, wrl_ref, tri_ref,
                h_out_ref, route_ref, gw_ref, cnt_ref, carry_ref):
    cols = rows
    h = h_ref[rows, :] + _dot(o_ref[rows, :], wo_ref[...])
    h_out_ref[rows, :] = h
    xn = _rms(h, nf_ref[...])

    xh = xn.astype(BF16)
    xl = (xn - xh.astype(F32)).astype(BF16)
    logits = _dot(xh, wrh_ref[...]) + _dot(xl, wrh_ref[...]) + _dot(xh, wrl_ref[...])
    lt = logits.T[:N_EXPERTS]
    row = lax.broadcasted_iota(jnp.int32, lt.shape, 0)
    m1 = jnp.max(lt, axis=0, keepdims=True)
    i1 = jnp.min(jnp.where(lt == m1, row, N_EXPERTS), axis=0, keepdims=True)
    rest = jnp.where(row == i1, -jnp.inf, lt)
    m2 = jnp.max(rest, axis=0, keepdims=True)
    i2 = jnp.min(jnp.where(rest == m2, row, N_EXPERTS), axis=0, keepdims=True)
    e2 = jnp.exp(m2 - m1)
    w1 = 1.0 / (1.0 + e2)
    w2 = e2 / (1.0 + e2)

    member = ((row == i1) | (row == i2)).astype(F32)
    within = _dot(member.astype(BF16), tri_ref[...])
    carry = carry_ref[...]
    rank = within + carry[:, 0:1]
    r1 = jnp.sum(jnp.where(row == i1, rank, 0.0), axis=0, keepdims=True).astype(jnp.int32)
    r2 = jnp.sum(jnp.where(row == i2, rank, 0.0), axis=0, keepdims=True).astype(jnp.int32)
    carry_ref[...] = carry + jnp.sum(member, axis=1, keepdims=True)
    cnt_ref[...] = carry_ref[...].astype(jnp.int32)

    s1 = i1 * MOE_CAP + r1
    s2 = i2 * MOE_CAP + r2
    route_ref[:, cols] = jnp.where(row == 0, s1, jnp.where(row == 1, s2, 0))
    gwt = jnp.where(row == 0, w1, jnp.where(row == 1, w2, 0.0))
    pad = jnp.zeros((LANES - N_EXPERTS, gwt.shape[1]), F32)
    gw_ref[rows, :] = jnp.concatenate([gwt, pad], axis=0).T
    return xn


def _post_b_kernel(h_ref, o_ref, wo_ref, nf_ref, wrh_ref, wrl_ref, tri_ref,
                   h_out_ref, route_ref, gw_ref, cnt_ref, xs_hbm,
                   xbuf0, xbuf1, carry_ref, rs0, rs1, cnt_smem, zero_ref, sems, tail_sem):
    g = pl.program_id(0)
    tile_args = (h_ref, o_ref, wo_ref, nf_ref, wrh_ref, wrl_ref, tri_ref,
                 h_out_ref, route_ref, gw_ref, cnt_ref, carry_ref)
    first = slice(0, MOE_TM)
    second = slice(MOE_TM, 2 * MOE_TM)

    @pl.when(g == 0)
    def _():
        carry_ref[...] = jnp.zeros_like(carry_ref)
        xbuf1[...] = jnp.zeros_like(xbuf1)
        which = lax.broadcasted_iota(jnp.int32, (SUBLANES, MOE_TM), 0)
        token = lax.broadcasted_iota(jnp.int32, (SUBLANES, MOE_TM), 1)
        route_ref[:, first] = MOE_SPARE_ROW + jnp.minimum(which, TOP_K - 1) * MOE_TM + token
        pltpu.sync_copy(route_ref.at[:, first], rs1)

    _scatter_tile(xbuf1, rs1, xs_hbm, sems.at[1])
    xn = _route_tile(first, *tile_args)

    @pl.when(g >= 1)
    def _():
        for _ in range(TOP_K):
            _tile_wait(xs_hbm, xbuf0, sems.at[0])

    xbuf0[...] = xn.reshape(xbuf0.shape)
    pltpu.sync_copy(route_ref.at[:, first], rs0)

    _scatter_tile(xbuf0, rs0, xs_hbm, sems.at[0])
    xn = _route_tile(second, *tile_args)
    for _ in range(TOP_K):
        _tile_wait(xs_hbm, xbuf1, sems.at[1])
    xbuf1[...] = xn.reshape(xbuf1.shape)
    pltpu.sync_copy(route_ref.at[:, second], rs1)

    @pl.when(g == pl.num_programs(0) - 1)
    def _():
        _scatter_tile(xbuf1, rs1, xs_hbm, sems.at[1])
        for buf, sem in ((xbuf0, sems.at[0]), (xbuf1, sems.at[1])):
            for _ in range(TOP_K):
                _tile_wait(xs_hbm, buf, sem)

        pltpu.sync_copy(cnt_ref, cnt_smem)
        zero_ref[...] = jnp.zeros_like(zero_ref)
        for e in range(N_EXPERTS):
            for part in range(MOE_SUPER // MOE_TM):
                start = e * MOE_CAP + cnt_smem[e, 0] + part * MOE_TM
                pltpu.make_async_copy(zero_ref, xs_hbm.at[pl.ds(start, MOE_TM)], tail_sem).start()
        for _ in range(N_EXPERTS * (MOE_SUPER // MOE_TM)):
            _tile_wait(xs_hbm, zero_ref, tail_sem)


def _post_b(h, o, wo, nf, wr_hi, wr_lo):
    tm = MOE_TM
    pair = 2 * tm
    n = TOKENS // pair
    tile = lambda width: pl.BlockSpec((pair, width), lambda i: (i, 0))
    tri = jnp.asarray(np.triu(np.ones((tm, tm), np.float32), k=1), dtype=BF16)
    return pl.pallas_call(
        _post_b_kernel,
        grid=(n,),
        in_specs=[tile(D_MODEL), tile(D_MODEL), _const_spec(wo.shape), _const_spec((1, D_MODEL)),
                  _const_spec(wr_hi.shape), _const_spec(wr_lo.shape), _const_spec(tri.shape)],
        out_specs=[tile(D_MODEL),
                   pl.BlockSpec((SUBLANES, pair), lambda i: (0, i)),
                   tile(LANES),
                   pl.BlockSpec((SUBLANES, LANES), lambda i: (0, 0)),
                   pl.BlockSpec(memory_space=pl.ANY)],
        out_shape=[jax.ShapeDtypeStruct((TOKENS, D_MODEL), F32),
                   jax.ShapeDtypeStruct((SUBLANES, TOKENS), jnp.int32),
                   jax.ShapeDtypeStruct((TOKENS, LANES), F32),
                   jax.ShapeDtypeStruct((SUBLANES, LANES), jnp.int32),
                   jax.ShapeDtypeStruct((MOE_ROWS,) + ROW_TILE, F32)],
        scratch_shapes=[pltpu.VMEM((tm,) + ROW_TILE, F32),
                        pltpu.VMEM((tm,) + ROW_TILE, F32),
                        pltpu.VMEM((SUBLANES, LANES), F32),
                        pltpu.SMEM((SUBLANES, tm), jnp.int32),
                        pltpu.SMEM((SUBLANES, tm), jnp.int32),
                        pltpu.SMEM((SUBLANES, LANES), jnp.int32),
                        pltpu.VMEM((tm,) + ROW_TILE, F32),
                        pltpu.SemaphoreType.DMA((2,)),
                        pltpu.SemaphoreType.DMA(())],
        compiler_params=_params(("arbitrary",)),
        name="post_b",
    )(h, o, wo, nf, wr_hi, wr_lo, tri)


def _moe_kernel(blk_ref, exp_ref, halves_ref, nact_ref, x_ref, wg_ref, wu_ref, wd_ref, y_ref, xb_ref,
                acc_ref):
    j = pl.program_id(0)
    f = pl.program_id(1)
    first = slice(0, MOE_TM)
    second = slice(MOE_TM, MOE_SUPER)

    @pl.when(j < nact_ref[0])
    def _():
        last = pl.num_programs(1) - 1

        def half(rows):
            @pl.when(f == 0)
            def _():
                xb_ref[rows, :] = x_ref[rows].reshape(MOE_TM, D_MODEL).astype(BF16)

            x = xb_ref[rows, :]
            a = (_silu(_dot(x, wg_ref[...])) * _dot(x, wu_ref[...])).astype(BF16)
            part = _dot(a, wd_ref[...])

            @pl.when(f == 0)
            def _():
                acc_ref[rows, :] = part

            @pl.when((f > 0) & (f < last))
            def _():
                acc_ref[rows, :] += part

            @pl.when(f == last)
            def _():
                y_ref[rows] = (acc_ref[rows, :] + part).reshape((MOE_TM,) + ROW_TILE)

        half(first)

        @pl.when(halves_ref[j] == 2)
        def _():
            half(second)

        @pl.when((halves_ref[j] == 1) & (f == 0))
        def _():
            y_ref[second] = jnp.zeros((MOE_TM,) + ROW_TILE, F32)


def _moe(xs, tile_block, tile_expert, tile_halves, n_active, wg, wu, wd, tf=D_FF_EXPERT // 2):
    nf = D_FF_EXPERT // tf
    assert nf >= 2 and nf * tf == D_FF_EXPERT
    fidx = lambda j, f, nact: jnp.where(j < nact[0], f, nf - 1)
    rows = pl.BlockSpec((MOE_SUPER,) + ROW_TILE, lambda j, f, blk, exp, hv, nact: (blk[j], 0, 0))
    grid_spec = pltpu.PrefetchScalarGridSpec(
        num_scalar_prefetch=4,
        grid=(MOE_MAX_TILES, nf),
        in_specs=[rows,
                  pl.BlockSpec((None, D_MODEL, tf), lambda j, f, blk, exp, hv, nact: (exp[j], 0, fidx(j, f, nact))),
                  pl.BlockSpec((None, D_MODEL, tf), lambda j, f, blk, exp, hv, nact: (exp[j], 0, fidx(j, f, nact))),
                  pl.BlockSpec((None, tf, D_MODEL), lambda j, f, blk, exp, hv, nact: (exp[j], fidx(j, f, nact), 0))],
        out_specs=rows,
        scratch_shapes=[pltpu.VMEM((MOE_SUPER, D_MODEL), BF16), pltpu.VMEM((MOE_SUPER, D_MODEL), F32)],
    )
    return pl.pallas_call(
        _moe_kernel,
        grid_spec=grid_spec,
        out_shape=jax.ShapeDtypeStruct((MOE_ROWS,) + ROW_TILE, F32),
        compiler_params=_params(("arbitrary", "arbitrary")),
        name="moe",
    )(tile_block, tile_expert, tile_halves, n_active, xs, wg, wu, wd)


def _tile_plan(counts):
    experts = jnp.arange(N_EXPERTS)[None, :]
    halves = (counts + (MOE_TM - 1)) // MOE_TM
    tiles = (halves + 1) // 2
    ends = jnp.cumsum(tiles)
    n_active = ends[-1]
    j = jnp.minimum(jnp.arange(MOE_MAX_TILES, dtype=jnp.int32), n_active - 1)
    expert = jnp.sum((j[:, None] >= ends[None, :]).astype(jnp.int32), axis=1)
    pick = lambda per_expert: jnp.sum(jnp.where(expert[:, None] == experts, per_expert[None, :], 0), axis=1)
    k = j - pick(ends - tiles)
    block = expert * MOE_REGION_TILES + k
    tile_halves = jnp.minimum(pick(halves) - 2 * k, 2)
    i32 = lambda a: a.astype(jnp.int32)
    return i32(block), i32(expert), i32(tile_halves), i32(n_active.reshape(1))


def _gather_tile(ys_hbm, rs_smem, ybuf, sem):
    for t in range(MOE_TM):
        for which in range(TOP_K):
            _row_copy(ys_hbm, rs_smem[which, t], ybuf.at[which], t, sem).start(priority=which)


def _final_kernel(h_ref, p_ref, route_ref, route_next_ref, gw_ref, np_ref, wpu_ref, wpg_ref, nfin_ref,
                  ys_hbm, out_ref, ybuf0, ybuf1, rs_smem, sems):
    first = slice(0, MOE_TM)
    second = slice(MOE_TM, 2 * MOE_TM)

    def combine(rows, ybuf):
        gw = gw_ref[rows, :]
        y = (gw[:, 0:1] * ybuf[0].reshape(MOE_TM, D_MODEL)
             + gw[:, 1:2] * ybuf[1].reshape(MOE_TM, D_MODEL))
        hp = h_ref[rows, :] + y
        gate = jax.nn.sigmoid(_dot(_rms(hp, np_ref[...]).astype(BF16), wpg_ref[...]))
        hp = hp + _dot(p_ref[rows, :].astype(BF16), wpu_ref[...]) * gate
        out_ref[rows, :] = _rms(hp, nfin_ref[...])

    @pl.when(pl.program_id(0) == 0)
    def _():
        pltpu.sync_copy(route_ref.at[:, first], rs_smem)
        _gather_tile(ys_hbm, rs_smem, ybuf0, sems.at[0])

    for which in range(TOP_K):
        _tile_wait(ys_hbm, ybuf0.at[which], sems.at[0])
    pltpu.sync_copy(route_ref.at[:, second], rs_smem)
    _gather_tile(ys_hbm, rs_smem, ybuf1, sems.at[1])
    combine(first, ybuf0)

    for which in range(TOP_K):
        _tile_wait(ys_hbm, ybuf1.at[which], sems.at[1])
    pltpu.sync_copy(route_next_ref, rs_smem)
    _gather_tile(ys_hbm, rs_smem, ybuf0, sems.at[0])
    combine(second, ybuf1)

    @pl.when(pl.program_id(0) == pl.num_programs(0) - 1)
    def _():
        for which in range(TOP_K):
            _tile_wait(ys_hbm, ybuf0.at[which], sems.at[0])


def _final(h, p, route, gw, npl, wpu, wpg, nfin, ys):
    tm = MOE_TM
    pair = 2 * tm
    n = TOKENS // pair
    tile = lambda width: pl.BlockSpec((pair, width), lambda i: (i, 0))
    consts = [npl, wpu, wpg, nfin]
    next_tile = lambda i: (0, jnp.minimum(2 * i + 2, TOKENS // tm - 1))
    return pl.pallas_call(
        _final_kernel,
        grid=(n,),
        in_specs=[tile(D_MODEL), tile(PLE_DIM),
                  pl.BlockSpec((SUBLANES, pair), lambda i: (0, i)),
                  pl.BlockSpec((SUBLANES, tm), next_tile),
                  tile(LANES)]
                 + [_const_spec(c.shape) for c in consts] + [pl.BlockSpec(memory_space=pl.ANY)],
        out_specs=tile(D_MODEL),
        out_shape=jax.ShapeDtypeStruct((TOKENS, D_MODEL), F32),
        scratch_shapes=[pltpu.VMEM((TOP_K, tm) + ROW_TILE, F32),
                        pltpu.VMEM((TOP_K, tm) + ROW_TILE, F32),
                        pltpu.SMEM((SUBLANES, tm), jnp.int32),
                        pltpu.SemaphoreType.DMA((2,))],
        compiler_params=_params(("arbitrary",)),
        name="final",
    )(h, p, route, route, gw, *consts, ys)


def kernel(x, p, positions, norm_mix, norm_ffn, norm_ple, w_in_a, ret_gn, w_out_a, norm_kv, w_kv, w_q_b, rel_bias, w_out_b, w_gate_dense, w_up_dense, w_down_dense, w_router, w_gate_moe, w_up_moe, w_down_moe, w_ple_up, w_ple_gate, norm_final):
    bf = lambda w: w.astype(BF16)
    row = lambda g: g.reshape(1, -1).astype(F32)
    h0 = x.reshape(TOKENS, D_MODEL)
    p2 = p.reshape(2, TOKENS, PLE_DIM)
    pos = positions.reshape(TOKENS, 1)
    inv_freq = (1.0 / (ROPE_BASE ** jnp.linspace(0.0, 1.0, ROPE_HALF, dtype=F32))).reshape(1, ROPE_HALF)

    q, k, v, gate = _inproj(h0, pos, row(norm_mix[0]), inv_freq, bf(w_in_a[0]))
    og = _retention(q, k, v, gate, row(ret_gn[0]))
    h1 = _post_a(h0, og, p2[0], bf(w_out_a[0]), row(norm_ffn[0]), bf(w_gate_dense[0]), bf(w_up_dense[0]),
                 bf(w_down_dense[0]), row(norm_ple[0]), bf(w_ple_up[0]), bf(w_ple_gate[0]))

    qt, kp, vt = _qkv(h1, row(norm_mix[1]), row(norm_kv), bf(w_q_b[0]), bf(w_kv))
    ob = _band_attention(qt, kp, vt, _rel_vectors(rel_bias[0]))
    w_router_pad = jnp.zeros((D_MODEL, LANES), F32).at[:, :N_EXPERTS].set(w_router[0])
    wr_hi = bf(w_router_pad)
    wr_lo = bf(w_router_pad - wr_hi.astype(F32))
    h2, route, gw, counts, xs = _post_b(h1, ob, bf(w_out_b[0]), row(norm_ffn[1]), wr_hi, wr_lo)
    tile_block, tile_expert, tile_halves, n_active = _tile_plan(counts[:, 0])
    ys = _moe(xs, tile_block, tile_expert, tile_halves, n_active,
              bf(w_gate_moe[0]), bf(w_up_moe[0]), bf(w_down_moe[0]))
    out = _final(h2, p2[1], route, gw, row(norm_ple[1]), bf(w_ple_up[1]), bf(w_ple_gate[1]), row(norm_final), ys)
    return out.reshape(BATCH, SEQ, D_MODEL)
```

```python
import functools

import numpy as np
import jax
import jax.numpy as jnp
from jax import lax
from jax.experimental import pallas as pl
from jax.experimental.pallas import tpu as pltpu

F32 = jnp.float32
BF16 = jnp.bfloat16

D_MODEL = 1024
BATCH = 2
SEQ = 8192
TOKENS = BATCH * SEQ
CHUNK = 64
PLE_DIM = 256

RET_HEADS = 4
RET_QK_DIM = 256
RET_V_DIM = 512
RET_QK_WIDTH = RET_HEADS * RET_QK_DIM
RET_V_WIDTH = RET_HEADS * RET_V_DIM
ROPE_BASE = 10000.0
ROPE_HALF = RET_QK_DIM // 2

ATT_HEADS = 16
ATT_HEAD_DIM = 64
LEFT_CHUNKS = 8
LEFT_PAD = LEFT_CHUNKS * CHUNK
REL_CLIP = 256

D_FF_DENSE = 2816
N_EXPERTS = 8
D_FF_EXPERT = 3584
EPS = 1e-6

LANES = 128
MXU_DIM = 256
VMEM_LIMIT = 56 * 1024 * 1024

RET_SUB = MXU_DIM
RET_BLOCK = 512
ATT_QB = MXU_DIM
ATT_KB = ATT_QB + LEFT_PAD
ATT_QS = 4096
ATT_REL_WIDTH = 1024
ATT_PAD_BLOCKS = LEFT_PAD // ATT_QB
ATT_LIVE = tuple((slice(LANES * t, LANES * t + LEFT_PAD + LANES), slice(LANES * t, LANES * (t + 1)))
                 for t in range(ATT_QB // LANES))
LOG2E = 1.4426950408889634
ATT_HG = MXU_DIM // ATT_HEAD_DIM
NEG_BIG = -1e30

TOP_K = 2
MOE_TM = 512
MOE_SUPER = 2 * MOE_TM
MOE_CAP = TOKENS + 2 * MOE_SUPER
MOE_REGION_TILES = MOE_CAP // MOE_SUPER
MOE_MAX_TILES = TOP_K * TOKENS // MOE_SUPER + N_EXPERTS
MOE_SPARE_ROW = N_EXPERTS * MOE_CAP
MOE_ROWS = MOE_SPARE_ROW + TOP_K * MOE_TM
SUBLANES = 8


def _dot(a, b):
    return jnp.dot(a, b, preferred_element_type=F32)


def _rms(x, g):
    return x * lax.rsqrt(jnp.mean(x * x, axis=-1, keepdims=True) + EPS) * g


def _silu(x):
    return x * jax.nn.sigmoid(x)


def _const_spec(shape):
    nd = len(shape)
    return pl.BlockSpec(shape, lambda *_: (0,) * nd, pipeline_mode=pl.Buffered(1))


def _params(sem):
    return pltpu.CompilerParams(dimension_semantics=sem, vmem_limit_bytes=VMEM_LIMIT)


def _inproj_tile(x_ref, pos_ref, g_ref, invf_ref, w_ref, q_ref, k_ref, v_ref, gate_ref):
    xn = _rms(x_ref[...], g_ref[...]).astype(BF16)
    v0 = 2 * RET_QK_WIDTH
    g0 = v0 + RET_V_WIDTH
    for c in range(RET_HEADS):
        lo, hi = c * RET_V_DIM, (c + 1) * RET_V_DIM
        v_ref[:, lo:hi] = _dot(xn, w_ref[:, v0 + lo:v0 + hi]).astype(BF16)
        gate_ref[:, lo:hi] = _silu(_dot(xn, w_ref[:, g0 + lo:g0 + hi])).astype(BF16)
    ang = pos_ref[...].astype(F32) * invf_ref[...]
    cos = jnp.cos(ang)
    sin = jnp.sin(ang)
    k_scale = RET_QK_DIM ** -0.5
    for h in range(RET_HEADS):
        lo = h * RET_QK_DIM
        mid = lo + ROPE_HALF
        hi = lo + RET_QK_DIM
        pq = _dot(xn, w_ref[:, lo:hi])
        x1, x2 = pq[:, :ROPE_HALF], pq[:, ROPE_HALF:]
        q_ref[:, lo:mid] = (x1 * cos - x2 * sin).astype(BF16)
        q_ref[:, mid:hi] = (x1 * sin + x2 * cos).astype(BF16)
        pk = _dot(xn, w_ref[:, RET_QK_WIDTH + lo:RET_QK_WIDTH + hi])
        x1, x2 = pk[:, :ROPE_HALF], pk[:, ROPE_HALF:]
        k_ref[:, lo:mid] = ((x1 * cos - x2 * sin) * k_scale).astype(BF16)
        k_ref[:, mid:hi] = ((x1 * sin + x2 * cos) * k_scale).astype(BF16)


def _retention_kernel(gc_ref, x_ref, pos_ref, g_ref, invf_ref, w_ref, dm_ref, xi_ref, zeta_ref, gn_ref,
                      o_ref, q_ref, k_ref, v_ref, gate_ref, state_ref):
    @pl.when(pl.program_id(1) == 0)
    def _():
        state_ref[...] = jnp.zeros_like(state_ref)

    _inproj_tile(x_ref, pos_ref, g_ref, invf_ref, w_ref, q_ref, k_ref, v_ref, gate_ref)

    for h in range(RET_HEADS):
        qk = slice(h * RET_QK_DIM, (h + 1) * RET_QK_DIM)
        vd = slice(h * RET_V_DIM, (h + 1) * RET_V_DIM)
        g_chunk = gc_ref[h]
        dm = dm_ref[h]
        xi = xi_ref[h]
        zeta = zeta_ref[h]
        gn = gn_ref[:, vd]
        for j in range(RET_BLOCK // RET_SUB):
            sl = slice(j * RET_SUB, (j + 1) * RET_SUB)
            qj = q_ref[sl, qk]
            kj = k_ref[sl, qk]
            vj = v_ref[sl, vd]
            st = state_ref[h]
            s = lax.dot_general(qj, kj, (((1,), (1,)), ((), ())), preferred_element_type=F32) * dm
            inner = _dot(s.astype(BF16), vj)
            cross = _dot(qj, st.astype(BF16)) * xi
            kz = (kj.astype(F32) * zeta).astype(BF16)
            upd = lax.dot_general(kz, vj, (((0,), (0,)), ((), ())), preferred_element_type=F32)
            state_ref[h] = st * g_chunk + upd
            o = inner + cross
            on = _rms(o, gn)
            o_ref[sl, vd] = (on * gate_ref[sl, vd].astype(F32)).astype(BF16)


def _retention_layer(x, pos, g, invf, w, gn):
    lg = jnp.log1p(-jnp.exp2(-5.0 - jnp.arange(RET_HEADS, dtype=F32)))
    idx = jnp.arange(RET_SUB, dtype=F32)
    diff = idx[:, None] - idx[None, :]
    causal = diff >= 0
    dmask = jnp.where(causal, jnp.exp(jnp.where(causal, diff, 0.0)[None] * lg[:, None, None]), 0.0)
    xi = jnp.exp((idx[None, :] + 1.0) * lg[:, None])[:, :, None]
    zeta = jnp.exp((RET_SUB - 1.0 - idx)[None, :] * lg[:, None])[:, :, None]
    g_chunk = jnp.exp(RET_SUB * lg)
    nb = SEQ // RET_BLOCK
    tile = lambda width: pl.BlockSpec((RET_BLOCK, width), lambda b, n: (b * nb + n, 0))
    scratch = lambda width: pltpu.VMEM((RET_BLOCK, width), BF16)
    return pl.pallas_call(
        _retention_kernel,
        grid=(BATCH, nb),
        in_specs=[pl.BlockSpec(memory_space=pltpu.SMEM),
                  tile(D_MODEL), tile(1), _const_spec((1, D_MODEL)), _const_spec((1, ROPE_HALF)),
                  _const_spec(w.shape), _const_spec(dmask.shape), _const_spec(xi.shape),
                  _const_spec(zeta.shape), _const_spec(gn.shape)],
        out_specs=tile(RET_V_WIDTH),
        out_shape=jax.ShapeDtypeStruct((TOKENS, RET_V_WIDTH), BF16),
        scratch_shapes=[scratch(RET_QK_WIDTH), scratch(RET_QK_WIDTH), scratch(RET_V_WIDTH),
                        scratch(RET_V_WIDTH), pltpu.VMEM((RET_HEADS, RET_QK_DIM, RET_V_DIM), F32)],
        compiler_params=_params(("parallel", "arbitrary")),
        name="retention",
    )(g_chunk, x, pos, g, invf, w, dmask, xi, zeta, gn)


def _ple(h, p_ref, np_ref, wpu_ref, wpg_ref):
    gate = jax.nn.sigmoid(_dot(_rms(h, np_ref[...]).astype(BF16), wpg_ref[...]))
    return h + _dot(p_ref[...].astype(BF16), wpu_ref[...]) * gate


def _post_a_kernel(h_ref, og_ref, p_ref, wo_ref, nf_ref, wg_ref, wu_ref, wd_ref, np_ref, wpu_ref,
                   wpg_ref, out_ref):
    h1 = h_ref[...] + _dot(og_ref[...], wo_ref[...])
    xn = _rms(h1, nf_ref[...]).astype(BF16)
    a = (_silu(_dot(xn, wg_ref[...])) * _dot(xn, wu_ref[...])).astype(BF16)
    h2 = h1 + _dot(a, wd_ref[...])
    out_ref[...] = _ple(h2, p_ref, np_ref, wpu_ref, wpg_ref)


def _post_a(h, og, p, wo, nf, wg, wu, wd, npl, wpu, wpg, tm=512):
    n = TOKENS // tm
    tile = lambda width: pl.BlockSpec((tm, width), lambda i: (i, 0))
    consts = [wo, nf, wg, wu, wd, npl, wpu, wpg]
    return pl.pallas_call(
        _post_a_kernel,
        grid=(n,),
        in_specs=[tile(D_MODEL), tile(RET_V_WIDTH), pl.BlockSpec((None, tm, PLE_DIM), lambda i: (0, i, 0))]
                 + [_const_spec(c.shape) for c in consts],
        out_specs=tile(D_MODEL),
        out_shape=jax.ShapeDtypeStruct((TOKENS, D_MODEL), F32),
        compiler_params=_params(("parallel",)),
        name="post_a",
    )(h, og, p, *consts)


def _qkv_kernel(h_ref, gq_ref, gkv_ref, wq_ref, wkv_ref, qt_ref, k_ref, vt_ref):
    h = h_ref[...]
    hn = h * lax.rsqrt(jnp.mean(h * h, axis=-1, keepdims=True) + EPS)
    q = _dot((hn * gq_ref[...]).astype(BF16), wq_ref[...]) * (ATT_HEAD_DIM ** -0.5 * LOG2E)
    qt_ref[...] = q.T.astype(BF16)
    kv = _dot((hn * gkv_ref[...]).astype(BF16), wkv_ref[...])
    keep = (pl.program_id(1) > 0).astype(F32)
    k_ref[...] = (kv[:, :D_MODEL] * keep).astype(BF16)
    vt_ref[...] = (kv[:, D_MODEL:] * keep).T.astype(BF16)


def _qkv(h, gq, gkv, wq, wkv):
    tm = LEFT_PAD
    nb = SEQ // tm
    src = lambda b, j: (b * nb + jnp.maximum(j - 1, 0), 0)
    return pl.pallas_call(
        _qkv_kernel,
        grid=(BATCH, nb + 1),
        in_specs=[pl.BlockSpec((tm, D_MODEL), src), _const_spec((1, D_MODEL)), _const_spec((1, D_MODEL)),
                  _const_spec(wq.shape), _const_spec(wkv.shape)],
        out_specs=[pl.BlockSpec((D_MODEL, tm), lambda b, j: (0, b * nb + jnp.maximum(j - 1, 0))),
                   pl.BlockSpec((None, tm, D_MODEL), lambda b, j: (b, j, 0)),
                   pl.BlockSpec((None, D_MODEL, tm), lambda b, j: (b, 0, j))],
        out_shape=[jax.ShapeDtypeStruct((D_MODEL, TOKENS), BF16),
                   jax.ShapeDtypeStruct((BATCH, SEQ + LEFT_PAD, D_MODEL), BF16),
                   jax.ShapeDtypeStruct((BATCH, D_MODEL, SEQ + LEFT_PAD), BF16)],
        compiler_params=_params(("parallel", "arbitrary")),
        name="qkv_b",
    )(h, gq, gkv, wq, wkv)


def _attn_kernel(rel_ref, qt_ref, k_ref, vt_ref, o_ref, bias_ref, s0_ref, s1_ref, p0_ref, p1_ref):
    n = pl.program_id(2)

    @pl.when(n == 0)
    def _():
        kj = lax.broadcasted_iota(jnp.int32, (ATT_KB, ATT_QB), 0)
        qi = lax.broadcasted_iota(jnp.int32, (ATT_KB, ATT_QB), 1)
        k_chunk = kj // CHUNK - LEFT_CHUNKS
        q_chunk = qi // CHUNK
        allowed = (k_chunk <= q_chunk) & (k_chunk >= q_chunk - LEFT_CHUNKS)
        for g in range(ATT_HG):
            base = jnp.broadcast_to(rel_ref[g], (ATT_KB, ATT_REL_WIDTH))
            rolled = pltpu.roll(base, 0, 1, stride=1, stride_axis=0)
            table = jnp.where(allowed, rolled[:, :ATT_QB], NEG_BIG)
            bias_ref[0, g] = table
            for v in range(1, ATT_PAD_BLOCKS + 1):
                bias_ref[v, g] = jnp.where(kj >= LEFT_PAD - (v - 1) * ATT_QB, table, NEG_BIG)
        p0_ref[...] = jnp.zeros_like(p0_ref)
        p1_ref[...] = jnp.zeros_like(p1_ref)

    head_of_row = lax.broadcasted_iota(jnp.int32, (MXU_DIM, ATT_QB), 0) // ATT_HEAD_DIM
    n_blocks = ATT_QS // ATT_QB

    def window(qb):
        qb = jnp.minimum(qb, n_blocks - 1)
        q0 = pl.multiple_of(qb * ATT_QB, ATT_QB)
        kstart = pl.multiple_of(n * ATT_QS + qb * ATT_QB, ATT_QB)
        return q0, kstart

    def scores(qb, g, s_ref):
        q0, kstart = window(qb)
        qt = qt_ref[:, pl.ds(q0, ATT_QB)]
        kb = k_ref[pl.ds(kstart, ATT_KB), :]
        qg = jnp.where(head_of_row == g, qt, jnp.zeros_like(qt))
        block_in_seq = kstart // ATT_QB
        variant = jnp.where(block_in_seq < ATT_PAD_BLOCKS, block_in_seq + 1, 0)
        s = _dot(kb, qg)
        maxes = []
        for rows, lanes in ATT_LIVE:
            part = s[rows, lanes] + bias_ref[variant, g, rows, lanes]
            s_ref[rows, lanes] = part
            maxes.append(jnp.max(part, axis=0, keepdims=True))
        return jnp.concatenate(maxes, axis=1)

    def probabilities(s_ref, p_ref, m):
        for rows, lanes in ATT_LIVE:
            p_ref[rows, lanes] = jnp.exp2(s_ref[rows, lanes] - m[:, lanes]).astype(BF16)

    ones_rows = jnp.ones((2 * SUBLANES, ATT_KB), BF16)

    def values(qb, g, p_ref):
        _, kstart = window(qb)
        vg = vt_ref[g * ATT_HEAD_DIM:(g + 1) * ATT_HEAD_DIM, pl.ds(kstart, ATT_KB)]
        pv = _dot(jnp.concatenate([vg, ones_rows], axis=0), p_ref[...])
        return pv[:ATT_HEAD_DIM] / pv[ATT_HEAD_DIM:ATT_HEAD_DIM + 1]

    s_bufs = (s0_ref, s1_ref)
    p_bufs = (p0_ref, p1_ref)
    m_first = scores(0, 0, s_bufs[0])
    m_second = scores(0, 1, s_bufs[1])
    probabilities(s_bufs[0], p_bufs[0], m_first)

    def body(qb, m_next):
        outs = []
        for g in range(ATT_HG):
            slot = g % 2
            outs.append(values(qb, g, p_bufs[slot]))
            probabilities(s_bufs[1 - slot], p_bufs[1 - slot], m_next)
            ahead = g + 2
            m_next = scores(qb + ahead // ATT_HG, ahead % ATT_HG, s_bufs[slot])
        q0, _ = window(qb)
        o_ref[pl.ds(q0, ATT_QB), :] = jnp.concatenate(outs, axis=0).T.astype(BF16)
        return m_next

    lax.fori_loop(0, n_blocks, body, m_second)


def _rel_vectors(rel_table):
    heads, n_rel = rel_table.shape
    far = rel_table[:, n_rel - 1:]
    near = rel_table[:, :1]
    n_far_front = ATT_QB + 1
    n_near = ATT_REL_WIDTH - LEFT_PAD - (CHUNK - 1) - n_far_front
    n_far_back = ATT_REL_WIDTH - n_far_front - n_near - n_rel
    g = jnp.concatenate([jnp.broadcast_to(far, (heads, n_far_front)),
                         jnp.broadcast_to(near, (heads, n_near)),
                         rel_table,
                         jnp.broadcast_to(far, (heads, n_far_back))], axis=1)
    return (g.astype(F32) * LOG2E).reshape(heads, 1, ATT_REL_WIDTH)


def _band_attention(qt, kp, vt, rel):
    ns = SEQ // ATT_QS
    return pl.pallas_call(
        _attn_kernel,
        grid=(BATCH, ATT_HEADS // ATT_HG, ns),
        in_specs=[pl.BlockSpec((ATT_HG, 1, ATT_REL_WIDTH), lambda b, g, n: (g, 0, 0)),
                  pl.BlockSpec((MXU_DIM, ATT_QS), lambda b, g, n: (g, b * ns + n)),
                  pl.BlockSpec((None, SEQ + LEFT_PAD, MXU_DIM), lambda b, g, n: (b, 0, g)),
                  pl.BlockSpec((None, MXU_DIM, SEQ + LEFT_PAD), lambda b, g, n: (b, g, 0))],
        out_specs=pl.BlockSpec((ATT_QS, MXU_DIM), lambda b, g, n: (b * ns + n, g)),
        out_shape=jax.ShapeDtypeStruct((TOKENS, D_MODEL), BF16),
        scratch_shapes=[pltpu.VMEM((ATT_PAD_BLOCKS + 1, ATT_HG, ATT_KB, ATT_QB), F32),
                        pltpu.VMEM((ATT_KB, ATT_QB), F32), pltpu.VMEM((ATT_KB, ATT_QB), F32),
                        pltpu.VMEM((ATT_KB, ATT_QB), BF16), pltpu.VMEM((ATT_KB, ATT_QB), BF16)],
        compiler_params=_params(("parallel", "parallel", "arbitrary")),
        name="band_attn",
    )(rel, qt, kp, vt)


def _row_copy(src, src_row, dst, dst_row, sem):
    return pltpu.make_async_copy(src.at[pl.ds(src_row, 1)], dst.at[pl.ds(dst_row, 1)], sem)


def _tile_wait(hbm, vmem_tile, sem):
    pltpu.make_async_copy(hbm.at[pl.ds(0, MOE_TM)], vmem_tile, sem).wait()


def _scatter_tile(xbuf, rs_smem, xs_hbm, sem):
    for t in range(MOE_TM):
        for which in range(TOP_K):
            _row_copy(xbuf, t, xs_hbm, rs_smem[which, t], sem).start(priority=which)


def _route_tile(rows, h_ref, o_ref, wo_ref, nf_ref, wrh_ref, wrl_ref, tri_ref,
                h_out_ref, route_ref, gw_ref, cnt_ref, carry_ref):
    cols = rows
    h = h_ref[rows, :] + _dot(o_ref[rows, :], wo_ref[...])
    h_out_ref[rows, :] = h
    xn = _rms(h, nf_ref[...])

    xh = xn.astype(BF16)
    xl = (xn - xh.astype(F32)).astype(BF16)
    logits = _dot(xh, wrh_ref[...]) + _dot(xl, wrh_ref[...]) + _dot(xh, wrl_ref[...])
    lt = logits.T[:N_EXPERTS]
    row = lax.broadcasted_iota(jnp.int32, lt.shape, 0)
    m1 = jnp.max(lt, axis=0, keepdims=True)
    i1 = jnp.min(jnp.where(lt == m1, row, N_EXPERTS), axis=0, keepdims=True)
    rest = jnp.where(row == i1, -jnp.inf, lt)
    m2 = jnp.max(rest, axis=0, keepdims=True)
    i2 = jnp.min(jnp.where(rest == m2, row, N_EXPERTS), axis=0, keepdims=True)
    e2 = jnp.exp(m2 - m1)
    w1 = 1.0 / (1.0 + e2)
    w2 = e2 / (1.0 + e2)

    member = ((row == i1) | (row == i2)).astype(F32)
    within = _dot(member.astype(BF16), tri_ref[...])
    carry = carry_ref[...]
    rank = within + carry[:, 0:1]
    r1 = jnp.sum(jnp.where(row == i1, rank, 0.0), axis=0, keepdims=True).astype(jnp.int32)
    r2 = jnp.sum(jnp.where(row == i2, rank, 0.0), axis=0, keepdims=True).astype(jnp.int32)
    carry_ref[...] = carry + jnp.sum(member, axis=1, keepdims=True)
    cnt_ref[...] = carry_ref[...].astype(jnp.int32)

    s1 = i1 * MOE_CAP + r1
    s2 = i2 * MOE_CAP + r2
    route_ref[:, cols] = jnp.where(row == 0, s1, jnp.where(row == 1, s2, 0))
    gwt = jnp.where(row == 0, w1, jnp.where(row == 1, w2, 0.0))
    pad = jnp.zeros((LANES - N_EXPERTS, gwt.shape[1]), F32)
    gw_ref[rows, :] = jnp.concatenate([gwt, pad], axis=0).T
    return xn


def _post_b_kernel(h_ref, o_ref, wo_ref, nf_ref, wrh_ref, wrl_ref, tri_ref,
                   h_out_ref, route_ref, gw_ref, cnt_ref, xs_hbm,
                   xbuf0, xbuf1, carry_ref, rs0, rs1, cnt_smem, zero_ref, sems, tail_sem):
    g = pl.program_id(0)
    tile_args = (h_ref, o_ref, wo_ref, nf_ref, wrh_ref, wrl_ref, tri_ref,
                 h_out_ref, route_ref, gw_ref, cnt_ref, carry_ref)
    first = slice(0, MOE_TM)
    second = slice(MOE_TM, 2 * MOE_TM)

    @pl.when(g == 0)
    def _():
        carry_ref[...] = jnp.zeros_like(carry_ref)
        xbuf1[...] = jnp.zeros_like(xbuf1)
        which = lax.broadcasted_iota(jnp.int32, (SUBLANES, MOE_TM), 0)
        token = lax.broadcasted_iota(jnp.int32, (SUBLANES, MOE_TM), 1)
        route_ref[:, first] = MOE_SPARE_ROW + jnp.minimum(which, TOP_K - 1) * MOE_TM + token
        pltpu.sync_copy(route_ref.at[:, first], rs1)

    _scatter_tile(xbuf1, rs1, xs_hbm, sems.at[1])
    xn = _route_tile(first, *tile_args)

    @pl.when(g >= 1)
    def _():
        for _ in range(TOP_K):
            _tile_wait(xs_hbm, xbuf0, sems.at[0])

    xbuf0[...] = xn
    pltpu.sync_copy(route_ref.at[:, first], rs0)

    _scatter_tile(xbuf0, rs0, xs_hbm, sems.at[0])
    xn = _route_tile(second, *tile_args)
    for _ in range(TOP_K):
        _tile_wait(xs_hbm, xbuf1, sems.at[1])
    xbuf1[...] = xn
    pltpu.sync_copy(route_ref.at[:, second], rs1)

    @pl.when(g == pl.num_programs(0) - 1)
    def _():
        _scatter_tile(xbuf1, rs1, xs_hbm, sems.at[1])
        for buf, sem in ((xbuf0, sems.at[0]), (xbuf1, sems.at[1])):
            for _ in range(TOP_K):
                _tile_wait(xs_hbm, buf, sem)

        pltpu.sync_copy(cnt_ref, cnt_smem)
        zero_ref[...] = jnp.zeros_like(zero_ref)
        for e in range(N_EXPERTS):
            count = cnt_smem[e, 0]
            aligned = ((count + (SUBLANES - 1)) // SUBLANES) * SUBLANES
            for part in range(MOE_SUPER // MOE_TM):
                start = pl.multiple_of(e * MOE_CAP + aligned + part * MOE_TM, SUBLANES)
                pltpu.make_async_copy(zero_ref, xs_hbm.at[pl.ds(start, MOE_TM)], tail_sem).start()
            for k in range(SUBLANES - 1):
                @pl.when(count + k < aligned)
                def _():
                    _row_copy(zero_ref, 0, xs_hbm, e * MOE_CAP + count + k, tail_sem).start()
        for e in range(N_EXPERTS):
            count = cnt_smem[e, 0]
            aligned = ((count + (SUBLANES - 1)) // SUBLANES) * SUBLANES
            for part in range(MOE_SUPER // MOE_TM):
                _tile_wait(xs_hbm, zero_ref, tail_sem)
            for k in range(SUBLANES - 1):
                @pl.when(count + k < aligned)
                def _():
                    _row_copy(zero_ref, 0, xs_hbm, 0, tail_sem).wait()


def _post_b(h, o, wo, nf, wr_hi, wr_lo):
    tm = MOE_TM
    pair = 2 * tm
    n = TOKENS // pair
    tile = lambda width: pl.BlockSpec((pair, width), lambda i: (i, 0))
    tri = jnp.asarray(np.triu(np.ones((tm, tm), np.float32), k=1), dtype=BF16)
    return pl.pallas_call(
        _post_b_kernel,
        grid=(n,),
        in_specs=[tile(D_MODEL), tile(D_MODEL), _const_spec(wo.shape), _const_spec((1, D_MODEL)),
                  _const_spec(wr_hi.shape), _const_spec(wr_lo.shape), _const_spec(tri.shape)],
        out_specs=[tile(D_MODEL),
                   pl.BlockSpec((SUBLANES, pair), lambda i: (0, i)),
                   tile(LANES),
                   pl.BlockSpec((SUBLANES, LANES), lambda i: (0, 0)),
                   pl.BlockSpec(memory_space=pl.ANY)],
        out_shape=[jax.ShapeDtypeStruct((TOKENS, D_MODEL), F32),
                   jax.ShapeDtypeStruct((SUBLANES, TOKENS), jnp.int32),
                   jax.ShapeDtypeStruct((TOKENS, LANES), F32),
                   jax.ShapeDtypeStruct((SUBLANES, LANES), jnp.int32),
                   jax.ShapeDtypeStruct((MOE_ROWS, D_MODEL), F32)],
        scratch_shapes=[pltpu.VMEM((tm, D_MODEL), F32),
                        pltpu.VMEM((tm, D_MODEL), F32),
                        pltpu.VMEM((SUBLANES, LANES), F32),
                        pltpu.SMEM((SUBLANES, tm), jnp.int32),
                        pltpu.SMEM((SUBLANES, tm), jnp.int32),
                        pltpu.SMEM((SUBLANES, LANES), jnp.int32),
                        pltpu.VMEM((tm, D_MODEL), F32),
                        pltpu.SemaphoreType.DMA((2,)),
                        pltpu.SemaphoreType.DMA(())],
        compiler_params=_params(("arbitrary",)),
        name="post_b",
    )(h, o, wo, nf, wr_hi, wr_lo, tri)


def _moe_kernel(blk_ref, exp_ref, halves_ref, nact_ref, x_ref, wg_ref, wu_ref, wd_ref, y_ref, xb_ref):
    j = pl.program_id(0)
    f = pl.program_id(1)
    first = slice(0, MOE_TM)
    second = slice(MOE_TM, MOE_SUPER)

    @pl.when(j < nact_ref[0])
    def _():
        def half(rows):
            @pl.when(f == 0)
            def _():
                xb_ref[rows, :] = x_ref[rows, :].astype(BF16)
                y_ref[rows, :] = jnp.zeros((MOE_TM, D_MODEL), F32)

            x = xb_ref[rows, :]
            a = (_silu(_dot(x, wg_ref[...])) * _dot(x, wu_ref[...])).astype(BF16)
            y_ref[rows, :] += _dot(a, wd_ref[...])

        half(first)

        @pl.when(halves_ref[j] == 2)
        def _():
            half(second)

        @pl.when((halves_ref[j] == 1) & (f == 0))
        def _():
            y_ref[second, :] = jnp.zeros((MOE_TM, D_MODEL), F32)


def _moe(xs, tile_block, tile_expert, tile_halves, n_active, wg, wu, wd, tf=D_FF_EXPERT // 2):
    nf = D_FF_EXPERT // tf
    fidx = lambda j, f, nact: jnp.where(j < nact[0], f, nf - 1)
    rows = pl.BlockSpec((MOE_SUPER, D_MODEL), lambda j, f, blk, exp, hv, nact: (blk[j], 0))
    grid_spec = pltpu.PrefetchScalarGridSpec(
        num_scalar_prefetch=4,
        grid=(MOE_MAX_TILES, nf),
        in_specs=[rows,
                  pl.BlockSpec((None, D_MODEL, tf), lambda j, f, blk, exp, hv, nact: (exp[j], 0, fidx(j, f, nact))),
                  pl.BlockSpec((None, D_MODEL, tf), lambda j, f, blk, exp, hv, nact: (exp[j], 0, fidx(j, f, nact))),
                  pl.BlockSpec((None, tf, D_MODEL), lambda j, f, blk, exp, hv, nact: (exp[j], fidx(j, f, nact), 0))],
        out_specs=rows,
        scratch_shapes=[pltpu.VMEM((MOE_SUPER, D_MODEL), BF16)],
    )
    return pl.pallas_call(
        _moe_kernel,
        grid_spec=grid_spec,
        out_shape=jax.ShapeDtypeStruct((MOE_ROWS, D_MODEL), F32),
        compiler_params=_params(("arbitrary", "arbitrary")),
        name="moe",
    )(tile_block, tile_expert, tile_halves, n_active, xs, wg, wu, wd)


def _tile_plan(counts):
    experts = jnp.arange(N_EXPERTS)[None, :]
    halves = (counts + (MOE_TM - 1)) // MOE_TM
    tiles = (halves + 1) // 2
    ends = jnp.cumsum(tiles)
    n_active = ends[-1]
    j = jnp.minimum(jnp.arange(MOE_MAX_TILES, dtype=jnp.int32), n_active - 1)
    expert = jnp.sum((j[:, None] >= ends[None, :]).astype(jnp.int32), axis=1)
    pick = lambda per_expert: jnp.sum(jnp.where(expert[:, None] == experts, per_expert[None, :], 0), axis=1)
    k = j - pick(ends - tiles)
    block = expert * MOE_REGION_TILES + k
    tile_halves = jnp.minimum(pick(halves) - 2 * k, 2)
    i32 = lambda a: a.astype(jnp.int32)
    return i32(block), i32(expert), i32(tile_halves), i32(n_active.reshape(1))


def _gather_tile(ys_hbm, rs_smem, ybuf, sem):
    for t in range(MOE_TM):
        for which in range(TOP_K):
            _row_copy(ys_hbm, rs_smem[which, t], ybuf.at[which], t, sem).start(priority=which)


def _final_kernel(h_ref, p_ref, route_ref, route_next_ref, gw_ref, np_ref, wpu_ref, wpg_ref, nfin_ref,
                  ys_hbm, out_ref, ybuf0, ybuf1, rs_smem, sems):
    first = slice(0, MOE_TM)
    second = slice(MOE_TM, 2 * MOE_TM)

    def combine(rows, ybuf):
        gw = gw_ref[rows, :]
        y = gw[:, 0:1] * ybuf[0] + gw[:, 1:2] * ybuf[1]
        hp = h_ref[rows, :] + y
        gate = jax.nn.sigmoid(_dot(_rms(hp, np_ref[...]).astype(BF16), wpg_ref[...]))
        hp = hp + _dot(p_ref[rows, :].astype(BF16), wpu_ref[...]) * gate
        out_ref[rows, :] = _rms(hp, nfin_ref[...])

    @pl.when(pl.program_id(0) == 0)
    def _():
        pltpu.sync_copy(route_ref.at[:, first], rs_smem)
        _gather_tile(ys_hbm, rs_smem, ybuf0, sems.at[0])

    for which in range(TOP_K):
        _tile_wait(ys_hbm, ybuf0.at[which], sems.at[0])
    pltpu.sync_copy(route_ref.at[:, second], rs_smem)
    _gather_tile(ys_hbm, rs_smem, ybuf1, sems.at[1])
    combine(first, ybuf0)

    for which in range(TOP_K):
        _tile_wait(ys_hbm, ybuf1.at[which], sems.at[1])
    pltpu.sync_copy(route_next_ref, rs_smem)
    _gather_tile(ys_hbm, rs_smem, ybuf0, sems.at[0])
    combine(second, ybuf1)

    @pl.when(pl.program_id(0) == pl.num_programs(0) - 1)
    def _():
        for which in range(TOP_K):
            _tile_wait(ys_hbm, ybuf0.at[which], sems.at[0])


def _final(h, p, route, gw, npl, wpu, wpg, nfin, ys):
    tm = MOE_TM
    pair = 2 * tm
    n = TOKENS // pair
    tile = lambda width: pl.BlockSpec((pair, width), lambda i: (i, 0))
    consts = [npl, wpu, wpg, nfin]
    next_tile = lambda i: (0, jnp.minimum(2 * i + 2, TOKENS // tm - 1))
    return pl.pallas_call(
        _final_kernel,
        grid=(n,),
        in_specs=[tile(D_MODEL), pl.BlockSpec((None, pair, PLE_DIM), lambda i: (1, i, 0)),
                  pl.BlockSpec((SUBLANES, pair), lambda i: (0, i)),
                  pl.BlockSpec((SUBLANES, tm), next_tile),
                  tile(LANES)]
                 + [_const_spec(c.shape) for c in consts] + [pl.BlockSpec(memory_space=pl.ANY)],
        out_specs=tile(D_MODEL),
        out_shape=jax.ShapeDtypeStruct((TOKENS, D_MODEL), F32),
        scratch_shapes=[pltpu.VMEM((TOP_K, tm, D_MODEL), F32),
                        pltpu.VMEM((TOP_K, tm, D_MODEL), F32),
                        pltpu.SMEM((SUBLANES, tm), jnp.int32),
                        pltpu.SemaphoreType.DMA((2,))],
        compiler_params=_params(("arbitrary",)),
        name="final",
    )(h, p, route, route, gw, *consts, ys)


def kernel(x, p, positions, norm_mix, norm_ffn, norm_ple, w_in_a, ret_gn, w_out_a, norm_kv, w_kv, w_q_b, rel_bias, w_out_b, w_gate_dense, w_up_dense, w_down_dense, w_router, w_gate_moe, w_up_moe, w_down_moe, w_ple_up, w_ple_gate, norm_final):
    bf = lambda w: w.astype(BF16)
    row = lambda g: g.reshape(1, -1).astype(F32)
    h0 = x.reshape(TOKENS, D_MODEL)
    p2 = p.reshape(2, TOKENS, PLE_DIM)
    pos = positions.reshape(TOKENS, 1)
    inv_freq = (1.0 / (ROPE_BASE ** jnp.linspace(0.0, 1.0, ROPE_HALF, dtype=F32))).reshape(1, ROPE_HALF)

    og = _retention_layer(h0, pos, row(norm_mix[0]), inv_freq, bf(w_in_a[0]), row(ret_gn[0]))
    h1 = _post_a(h0, og, p2, bf(w_out_a[0]), row(norm_ffn[0]), bf(w_gate_dense[0]), bf(w_up_dense[0]),
                 bf(w_down_dense[0]), row(norm_ple[0]), bf(w_ple_up[0]), bf(w_ple_gate[0]))

    qt, kp, vt = _qkv(h1, row(norm_mix[1]), row(norm_kv), bf(w_q_b[0]), bf(w_kv))
    ob = _band_attention(qt, kp, vt, _rel_vectors(rel_bias[0]))
    w_router_pad = jnp.zeros((D_MODEL, LANES), F32).at[:, :N_EXPERTS].set(w_router[0])
    wr_hi = bf(w_router_pad)
    wr_lo = bf(w_router_pad - wr_hi.astype(F32))
    h2, route, gw, counts, xs = _post_b(h1, ob, bf(w_out_b[0]), row(norm_ffn[1]), wr_hi, wr_lo)
    tile_block, tile_expert, tile_halves, n_active = _tile_plan(counts[:, 0])
    ys = _moe(xs, tile_block, tile_expert, tile_halves, n_active,
              bf(w_gate_moe[0]), bf(w_up_moe[0]), bf(w_down_moe[0]))
    out = _final(h2, p2, route, gw, row(norm_ple[1]), bf(w_ple_up[1]), bf(w_ple_gate[1]), row(norm_final), ys)
    return out.reshape(BATCH, SEQ, D_MODEL)
```

```python
import functools

import numpy as np
import jax
import jax.numpy as jnp
from jax import lax
from jax.experimental import pallas as pl
from jax.experimental.pallas import tpu as pltpu

F32 = jnp.float32
BF16 = jnp.bfloat16

D_MODEL = 1024
BATCH = 2
SEQ = 8192
TOKENS = BATCH * SEQ
CHUNK = 64
PLE_DIM = 256

RET_HEADS = 4
RET_QK_DIM = 256
RET_V_DIM = 512
RET_QK_WIDTH = RET_HEADS * RET_QK_DIM
RET_V_WIDTH = RET_HEADS * RET_V_DIM
ROPE_BASE = 10000.0
ROPE_HALF = RET_QK_DIM // 2

ATT_HEADS = 16
ATT_HEAD_DIM = 64
LEFT_CHUNKS = 8
LEFT_PAD = LEFT_CHUNKS * CHUNK
REL_CLIP = 256

D_FF_DENSE = 2816
N_EXPERTS = 8
D_FF_EXPERT = 3584
EPS = 1e-6

LANES = 128
MXU_DIM = 256
VMEM_LIMIT = 56 * 1024 * 1024

RET_SUB = MXU_DIM
RET_BLOCK = 512
ATT_QB = MXU_DIM
ATT_KB = ATT_QB + LEFT_PAD
ATT_QS = 4096
ATT_REL_WIDTH = 1024
ATT_PAD_BLOCKS = LEFT_PAD // ATT_QB
ATT_LIVE = tuple((slice(LANES * t, LANES * t + LEFT_PAD + LANES), slice(LANES * t, LANES * (t + 1)))
                 for t in range(ATT_QB // LANES))
LOG2E = 1.4426950408889634
ATT_HG = MXU_DIM // ATT_HEAD_DIM
NEG_BIG = -1e30

TOP_K = 2
MOE_TM = 512
MOE_SUPER = 2 * MOE_TM
MOE_CAP = TOKENS + 2 * MOE_SUPER
MOE_REGION_TILES = MOE_CAP // MOE_SUPER
MOE_MAX_TILES = TOP_K * TOKENS // MOE_SUPER + N_EXPERTS
MOE_GRID_TILES = MOE_MAX_TILES + 2
MOE_SPARE_ROW = N_EXPERTS * MOE_CAP
MOE_ROWS = MOE_SPARE_ROW + TOP_K * MOE_TM
MOE_ROW_WIDTH = D_MODEL + LANES
MOE_DUMP_ROW = TOP_K * TOKENS
MOE_N_DUMP = MOE_SUPER + MOE_TM
MOE_OUT_ROWS = MOE_DUMP_ROW + MOE_N_DUMP
SUBLANES = 8


def _dot(a, b):
    return jnp.dot(a, b, preferred_element_type=F32)


def _rms(x, g):
    return x * lax.rsqrt(jnp.mean(x * x, axis=-1, keepdims=True) + EPS) * g


def _silu(x):
    return x * jax.nn.sigmoid(x)


def _const_spec(shape):
    nd = len(shape)
    return pl.BlockSpec(shape, lambda *_: (0,) * nd, pipeline_mode=pl.Buffered(1))


def _params(sem):
    return pltpu.CompilerParams(dimension_semantics=sem, vmem_limit_bytes=VMEM_LIMIT)


def _inproj_tile(x_ref, pos_ref, g_ref, invf_ref, w_ref, q_ref, k_ref, v_ref, gate_ref):
    xn = _rms(x_ref[...], g_ref[...]).astype(BF16)
    v0 = 2 * RET_QK_WIDTH
    g0 = v0 + RET_V_WIDTH
    for c in range(RET_HEADS):
        lo, hi = c * RET_V_DIM, (c + 1) * RET_V_DIM
        v_ref[:, lo:hi] = _dot(xn, w_ref[:, v0 + lo:v0 + hi]).astype(BF16)
        gate_ref[:, lo:hi] = _silu(_dot(xn, w_ref[:, g0 + lo:g0 + hi])).astype(BF16)
    ang = pos_ref[...].astype(F32) * invf_ref[...]
    cos = jnp.cos(ang)
    sin = jnp.sin(ang)
    k_scale = RET_QK_DIM ** -0.5
    for h in range(RET_HEADS):
        lo = h * RET_QK_DIM
        mid = lo + ROPE_HALF
        hi = lo + RET_QK_DIM
        pq = _dot(xn, w_ref[:, lo:hi])
        x1, x2 = pq[:, :ROPE_HALF], pq[:, ROPE_HALF:]
        q_ref[:, lo:mid] = (x1 * cos - x2 * sin).astype(BF16)
        q_ref[:, mid:hi] = (x1 * sin + x2 * cos).astype(BF16)
        pk = _dot(xn, w_ref[:, RET_QK_WIDTH + lo:RET_QK_WIDTH + hi])
        x1, x2 = pk[:, :ROPE_HALF], pk[:, ROPE_HALF:]
        k_ref[:, lo:mid] = ((x1 * cos - x2 * sin) * k_scale).astype(BF16)
        k_ref[:, mid:hi] = ((x1 * sin + x2 * cos) * k_scale).astype(BF16)


def _retention_kernel(gc_ref, x_ref, pos_ref, g_ref, invf_ref, w_ref, dm_ref, xi_ref, zeta_ref, gn_ref,
                      o_ref, q_ref, k_ref, v_ref, gate_ref, state_ref):
    @pl.when(pl.program_id(1) == 0)
    def _():
        state_ref[...] = jnp.zeros_like(state_ref)

    _inproj_tile(x_ref, pos_ref, g_ref, invf_ref, w_ref, q_ref, k_ref, v_ref, gate_ref)

    for h in range(RET_HEADS):
        qk = slice(h * RET_QK_DIM, (h + 1) * RET_QK_DIM)
        vd = slice(h * RET_V_DIM, (h + 1) * RET_V_DIM)
        g_chunk = gc_ref[h]
        dm = dm_ref[h]
        xi = xi_ref[h]
        zeta = zeta_ref[h]
        gn = gn_ref[:, vd]
        for j in range(RET_BLOCK // RET_SUB):
            sl = slice(j * RET_SUB, (j + 1) * RET_SUB)
            qj = q_ref[sl, qk]
            kj = k_ref[sl, qk]
            vj = v_ref[sl, vd]
            st = state_ref[h]
            s = lax.dot_general(qj, kj, (((1,), (1,)), ((), ())), preferred_element_type=F32) * dm
            inner = _dot(s.astype(BF16), vj)
            cross = _dot(qj, st.astype(BF16)) * xi
            kz = (kj.astype(F32) * zeta).astype(BF16)
            upd = lax.dot_general(kz, vj, (((0,), (0,)), ((), ())), preferred_element_type=F32)
            state_ref[h] = st * g_chunk + upd
            o = inner + cross
            on = _rms(o, gn)
            o_ref[sl, vd] = (on * gate_ref[sl, vd].astype(F32)).astype(BF16)


def _retention_layer(x, pos, g, invf, w, gn):
    lg = jnp.log1p(-jnp.exp2(-5.0 - jnp.arange(RET_HEADS, dtype=F32)))
    idx = jnp.arange(RET_SUB, dtype=F32)
    diff = idx[:, None] - idx[None, :]
    causal = diff >= 0
    dmask = jnp.where(causal, jnp.exp(jnp.where(causal, diff, 0.0)[None] * lg[:, None, None]), 0.0)
    xi = jnp.exp((idx[None, :] + 1.0) * lg[:, None])[:, :, None]
    zeta = jnp.exp((RET_SUB - 1.0 - idx)[None, :] * lg[:, None])[:, :, None]
    g_chunk = jnp.exp(RET_SUB * lg)
    nb = SEQ // RET_BLOCK
    tile = lambda width: pl.BlockSpec((RET_BLOCK, width), lambda b, n: (b * nb + n, 0))
    scratch = lambda width: pltpu.VMEM((RET_BLOCK, width), BF16)
    return pl.pallas_call(
        _retention_kernel,
        grid=(BATCH, nb),
        in_specs=[pl.BlockSpec(memory_space=pltpu.SMEM),
                  tile(D_MODEL), tile(1), _const_spec((1, D_MODEL)), _const_spec((1, ROPE_HALF)),
                  _const_spec(w.shape), _const_spec(dmask.shape), _const_spec(xi.shape),
                  _const_spec(zeta.shape), _const_spec(gn.shape)],
        out_specs=tile(RET_V_WIDTH),
        out_shape=jax.ShapeDtypeStruct((TOKENS, RET_V_WIDTH), BF16),
        scratch_shapes=[scratch(RET_QK_WIDTH), scratch(RET_QK_WIDTH), scratch(RET_V_WIDTH),
                        scratch(RET_V_WIDTH), pltpu.VMEM((RET_HEADS, RET_QK_DIM, RET_V_DIM), F32)],
        compiler_params=_params(("parallel", "arbitrary")),
        name="retention",
    )(g_chunk, x, pos, g, invf, w, dmask, xi, zeta, gn)


def _ple(h, p_ref, np_ref, wpu_ref, wpg_ref):
    gate = jax.nn.sigmoid(_dot(_rms(h, np_ref[...]).astype(BF16), wpg_ref[...]))
    return h + _dot(p_ref[...].astype(BF16), wpu_ref[...]) * gate


def _post_a_kernel(h_ref, og_ref, p_ref, wo_ref, nf_ref, wg_ref, wu_ref, wd_ref, np_ref, wpu_ref,
                   wpg_ref, out_ref):
    h1 = h_ref[...] + _dot(og_ref[...], wo_ref[...])
    xn = _rms(h1, nf_ref[...]).astype(BF16)
    a = (_silu(_dot(xn, wg_ref[...])) * _dot(xn, wu_ref[...])).astype(BF16)
    h2 = h1 + _dot(a, wd_ref[...])
    out_ref[...] = _ple(h2, p_ref, np_ref, wpu_ref, wpg_ref)


def _post_a(h, og, p, wo, nf, wg, wu, wd, npl, wpu, wpg, tm=512):
    n = TOKENS // tm
    tile = lambda width: pl.BlockSpec((tm, width), lambda i: (i, 0))
    consts = [wo, nf, wg, wu, wd, npl, wpu, wpg]
    return pl.pallas_call(
        _post_a_kernel,
        grid=(n,),
        in_specs=[tile(D_MODEL), tile(RET_V_WIDTH), pl.BlockSpec((None, tm, PLE_DIM), lambda i: (0, i, 0))]
                 + [_const_spec(c.shape) for c in consts],
        out_specs=tile(D_MODEL),
        out_shape=jax.ShapeDtypeStruct((TOKENS, D_MODEL), F32),
        compiler_params=_params(("parallel",)),
        name="post_a",
    )(h, og, p, *consts)


def _qkv_kernel(h_ref, gq_ref, gkv_ref, wq_ref, wkv_ref, qt_ref, k_ref, vt_ref):
    h = h_ref[...]
    hn = h * lax.rsqrt(jnp.mean(h * h, axis=-1, keepdims=True) + EPS)
    q = _dot((hn * gq_ref[...]).astype(BF16), wq_ref[...]) * (ATT_HEAD_DIM ** -0.5 * LOG2E)
    qt_ref[...] = q.T.astype(BF16)
    kv = _dot((hn * gkv_ref[...]).astype(BF16), wkv_ref[...])
    keep = (pl.program_id(1) > 0).astype(F32)
    k_ref[...] = (kv[:, :D_MODEL] * keep).astype(BF16)
    vt_ref[...] = (kv[:, D_MODEL:] * keep).T.astype(BF16)


def _qkv(h, gq, gkv, wq, wkv):
    tm = LEFT_PAD
    nb = SEQ // tm
    src = lambda b, j: (b * nb + jnp.maximum(j - 1, 0), 0)
    return pl.pallas_call(
        _qkv_kernel,
        grid=(BATCH, nb + 1),
        in_specs=[pl.BlockSpec((tm, D_MODEL), src), _const_spec((1, D_MODEL)), _const_spec((1, D_MODEL)),
                  _const_spec(wq.shape), _const_spec(wkv.shape)],
        out_specs=[pl.BlockSpec((D_MODEL, tm), lambda b, j: (0, b * nb + jnp.maximum(j - 1, 0))),
                   pl.BlockSpec((None, tm, D_MODEL), lambda b, j: (b, j, 0)),
                   pl.BlockSpec((None, D_MODEL, tm), lambda b, j: (b, 0, j))],
        out_shape=[jax.ShapeDtypeStruct((D_MODEL, TOKENS), BF16),
                   jax.ShapeDtypeStruct((BATCH, SEQ + LEFT_PAD, D_MODEL), BF16),
                   jax.ShapeDtypeStruct((BATCH, D_MODEL, SEQ + LEFT_PAD), BF16)],
        compiler_params=_params(("parallel", "arbitrary")),
        name="qkv_b",
    )(h, gq, gkv, wq, wkv)


def _attn_kernel(rel_ref, qt_ref, k_ref, vt_ref, o_ref, bias_ref, s0_ref, s1_ref, p0_ref, p1_ref):
    n = pl.program_id(2)

    @pl.when(n == 0)
    def _():
        kj = lax.broadcasted_iota(jnp.int32, (ATT_KB, ATT_QB), 0)
        qi = lax.broadcasted_iota(jnp.int32, (ATT_KB, ATT_QB), 1)
        k_chunk = kj // CHUNK - LEFT_CHUNKS
        q_chunk = qi // CHUNK
        allowed = (k_chunk <= q_chunk) & (k_chunk >= q_chunk - LEFT_CHUNKS)
        for g in range(ATT_HG):
            base = jnp.broadcast_to(rel_ref[g], (ATT_KB, ATT_REL_WIDTH))
            rolled = pltpu.roll(base, 0, 1, stride=1, stride_axis=0)
            table = jnp.where(allowed, rolled[:, :ATT_QB], NEG_BIG)
            bias_ref[0, g] = table
            for v in range(1, ATT_PAD_BLOCKS + 1):
                bias_ref[v, g] = jnp.where(kj >= LEFT_PAD - (v - 1) * ATT_QB, table, NEG_BIG)
        p0_ref[...] = jnp.zeros_like(p0_ref)
        p1_ref[...] = jnp.zeros_like(p1_ref)

    head_of_row = lax.broadcasted_iota(jnp.int32, (MXU_DIM, ATT_QB), 0) // ATT_HEAD_DIM
    n_blocks = ATT_QS // ATT_QB

    def window(qb):
        qb = jnp.minimum(qb, n_blocks - 1)
        q0 = pl.multiple_of(qb * ATT_QB, ATT_QB)
        kstart = pl.multiple_of(n * ATT_QS + qb * ATT_QB, ATT_QB)
        return q0, kstart

    def scores(qb, g, s_ref):
        q0, kstart = window(qb)
        qt = qt_ref[:, pl.ds(q0, ATT_QB)]
        kb = k_ref[pl.ds(kstart, ATT_KB), :]
        qg = jnp.where(head_of_row == g, qt, jnp.zeros_like(qt))
        block_in_seq = kstart // ATT_QB
        variant = jnp.where(block_in_seq < ATT_PAD_BLOCKS, block_in_seq + 1, 0)
        s = _dot(kb, qg)
        maxes = []
        for rows, lanes in ATT_LIVE:
            part = s[rows, lanes] + bias_ref[variant, g, rows, lanes]
            s_ref[rows, lanes] = part
            maxes.append(jnp.max(part, axis=0, keepdims=True))
        return jnp.concatenate(maxes, axis=1)

    def probabilities(s_ref, p_ref, m):
        for rows, lanes in ATT_LIVE:
            p_ref[rows, lanes] = jnp.exp2(s_ref[rows, lanes] - m[:, lanes]).astype(BF16)

    ones_rows = jnp.ones((2 * SUBLANES, ATT_KB), BF16)

    def values(qb, g, p_ref):
        _, kstart = window(qb)
        vg = vt_ref[g * ATT_HEAD_DIM:(g + 1) * ATT_HEAD_DIM, pl.ds(kstart, ATT_KB)]
        pv = _dot(jnp.concatenate([vg, ones_rows], axis=0), p_ref[...])
        return pv[:ATT_HEAD_DIM] / pv[ATT_HEAD_DIM:ATT_HEAD_DIM + 1]

    s_bufs = (s0_ref, s1_ref)
    p_bufs = (p0_ref, p1_ref)
    m_first = scores(0, 0, s_bufs[0])
    m_second = scores(0, 1, s_bufs[1])
    probabilities(s_bufs[0], p_bufs[0], m_first)

    def body(qb, m_next):
        outs = []
        for g in range(ATT_HG):
            slot = g % 2
            outs.append(values(qb, g, p_bufs[slot]))
            probabilities(s_bufs[1 - slot], p_bufs[1 - slot], m_next)
            ahead = g + 2
            m_next = scores(qb + ahead // ATT_HG, ahead % ATT_HG, s_bufs[slot])
        q0, _ = window(qb)
        o_ref[pl.ds(q0, ATT_QB), :] = jnp.concatenate(outs, axis=0).T.astype(BF16)
        return m_next

    lax.fori_loop(0, n_blocks, body, m_second)


def _rel_vectors(rel_table):
    heads, n_rel = rel_table.shape
    far = rel_table[:, n_rel - 1:]
    near = rel_table[:, :1]
    n_far_front = ATT_QB + 1
    n_near = ATT_REL_WIDTH - LEFT_PAD - (CHUNK - 1) - n_far_front
    n_far_back = ATT_REL_WIDTH - n_far_front - n_near - n_rel
    g = jnp.concatenate([jnp.broadcast_to(far, (heads, n_far_front)),
                         jnp.broadcast_to(near, (heads, n_near)),
                         rel_table,
                         jnp.broadcast_to(far, (heads, n_far_back))], axis=1)
    return (g.astype(F32) * LOG2E).reshape(heads, 1, ATT_REL_WIDTH)


def _band_attention(qt, kp, vt, rel):
    ns = SEQ // ATT_QS
    return pl.pallas_call(
        _attn_kernel,
        grid=(BATCH, ATT_HEADS // ATT_HG, ns),
        in_specs=[pl.BlockSpec((ATT_HG, 1, ATT_REL_WIDTH), lambda b, g, n: (g, 0, 0)),
                  pl.BlockSpec((MXU_DIM, ATT_QS), lambda b, g, n: (g, b * ns + n)),
                  pl.BlockSpec((None, SEQ + LEFT_PAD, MXU_DIM), lambda b, g, n: (b, 0, g)),
                  pl.BlockSpec((None, MXU_DIM, SEQ + LEFT_PAD), lambda b, g, n: (b, g, 0))],
        out_specs=pl.BlockSpec((ATT_QS, MXU_DIM), lambda b, g, n: (b * ns + n, g)),
        out_shape=jax.ShapeDtypeStruct((TOKENS, D_MODEL), BF16),
        scratch_shapes=[pltpu.VMEM((ATT_PAD_BLOCKS + 1, ATT_HG, ATT_KB, ATT_QB), F32),
                        pltpu.VMEM((ATT_KB, ATT_QB), F32), pltpu.VMEM((ATT_KB, ATT_QB), F32),
                        pltpu.VMEM((ATT_KB, ATT_QB), BF16), pltpu.VMEM((ATT_KB, ATT_QB), BF16)],
        compiler_params=_params(("parallel", "parallel", "arbitrary")),
        name="band_attn",
    )(rel, qt, kp, vt)


def _row_copy(src, src_row, dst, dst_row, sem):
    return pltpu.make_async_copy(src.at[pl.ds(src_row, 1)], dst.at[pl.ds(dst_row, 1)], sem)


def _tile_wait(hbm, vmem_tile, sem):
    pltpu.make_async_copy(hbm.at[pl.ds(0, MOE_TM)], vmem_tile, sem).wait()


def _scatter_tile(xbuf, rs_smem, xs_hbm, sem):
    for t in range(MOE_TM):
        for which in range(TOP_K):
            _row_copy(xbuf.at[which], t, xs_hbm, rs_smem[which, t], sem).start(priority=which)


def _fill_dispatch_rows(xbuf, xn, first_token):
    token = first_token + lax.broadcasted_iota(jnp.int32, (MOE_TM, LANES), 0)
    for which in range(TOP_K):
        xbuf[which, :, :D_MODEL] = xn
        xbuf[which, :, D_MODEL:] = (which * TOKENS + token).astype(F32)


def _route_tile(rows, h_ref, o_ref, wo_ref, nf_ref, wrh_ref, wrl_ref, tri_ref,
                h_out_ref, route_ref, gw_ref, cnt_ref, carry_ref):
    cols = rows
    h = h_ref[rows, :] + _dot(o_ref[rows, :], wo_ref[...])
    h_out_ref[rows, :] = h
    xn = _rms(h, nf_ref[...])

    xh = xn.astype(BF16)
    xl = (xn - xh.astype(F32)).astype(BF16)
    logits = _dot(xh, wrh_ref[...]) + _dot(xl, wrh_ref[...]) + _dot(xh, wrl_ref[...])
    lt = logits.T[:N_EXPERTS]
    row = lax.broadcasted_iota(jnp.int32, lt.shape, 0)
    m1 = jnp.max(lt, axis=0, keepdims=True)
    i1 = jnp.min(jnp.where(lt == m1, row, N_EXPERTS), axis=0, keepdims=True)
    rest = jnp.where(row == i1, -jnp.inf, lt)
    m2 = jnp.max(rest, axis=0, keepdims=True)
    i2 = jnp.min(jnp.where(rest == m2, row, N_EXPERTS), axis=0, keepdims=True)
    e2 = jnp.exp(m2 - m1)
    w1 = 1.0 / (1.0 + e2)
    w2 = e2 / (1.0 + e2)

    member = ((row == i1) | (row == i2)).astype(F32)
    within = _dot(member.astype(BF16), tri_ref[...])
    carry = carry_ref[...]
    rank = within + carry[:, 0:1]
    r1 = jnp.sum(jnp.where(row == i1, rank, 0.0), axis=0, keepdims=True).astype(jnp.int32)
    r2 = jnp.sum(jnp.where(row == i2, rank, 0.0), axis=0, keepdims=True).astype(jnp.int32)
    carry_ref[...] = carry + jnp.sum(member, axis=1, keepdims=True)
    cnt_ref[...] = carry_ref[...].astype(jnp.int32)

    s1 = i1 * MOE_CAP + r1
    s2 = i2 * MOE_CAP + r2
    route_ref[:, cols] = jnp.where(row == 0, s1, jnp.where(row == 1, s2, 0))
    gwt = jnp.where(row == 0, w1, jnp.where(row == 1, w2, 0.0))
    pad = jnp.zeros((LANES - N_EXPERTS, gwt.shape[1]), F32)
    gw_ref[rows, :] = jnp.concatenate([gwt, pad], axis=0).T
    return xn


def _post_b_kernel(h_ref, o_ref, wo_ref, nf_ref, wrh_ref, wrl_ref, tri_ref,
                   h_out_ref, route_ref, gw_ref, cnt_ref, xs_hbm,
                   xbuf0, xbuf1, carry_ref, rs0, rs1, cnt_smem, zero_ref, sems, tail_sem):
    g = pl.program_id(0)
    tile_args = (h_ref, o_ref, wo_ref, nf_ref, wrh_ref, wrl_ref, tri_ref,
                 h_out_ref, route_ref, gw_ref, cnt_ref, carry_ref)
    first = slice(0, MOE_TM)
    second = slice(MOE_TM, 2 * MOE_TM)

    @pl.when(g == 0)
    def _():
        carry_ref[...] = jnp.zeros_like(carry_ref)
        xbuf1[...] = jnp.zeros_like(xbuf1)
        which = lax.broadcasted_iota(jnp.int32, (SUBLANES, MOE_TM), 0)
        token = lax.broadcasted_iota(jnp.int32, (SUBLANES, MOE_TM), 1)
        route_ref[:, first] = MOE_SPARE_ROW + jnp.minimum(which, TOP_K - 1) * MOE_TM + token
        pltpu.sync_copy(route_ref.at[:, first], rs1)

    _scatter_tile(xbuf1, rs1, xs_hbm, sems.at[1])
    xn = _route_tile(first, *tile_args)

    @pl.when(g >= 1)
    def _():
        for which in range(TOP_K):
            _tile_wait(xs_hbm, xbuf0.at[which], sems.at[0])

    _fill_dispatch_rows(xbuf0, xn, 2 * g * MOE_TM)
    pltpu.sync_copy(route_ref.at[:, first], rs0)

    _scatter_tile(xbuf0, rs0, xs_hbm, sems.at[0])
    xn = _route_tile(second, *tile_args)
    for which in range(TOP_K):
        _tile_wait(xs_hbm, xbuf1.at[which], sems.at[1])
    _fill_dispatch_rows(xbuf1, xn, (2 * g + 1) * MOE_TM)
    pltpu.sync_copy(route_ref.at[:, second], rs1)

    @pl.when(g == pl.num_programs(0) - 1)
    def _():
        _scatter_tile(xbuf1, rs1, xs_hbm, sems.at[1])
        for buf, sem in ((xbuf0, sems.at[0]), (xbuf1, sems.at[1])):
            for which in range(TOP_K):
                _tile_wait(xs_hbm, buf.at[which], sem)

        pltpu.sync_copy(cnt_ref, cnt_smem)
        n_parts = MOE_SUPER // MOE_TM
        spare = lax.broadcasted_iota(jnp.int32, (MOE_TM, LANES), 0) + MOE_DUMP_ROW
        for part in range(n_parts + 1):
            zero_ref[part, :, :D_MODEL] = jnp.zeros((MOE_TM, D_MODEL), F32)
            zero_ref[part, :, D_MODEL:] = (spare + part * MOE_TM).astype(F32)
        for e in range(N_EXPERTS):
            count = cnt_smem[e, 0]
            aligned = ((count + (SUBLANES - 1)) // SUBLANES) * SUBLANES
            for part in range(n_parts):
                start = pl.multiple_of(e * MOE_CAP + aligned + part * MOE_TM, SUBLANES)
                pltpu.make_async_copy(zero_ref.at[part], xs_hbm.at[pl.ds(start, MOE_TM)], tail_sem).start()
            for k in range(SUBLANES - 1):
                @pl.when(count + k < aligned)
                def _():
                    _row_copy(zero_ref.at[n_parts], k, xs_hbm, e * MOE_CAP + count + k, tail_sem).start()
        for e in range(N_EXPERTS):
            count = cnt_smem[e, 0]
            aligned = ((count + (SUBLANES - 1)) // SUBLANES) * SUBLANES
            for part in range(n_parts):
                _tile_wait(xs_hbm, zero_ref.at[part], tail_sem)
            for k in range(SUBLANES - 1):
                @pl.when(count + k < aligned)
                def _():
                    _row_copy(zero_ref.at[n_parts], k, xs_hbm, 0, tail_sem).wait()


def _post_b(h, o, wo, nf, wr_hi, wr_lo):
    tm = MOE_TM
    pair = 2 * tm
    n = TOKENS // pair
    tile = lambda width: pl.BlockSpec((pair, width), lambda i: (i, 0))
    tri = jnp.asarray(np.triu(np.ones((tm, tm), np.float32), k=1), dtype=BF16)
    return pl.pallas_call(
        _post_b_kernel,
        grid=(n,),
        in_specs=[tile(D_MODEL), tile(D_MODEL), _const_spec(wo.shape), _const_spec((1, D_MODEL)),
                  _const_spec(wr_hi.shape), _const_spec(wr_lo.shape), _const_spec(tri.shape)],
        out_specs=[tile(D_MODEL),
                   pl.BlockSpec((SUBLANES, pair), lambda i: (0, i)),
                   tile(LANES),
                   pl.BlockSpec((SUBLANES, LANES), lambda i: (0, 0)),
                   pl.BlockSpec(memory_space=pl.ANY)],
        out_shape=[jax.ShapeDtypeStruct((TOKENS, D_MODEL), F32),
                   jax.ShapeDtypeStruct((SUBLANES, TOKENS), jnp.int32),
                   jax.ShapeDtypeStruct((TOKENS, LANES), F32),
                   jax.ShapeDtypeStruct((SUBLANES, LANES), jnp.int32),
                   jax.ShapeDtypeStruct((MOE_ROWS, MOE_ROW_WIDTH), F32)],
        scratch_shapes=[pltpu.VMEM((TOP_K, tm, MOE_ROW_WIDTH), F32),
                        pltpu.VMEM((TOP_K, tm, MOE_ROW_WIDTH), F32),
                        pltpu.VMEM((SUBLANES, LANES), F32),
                        pltpu.SMEM((SUBLANES, tm), jnp.int32),
                        pltpu.SMEM((SUBLANES, tm), jnp.int32),
                        pltpu.SMEM((SUBLANES, LANES), jnp.int32),
                        pltpu.VMEM((MOE_SUPER // MOE_TM + 1, tm, MOE_ROW_WIDTH), F32),
                        pltpu.SemaphoreType.DMA((2,)),
                        pltpu.SemaphoreType.DMA(())],
        compiler_params=_params(("arbitrary",)),
        name="post_b",
    )(h, o, wo, nf, wr_hi, wr_lo, tri)


def _moe_kernel(blk_ref, exp_ref, halves_ref, nact_ref, x_ref, wg_ref, wu_ref, wd_ref, out_hbm,
                xb_ref, y0_ref, y1_ref, ids_ref, ids0_smem, ids1_smem, sem):
    j = pl.program_id(0)
    f = pl.program_id(1)
    nact = nact_ref[0]
    first = slice(0, MOE_TM)
    second = slice(MOE_TM, MOE_SUPER)

    def copy_out(y_ref, ids_smem):
        for r in range(MOE_SUPER):
            _row_copy(y_ref, r, out_hbm, ids_smem[0, r], sem).start(priority=r % 2)

    def half(rows, y_ref, first_chunk):
        if first_chunk:
            xb_ref[rows, :] = x_ref[rows, :D_MODEL].astype(BF16)
        x = xb_ref[rows, :]
        a = (_silu(_dot(x, wg_ref[...])) * _dot(x, wu_ref[...])).astype(BF16)
        part = _dot(a, wd_ref[...])
        if first_chunk:
            y_ref[rows, :] = part
        else:
            y_ref[rows, :] += part

    def step(y_cur, ids_cur, y_prev, ids_prev):
        @pl.when(f == 0)
        def _():
            @pl.when((j >= 1) & (j <= nact + 1))
            def _():
                for rows in (first, second):
                    _tile_wait(out_hbm, y_cur.at[rows], sem)

            @pl.when(j <= nact)
            def _():
                ids = x_ref[:, D_MODEL:].T[0:1].astype(jnp.int32)
                ids_ref[...] = jnp.broadcast_to(ids, ids_ref.shape)
                pltpu.sync_copy(ids_ref, ids_cur)
                copy_out(y_prev, ids_prev)
                half(first, y_cur, True)

                @pl.when(halves_ref[j] == 2)
                def _():
                    half(second, y_cur, True)

                @pl.when(halves_ref[j] == 1)
                def _():
                    y_cur[second, :] = jnp.zeros((MOE_TM, D_MODEL), F32)

        @pl.when((f > 0) & (j < nact))
        def _():
            half(first, y_cur, False)

            @pl.when(halves_ref[j] == 2)
            def _():
                half(second, y_cur, False)

    @pl.when((j == 0) & (f == 0))
    def _():
        y1_ref[...] = jnp.zeros_like(y1_ref)
        ids_ref[...] = MOE_DUMP_ROW + lax.broadcasted_iota(jnp.int32, ids_ref.shape, 1)
        pltpu.sync_copy(ids_ref, ids1_smem)

    @pl.when(j % 2 == 0)
    def _():
        step(y0_ref, ids0_smem, y1_ref, ids1_smem)

    @pl.when(j % 2 == 1)
    def _():
        step(y1_ref, ids1_smem, y0_ref, ids0_smem)


def _moe(xs, tile_block, tile_expert, tile_halves, n_active, wg, wu, wd, tf=D_FF_EXPERT // 2):
    nf = D_FF_EXPERT // tf
    fidx = lambda j, f, nact: jnp.where(j < nact[0], f, nf - 1)
    grid_spec = pltpu.PrefetchScalarGridSpec(
        num_scalar_prefetch=4,
        grid=(MOE_GRID_TILES, nf),
        in_specs=[pl.BlockSpec((MOE_SUPER, MOE_ROW_WIDTH), lambda j, f, blk, exp, hv, nact: (blk[j], 0)),
                  pl.BlockSpec((None, D_MODEL, tf), lambda j, f, blk, exp, hv, nact: (exp[j], 0, fidx(j, f, nact))),
                  pl.BlockSpec((None, D_MODEL, tf), lambda j, f, blk, exp, hv, nact: (exp[j], 0, fidx(j, f, nact))),
                  pl.BlockSpec((None, tf, D_MODEL), lambda j, f, blk, exp, hv, nact: (exp[j], fidx(j, f, nact), 0))],
        out_specs=pl.BlockSpec(memory_space=pl.ANY),
        scratch_shapes=[pltpu.VMEM((MOE_SUPER, D_MODEL), BF16),
                        pltpu.VMEM((MOE_SUPER, D_MODEL), F32),
                        pltpu.VMEM((MOE_SUPER, D_MODEL), F32),
                        pltpu.VMEM((SUBLANES, MOE_SUPER), jnp.int32),
                        pltpu.SMEM((SUBLANES, MOE_SUPER), jnp.int32),
                        pltpu.SMEM((SUBLANES, MOE_SUPER), jnp.int32),
                        pltpu.SemaphoreType.DMA(())],
    )
    return pl.pallas_call(
        _moe_kernel,
        grid_spec=grid_spec,
        out_shape=jax.ShapeDtypeStruct((MOE_OUT_ROWS, D_MODEL), F32),
        compiler_params=_params(("arbitrary", "arbitrary")),
        name="moe",
    )(tile_block, tile_expert, tile_halves, n_active, xs, wg, wu, wd)


def _tile_plan(counts):
    experts = jnp.arange(N_EXPERTS)[None, :]
    halves = (counts + (MOE_TM - 1)) // MOE_TM
    tiles = (halves + 1) // 2
    ends = jnp.cumsum(tiles)
    n_active = ends[-1]
    j = jnp.minimum(jnp.arange(MOE_GRID_TILES, dtype=jnp.int32), n_active - 1)
    expert = jnp.sum((j[:, None] >= ends[None, :]).astype(jnp.int32), axis=1)
    pick = lambda per_expert: jnp.sum(jnp.where(expert[:, None] == experts, per_expert[None, :], 0), axis=1)
    k = j - pick(ends - tiles)
    block = expert * MOE_REGION_TILES + k
    tile_halves = jnp.minimum(pick(halves) - 2 * k, 2)
    i32 = lambda a: a.astype(jnp.int32)
    return i32(block), i32(expert), i32(tile_halves), i32(n_active.reshape(1))


def _final_kernel(h_ref, p_ref, gw_ref, y0_ref, y1_ref, np_ref, wpu_ref, wpg_ref, nfin_ref, out_ref):
    gw = gw_ref[...]
    h = h_ref[...] + gw[:, 0:1] * y0_ref[...] + gw[:, 1:2] * y1_ref[...]
    out_ref[...] = _rms(_ple(h, p_ref, np_ref, wpu_ref, wpg_ref), nfin_ref[...])


def _final(h, p, gw, ys, npl, wpu, wpg, nfin, tm=512):
    n = TOKENS // tm
    tile = lambda width: pl.BlockSpec((tm, width), lambda i: (i, 0))
    consts = [npl, wpu, wpg, nfin]
    return pl.pallas_call(
        _final_kernel,
        grid=(n,),
        in_specs=[tile(D_MODEL), pl.BlockSpec((None, tm, PLE_DIM), lambda i: (1, i, 0)), tile(LANES),
                  tile(D_MODEL), pl.BlockSpec((tm, D_MODEL), lambda i: (n + i, 0))]
                 + [_const_spec(c.shape) for c in consts],
        out_specs=tile(D_MODEL),
        out_shape=jax.ShapeDtypeStruct((TOKENS, D_MODEL), F32),
        compiler_params=_params(("parallel",)),
        name="final",
    )(h, p, gw, ys, ys, *consts)


def kernel(x, p, positions, norm_mix, norm_ffn, norm_ple, w_in_a, ret_gn, w_out_a, norm_kv, w_kv, w_q_b, rel_bias, w_out_b, w_gate_dense, w_up_dense, w_down_dense, w_router, w_gate_moe, w_up_moe, w_down_moe, w_ple_up, w_ple_gate, norm_final):
    bf = lambda w: w.astype(BF16)
    row = lambda g: g.reshape(1, -1).astype(F32)
    h0 = x.reshape(TOKENS, D_MODEL)
    p2 = p.reshape(2, TOKENS, PLE_DIM)
    pos = positions.reshape(TOKENS, 1)
    inv_freq = (1.0 / (ROPE_BASE ** jnp.linspace(0.0, 1.0, ROPE_HALF, dtype=F32))).reshape(1, ROPE_HALF)

    og = _retention_layer(h0, pos, row(norm_mix[0]), inv_freq, bf(w_in_a[0]), row(ret_gn[0]))
    h1 = _post_a(h0, og, p2, bf(w_out_a[0]), row(norm_ffn[0]), bf(w_gate_dense[0]), bf(w_up_dense[0]),
                 bf(w_down_dense[0]), row(norm_ple[0]), bf(w_ple_up[0]), bf(w_ple_gate[0]))

    qt, kp, vt = _qkv(h1, row(norm_mix[1]), row(norm_kv), bf(w_q_b[0]), bf(w_kv))
    ob = _band_attention(qt, kp, vt, _rel_vectors(rel_bias[0]))
    w_router_pad = jnp.zeros((D_MODEL, LANES), F32).at[:, :N_EXPERTS].set(w_router[0])
    wr_hi = bf(w_router_pad)
    wr_lo = bf(w_router_pad - wr_hi.astype(F32))
    h2, route, gw, counts, xs = _post_b(h1, ob, bf(w_out_b[0]), row(norm_ffn[1]), wr_hi, wr_lo)
    tile_block, tile_expert, tile_halves, n_active = _tile_plan(counts[:, 0])
    ys = _moe(xs, tile_block, tile_expert, tile_halves, n_active,
              bf(w_gate_moe[0]), bf(w_up_moe[0]), bf(w_down_moe[0]))
    out = _final(h2, p2, gw, ys, row(norm_ple[1]), bf(w_ple_up[1]), bf(w_ple_gate[1]), row(norm_final))
    return out.reshape(BATCH, SEQ, D_MODEL)
```

```python
import functools

import numpy as np
import jax
import jax.numpy as jnp
from jax import lax
from jax.experimental import pallas as pl
from jax.experimental.pallas import tpu as pltpu

F32 = jnp.float32
BF16 = jnp.bfloat16

D_MODEL = 1024
BATCH = 2
SEQ = 8192
TOKENS = BATCH * SEQ
CHUNK = 64
PLE_DIM = 256

RET_HEADS = 4
RET_QK_DIM = 256
RET_V_DIM = 512
RET_QK_WIDTH = RET_HEADS * RET_QK_DIM
RET_V_WIDTH = RET_HEADS * RET_V_DIM
ROPE_BASE = 10000.0
ROPE_HALF = RET_QK_DIM // 2

ATT_HEADS = 16
ATT_HEAD_DIM = 64
LEFT_CHUNKS = 8
LEFT_PAD = LEFT_CHUNKS * CHUNK
REL_CLIP = 256

D_FF_DENSE = 2816
N_EXPERTS = 8
D_FF_EXPERT = 3584
EPS = 1e-6

LANES = 128
MXU_DIM = 256
VMEM_LIMIT = 56 * 1024 * 1024

RET_SUB = MXU_DIM
RET_BLOCK = 512
ATT_QB = MXU_DIM
ATT_KB = ATT_QB + LEFT_PAD
ATT_QS = 4096
ATT_REL_WIDTH = 1024
ATT_PAD_BLOCKS = LEFT_PAD // ATT_QB
ATT_LIVE = tuple((slice(LANES * t, LANES * t + LEFT_PAD + LANES), slice(LANES * t, LANES * (t + 1)))
                 for t in range(ATT_QB // LANES))
LOG2E = 1.4426950408889634
ATT_HG = MXU_DIM // ATT_HEAD_DIM
NEG_BIG = -1e30

TOP_K = 2
MOE_TM = 512
MOE_SUPER = 2 * MOE_TM
MOE_CAP = TOKENS + 2 * MOE_SUPER
MOE_REGION_TILES = MOE_CAP // MOE_SUPER
MOE_MAX_TILES = TOP_K * TOKENS // MOE_SUPER + N_EXPERTS
MOE_GRID_TILES = MOE_MAX_TILES + 1
MOE_NF = 2
MOE_SPARE_ROW = N_EXPERTS * MOE_CAP
MOE_ROWS = MOE_SPARE_ROW + TOP_K * MOE_TM
MOE_ROW_WIDTH = D_MODEL + LANES
MOE_DUMP_ROW = TOP_K * TOKENS
MOE_N_DUMP = MOE_SUPER + MOE_TM
MOE_OUT_ROWS = MOE_DUMP_ROW + MOE_N_DUMP
SUBLANES = 8


def _dot(a, b):
    return jnp.dot(a, b, preferred_element_type=F32)


def _rms(x, g):
    return x * lax.rsqrt(jnp.mean(x * x, axis=-1, keepdims=True) + EPS) * g


def _silu(x):
    return x * jax.nn.sigmoid(x)


def _const_spec(shape):
    nd = len(shape)
    return pl.BlockSpec(shape, lambda *_: (0,) * nd, pipeline_mode=pl.Buffered(1))


def _params(sem):
    return pltpu.CompilerParams(dimension_semantics=sem, vmem_limit_bytes=VMEM_LIMIT)


def _inproj_tile(x_ref, pos_ref, g_ref, invf_ref, w_ref, q_ref, k_ref, v_ref, gate_ref):
    xn = _rms(x_ref[...], g_ref[...]).astype(BF16)
    v0 = 2 * RET_QK_WIDTH
    g0 = v0 + RET_V_WIDTH
    for c in range(RET_HEADS):
        lo, hi = c * RET_V_DIM, (c + 1) * RET_V_DIM
        v_ref[:, lo:hi] = _dot(xn, w_ref[:, v0 + lo:v0 + hi]).astype(BF16)
        gate_ref[:, lo:hi] = _silu(_dot(xn, w_ref[:, g0 + lo:g0 + hi])).astype(BF16)
    ang = pos_ref[...].astype(F32) * invf_ref[...]
    cos = jnp.cos(ang)
    sin = jnp.sin(ang)
    k_scale = RET_QK_DIM ** -0.5
    for h in range(RET_HEADS):
        lo = h * RET_QK_DIM
        mid = lo + ROPE_HALF
        hi = lo + RET_QK_DIM
        pq = _dot(xn, w_ref[:, lo:hi])
        x1, x2 = pq[:, :ROPE_HALF], pq[:, ROPE_HALF:]
        q_ref[:, lo:mid] = (x1 * cos - x2 * sin).astype(BF16)
        q_ref[:, mid:hi] = (x1 * sin + x2 * cos).astype(BF16)
        pk = _dot(xn, w_ref[:, RET_QK_WIDTH + lo:RET_QK_WIDTH + hi])
        x1, x2 = pk[:, :ROPE_HALF], pk[:, ROPE_HALF:]
        k_ref[:, lo:mid] = ((x1 * cos - x2 * sin) * k_scale).astype(BF16)
        k_ref[:, mid:hi] = ((x1 * sin + x2 * cos) * k_scale).astype(BF16)


def _retention_kernel(gc_ref, x_ref, pos_ref, g_ref, invf_ref, w_ref, dm_ref, xi_ref, zeta_ref, gn_ref,
                      o_ref, q_ref, k_ref, v_ref, gate_ref, state_ref):
    @pl.when(pl.program_id(1) == 0)
    def _():
        state_ref[...] = jnp.zeros_like(state_ref)

    _inproj_tile(x_ref, pos_ref, g_ref, invf_ref, w_ref, q_ref, k_ref, v_ref, gate_ref)

    for h in range(RET_HEADS):
        qk = slice(h * RET_QK_DIM, (h + 1) * RET_QK_DIM)
        vd = slice(h * RET_V_DIM, (h + 1) * RET_V_DIM)
        g_chunk = gc_ref[h]
        dm = dm_ref[h]
        xi = xi_ref[h]
        zeta = zeta_ref[h]
        gn = gn_ref[:, vd]
        for j in range(RET_BLOCK // RET_SUB):
            sl = slice(j * RET_SUB, (j + 1) * RET_SUB)
            qj = q_ref[sl, qk]
            kj = k_ref[sl, qk]
            vj = v_ref[sl, vd]
            st = state_ref[h]
            s = lax.dot_general(qj, kj, (((1,), (1,)), ((), ())), preferred_element_type=F32) * dm
            inner = _dot(s.astype(BF16), vj)
            cross = _dot(qj, st.astype(BF16)) * xi
            kz = (kj.astype(F32) * zeta).astype(BF16)
            upd = lax.dot_general(kz, vj, (((0,), (0,)), ((), ())), preferred_element_type=F32)
            state_ref[h] = st * g_chunk + upd
            o = inner + cross
            on = _rms(o, gn)
            o_ref[sl, vd] = (on * gate_ref[sl, vd].astype(F32)).astype(BF16)


def _retention_layer(x, pos, g, invf, w, gn):
    lg = jnp.log1p(-jnp.exp2(-5.0 - jnp.arange(RET_HEADS, dtype=F32)))
    idx = jnp.arange(RET_SUB, dtype=F32)
    diff = idx[:, None] - idx[None, :]
    causal = diff >= 0
    dmask = jnp.where(causal, jnp.exp(jnp.where(causal, diff, 0.0)[None] * lg[:, None, None]), 0.0)
    xi = jnp.exp((idx[None, :] + 1.0) * lg[:, None])[:, :, None]
    zeta = jnp.exp((RET_SUB - 1.0 - idx)[None, :] * lg[:, None])[:, :, None]
    g_chunk = jnp.exp(RET_SUB * lg)
    nb = SEQ // RET_BLOCK
    tile = lambda width: pl.BlockSpec((RET_BLOCK, width), lambda b, n: (b * nb + n, 0))
    scratch = lambda width: pltpu.VMEM((RET_BLOCK, width), BF16)
    return pl.pallas_call(
        _retention_kernel,
        grid=(BATCH, nb),
        in_specs=[pl.BlockSpec(memory_space=pltpu.SMEM),
                  tile(D_MODEL), tile(1), _const_spec((1, D_MODEL)), _const_spec((1, ROPE_HALF)),
                  _const_spec(w.shape), _const_spec(dmask.shape), _const_spec(xi.shape),
                  _const_spec(zeta.shape), _const_spec(gn.shape)],
        out_specs=tile(RET_V_WIDTH),
        out_shape=jax.ShapeDtypeStruct((TOKENS, RET_V_WIDTH), BF16),
        scratch_shapes=[scratch(RET_QK_WIDTH), scratch(RET_QK_WIDTH), scratch(RET_V_WIDTH),
                        scratch(RET_V_WIDTH), pltpu.VMEM((RET_HEADS, RET_QK_DIM, RET_V_DIM), F32)],
        compiler_params=_params(("parallel", "arbitrary")),
        name="retention",
    )(g_chunk, x, pos, g, invf, w, dmask, xi, zeta, gn)


def _ple(h, p_ref, np_ref, wpu_ref, wpg_ref):
    gate = jax.nn.sigmoid(_dot(_rms(h, np_ref[...]).astype(BF16), wpg_ref[...]))
    return h + _dot(p_ref[...].astype(BF16), wpu_ref[...]) * gate


def _post_a_kernel(h_ref, og_ref, p_ref, wo_ref, nf_ref, wg_ref, wu_ref, wd_ref, np_ref, wpu_ref,
                   wpg_ref, out_ref):
    h1 = h_ref[...] + _dot(og_ref[...], wo_ref[...])
    xn = _rms(h1, nf_ref[...]).astype(BF16)
    a = (_silu(_dot(xn, wg_ref[...])) * _dot(xn, wu_ref[...])).astype(BF16)
    h2 = h1 + _dot(a, wd_ref[...])
    out_ref[...] = _ple(h2, p_ref, np_ref, wpu_ref, wpg_ref)


def _post_a(h, og, p, wo, nf, wg, wu, wd, npl, wpu, wpg, tm=512):
    n = TOKENS // tm
    tile = lambda width: pl.BlockSpec((tm, width), lambda i: (i, 0))
    consts = [wo, nf, wg, wu, wd, npl, wpu, wpg]
    return pl.pallas_call(
        _post_a_kernel,
        grid=(n,),
        in_specs=[tile(D_MODEL), tile(RET_V_WIDTH), pl.BlockSpec((None, tm, PLE_DIM), lambda i: (0, i, 0))]
                 + [_const_spec(c.shape) for c in consts],
        out_specs=tile(D_MODEL),
        out_shape=jax.ShapeDtypeStruct((TOKENS, D_MODEL), F32),
        compiler_params=_params(("parallel",)),
        name="post_a",
    )(h, og, p, *consts)


def _qkv_kernel(h_ref, gq_ref, gkv_ref, wq_ref, wkv_ref, qt_ref, k_ref, vt_ref):
    h = h_ref[...]
    hn = h * lax.rsqrt(jnp.mean(h * h, axis=-1, keepdims=True) + EPS)
    q = _dot((hn * gq_ref[...]).astype(BF16), wq_ref[...]) * (ATT_HEAD_DIM ** -0.5 * LOG2E)
    qt_ref[...] = q.T.astype(BF16)
    kv = _dot((hn * gkv_ref[...]).astype(BF16), wkv_ref[...])
    keep = (pl.program_id(1) > 0).astype(F32)
    k_ref[...] = (kv[:, :D_MODEL] * keep).astype(BF16)
    vt_ref[...] = (kv[:, D_MODEL:] * keep).T.astype(BF16)


def _qkv(h, gq, gkv, wq, wkv):
    tm = LEFT_PAD
    nb = SEQ // tm
    src = lambda b, j: (b * nb + jnp.maximum(j - 1, 0), 0)
    return pl.pallas_call(
        _qkv_kernel,
        grid=(BATCH, nb + 1),
        in_specs=[pl.BlockSpec((tm, D_MODEL), src), _const_spec((1, D_MODEL)), _const_spec((1, D_MODEL)),
                  _const_spec(wq.shape), _const_spec(wkv.shape)],
        out_specs=[pl.BlockSpec((D_MODEL, tm), lambda b, j: (0, b * nb + jnp.maximum(j - 1, 0))),
                   pl.BlockSpec((None, tm, D_MODEL), lambda b, j: (b, j, 0)),
                   pl.BlockSpec((None, D_MODEL, tm), lambda b, j: (b, 0, j))],
        out_shape=[jax.ShapeDtypeStruct((D_MODEL, TOKENS), BF16),
                   jax.ShapeDtypeStruct((BATCH, SEQ + LEFT_PAD, D_MODEL), BF16),
                   jax.ShapeDtypeStruct((BATCH, D_MODEL, SEQ + LEFT_PAD), BF16)],
        compiler_params=_params(("parallel", "arbitrary")),
        name="qkv_b",
    )(h, gq, gkv, wq, wkv)


def _attn_kernel(rel_ref, qt_ref, k_ref, vt_ref, o_ref, bias_ref, s0_ref, s1_ref, p0_ref, p1_ref):
    n = pl.program_id(2)

    @pl.when(n == 0)
    def _():
        kj = lax.broadcasted_iota(jnp.int32, (ATT_KB, ATT_QB), 0)
        qi = lax.broadcasted_iota(jnp.int32, (ATT_KB, ATT_QB), 1)
        k_chunk = kj // CHUNK - LEFT_CHUNKS
        q_chunk = qi // CHUNK
        allowed = (k_chunk <= q_chunk) & (k_chunk >= q_chunk - LEFT_CHUNKS)
        for g in range(ATT_HG):
            base = jnp.broadcast_to(rel_ref[g], (ATT_KB, ATT_REL_WIDTH))
            rolled = pltpu.roll(base, 0, 1, stride=1, stride_axis=0)
            table = jnp.where(allowed, rolled[:, :ATT_QB], NEG_BIG)
            bias_ref[0, g] = table
            for v in range(1, ATT_PAD_BLOCKS + 1):
                bias_ref[v, g] = jnp.where(kj >= LEFT_PAD - (v - 1) * ATT_QB, table, NEG_BIG)
        p0_ref[...] = jnp.zeros_like(p0_ref)
        p1_ref[...] = jnp.zeros_like(p1_ref)

    head_of_row = lax.broadcasted_iota(jnp.int32, (MXU_DIM, ATT_QB), 0) // ATT_HEAD_DIM
    n_blocks = ATT_QS // ATT_QB

    def window(qb):
        qb = jnp.minimum(qb, n_blocks - 1)
        q0 = pl.multiple_of(qb * ATT_QB, ATT_QB)
        kstart = pl.multiple_of(n * ATT_QS + qb * ATT_QB, ATT_QB)
        return q0, kstart

    def scores(qb, g, s_ref):
        q0, kstart = window(qb)
        qt = qt_ref[:, pl.ds(q0, ATT_QB)]
        kb = k_ref[pl.ds(kstart, ATT_KB), :]
        qg = jnp.where(head_of_row == g, qt, jnp.zeros_like(qt))
        block_in_seq = kstart // ATT_QB
        variant = jnp.where(block_in_seq < ATT_PAD_BLOCKS, block_in_seq + 1, 0)
        s = _dot(kb, qg)
        maxes = []
        for rows, lanes in ATT_LIVE:
            part = s[rows, lanes] + bias_ref[variant, g, rows, lanes]
            s_ref[rows, lanes] = part
            maxes.append(jnp.max(part, axis=0, keepdims=True))
        return jnp.concatenate(maxes, axis=1)

    def probabilities(s_ref, p_ref, m):
        for rows, lanes in ATT_LIVE:
            p_ref[rows, lanes] = jnp.exp2(s_ref[rows, lanes] - m[:, lanes]).astype(BF16)

    ones_rows = jnp.ones((2 * SUBLANES, ATT_KB), BF16)

    def values(qb, g, p_ref):
        _, kstart = window(qb)
        vg = vt_ref[g * ATT_HEAD_DIM:(g + 1) * ATT_HEAD_DIM, pl.ds(kstart, ATT_KB)]
        pv = _dot(jnp.concatenate([vg, ones_rows], axis=0), p_ref[...])
        return pv[:ATT_HEAD_DIM] / pv[ATT_HEAD_DIM:ATT_HEAD_DIM + 1]

    s_bufs = (s0_ref, s1_ref)
    p_bufs = (p0_ref, p1_ref)
    m_first = scores(0, 0, s_bufs[0])
    m_second = scores(0, 1, s_bufs[1])
    probabilities(s_bufs[0], p_bufs[0], m_first)

    def body(qb, m_next):
        outs = []
        for g in range(ATT_HG):
            slot = g % 2
            outs.append(values(qb, g, p_bufs[slot]))
            probabilities(s_bufs[1 - slot], p_bufs[1 - slot], m_next)
            ahead = g + 2
            m_next = scores(qb + ahead // ATT_HG, ahead % ATT_HG, s_bufs[slot])
        q0, _ = window(qb)
        o_ref[pl.ds(q0, ATT_QB), :] = jnp.concatenate(outs, axis=0).T.astype(BF16)
        return m_next

    lax.fori_loop(0, n_blocks, body, m_second)


def _rel_vectors(rel_table):
    heads, n_rel = rel_table.shape
    far = rel_table[:, n_rel - 1:]
    near = rel_table[:, :1]
    n_far_front = ATT_QB + 1
    n_near = ATT_REL_WIDTH - LEFT_PAD - (CHUNK - 1) - n_far_front
    n_far_back = ATT_REL_WIDTH - n_far_front - n_near - n_rel
    g = jnp.concatenate([jnp.broadcast_to(far, (heads, n_far_front)),
                         jnp.broadcast_to(near, (heads, n_near)),
                         rel_table,
                         jnp.broadcast_to(far, (heads, n_far_back))], axis=1)
    return (g.astype(F32) * LOG2E).reshape(heads, 1, ATT_REL_WIDTH)


def _band_attention(qt, kp, vt, rel):
    ns = SEQ // ATT_QS
    return pl.pallas_call(
        _attn_kernel,
        grid=(BATCH, ATT_HEADS // ATT_HG, ns),
        in_specs=[pl.BlockSpec((ATT_HG, 1, ATT_REL_WIDTH), lambda b, g, n: (g, 0, 0)),
                  pl.BlockSpec((MXU_DIM, ATT_QS), lambda b, g, n: (g, b * ns + n)),
                  pl.BlockSpec((None, SEQ + LEFT_PAD, MXU_DIM), lambda b, g, n: (b, 0, g)),
                  pl.BlockSpec((None, MXU_DIM, SEQ + LEFT_PAD), lambda b, g, n: (b, g, 0))],
        out_specs=pl.BlockSpec((ATT_QS, MXU_DIM), lambda b, g, n: (b * ns + n, g)),
        out_shape=jax.ShapeDtypeStruct((TOKENS, D_MODEL), BF16),
        scratch_shapes=[pltpu.VMEM((ATT_PAD_BLOCKS + 1, ATT_HG, ATT_KB, ATT_QB), F32),
                        pltpu.VMEM((ATT_KB, ATT_QB), F32), pltpu.VMEM((ATT_KB, ATT_QB), F32),
                        pltpu.VMEM((ATT_KB, ATT_QB), BF16), pltpu.VMEM((ATT_KB, ATT_QB), BF16)],
        compiler_params=_params(("parallel", "parallel", "arbitrary")),
        name="band_attn",
    )(rel, qt, kp, vt)


def _row_copy(src, src_row, dst, dst_row, sem):
    return pltpu.make_async_copy(src.at[pl.ds(src_row, 1)], dst.at[pl.ds(dst_row, 1)], sem)


def _tile_wait(hbm, vmem_tile, sem):
    pltpu.make_async_copy(hbm.at[pl.ds(0, MOE_TM)], vmem_tile, sem).wait()


def _scatter_tile(xbuf, rs_smem, xs_hbm, sem):
    for t in range(MOE_TM):
        for which in range(TOP_K):
            _row_copy(xbuf.at[which], t, xs_hbm, rs_smem[which, t], sem).start(priority=which)


def _fill_dispatch_rows(xbuf, xn, first_token):
    token = first_token + lax.broadcasted_iota(jnp.int32, (MOE_TM, LANES), 0)
    for which in range(TOP_K):
        xbuf[which, :, :D_MODEL] = xn
        xbuf[which, :, D_MODEL:] = (which * TOKENS + token).astype(F32)


def _route_tile(rows, h_ref, o_ref, wo_ref, nf_ref, wrh_ref, wrl_ref, tri_ref,
                h_out_ref, route_ref, gw_ref, cnt_ref, carry_ref):
    cols = rows
    h = h_ref[rows, :] + _dot(o_ref[rows, :], wo_ref[...])
    h_out_ref[rows, :] = h
    xn = _rms(h, nf_ref[...])

    xh = xn.astype(BF16)
    xl = (xn - xh.astype(F32)).astype(BF16)
    logits = _dot(xh, wrh_ref[...]) + _dot(xl, wrh_ref[...]) + _dot(xh, wrl_ref[...])
    lt = logits.T[:N_EXPERTS]
    row = lax.broadcasted_iota(jnp.int32, lt.shape, 0)
    m1 = jnp.max(lt, axis=0, keepdims=True)
    i1 = jnp.min(jnp.where(lt == m1, row, N_EXPERTS), axis=0, keepdims=True)
    rest = jnp.where(row == i1, -jnp.inf, lt)
    m2 = jnp.max(rest, axis=0, keepdims=True)
    i2 = jnp.min(jnp.where(rest == m2, row, N_EXPERTS), axis=0, keepdims=True)
    e2 = jnp.exp(m2 - m1)
    w1 = 1.0 / (1.0 + e2)
    w2 = e2 / (1.0 + e2)

    member = ((row == i1) | (row == i2)).astype(F32)
    within = _dot(member.astype(BF16), tri_ref[...])
    carry = carry_ref[...]
    rank = within + carry[:, 0:1]
    r1 = jnp.sum(jnp.where(row == i1, rank, 0.0), axis=0, keepdims=True).astype(jnp.int32)
    r2 = jnp.sum(jnp.where(row == i2, rank, 0.0), axis=0, keepdims=True).astype(jnp.int32)
    carry_ref[...] = carry + jnp.sum(member, axis=1, keepdims=True)
    cnt_ref[...] = carry_ref[...].astype(jnp.int32)

    s1 = i1 * MOE_CAP + r1
    s2 = i2 * MOE_CAP + r2
    route_ref[:, cols] = jnp.where(row == 0, s1, jnp.where(row == 1, s2, 0))
    gwt = jnp.where(row == 0, w1, jnp.where(row == 1, w2, 0.0))
    pad = jnp.zeros((LANES - N_EXPERTS, gwt.shape[1]), F32)
    gw_ref[rows, :] = jnp.concatenate([gwt, pad], axis=0).T
    return xn


def _post_b_kernel(h_ref, o_ref, wo_ref, nf_ref, wrh_ref, wrl_ref, tri_ref,
                   h_out_ref, route_ref, gw_ref, cnt_ref, xs_hbm,
                   xbuf0, xbuf1, carry_ref, rs0, rs1, cnt_smem, zero_ref, sems, tail_sem):
    g = pl.program_id(0)
    tile_args = (h_ref, o_ref, wo_ref, nf_ref, wrh_ref, wrl_ref, tri_ref,
                 h_out_ref, route_ref, gw_ref, cnt_ref, carry_ref)
    first = slice(0, MOE_TM)
    second = slice(MOE_TM, 2 * MOE_TM)

    @pl.when(g == 0)
    def _():
        carry_ref[...] = jnp.zeros_like(carry_ref)
        xbuf1[...] = jnp.zeros_like(xbuf1)
        which = lax.broadcasted_iota(jnp.int32, (SUBLANES, MOE_TM), 0)
        token = lax.broadcasted_iota(jnp.int32, (SUBLANES, MOE_TM), 1)
        route_ref[:, first] = MOE_SPARE_ROW + jnp.minimum(which, TOP_K - 1) * MOE_TM + token
        pltpu.sync_copy(route_ref.at[:, first], rs1)

    _scatter_tile(xbuf1, rs1, xs_hbm, sems.at[1])
    xn = _route_tile(first, *tile_args)

    @pl.when(g >= 1)
    def _():
        for which in range(TOP_K):
            _tile_wait(xs_hbm, xbuf0.at[which], sems.at[0])

    _fill_dispatch_rows(xbuf0, xn, 2 * g * MOE_TM)
    pltpu.sync_copy(route_ref.at[:, first], rs0)

    _scatter_tile(xbuf0, rs0, xs_hbm, sems.at[0])
    xn = _route_tile(second, *tile_args)
    for which in range(TOP_K):
        _tile_wait(xs_hbm, xbuf1.at[which], sems.at[1])
    _fill_dispatch_rows(xbuf1, xn, (2 * g + 1) * MOE_TM)
    pltpu.sync_copy(route_ref.at[:, second], rs1)

    @pl.when(g == pl.num_programs(0) - 1)
    def _():
        _scatter_tile(xbuf1, rs1, xs_hbm, sems.at[1])
        for buf, sem in ((xbuf0, sems.at[0]), (xbuf1, sems.at[1])):
            for which in range(TOP_K):
                _tile_wait(xs_hbm, buf.at[which], sem)

        pltpu.sync_copy(cnt_ref, cnt_smem)
        n_parts = MOE_SUPER // MOE_TM
        spare = lax.broadcasted_iota(jnp.int32, (MOE_TM, LANES), 0) + MOE_DUMP_ROW
        for part in range(n_parts + 1):
            zero_ref[part, :, :D_MODEL] = jnp.zeros((MOE_TM, D_MODEL), F32)
            zero_ref[part, :, D_MODEL:] = (spare + part * MOE_TM).astype(F32)
        for e in range(N_EXPERTS):
            count = cnt_smem[e, 0]
            aligned = ((count + (SUBLANES - 1)) // SUBLANES) * SUBLANES
            for part in range(n_parts):
                start = pl.multiple_of(e * MOE_CAP + aligned + part * MOE_TM, SUBLANES)
                pltpu.make_async_copy(zero_ref.at[part], xs_hbm.at[pl.ds(start, MOE_TM)], tail_sem).start()
            for k in range(SUBLANES - 1):
                @pl.when(count + k < aligned)
                def _():
                    _row_copy(zero_ref.at[n_parts], k, xs_hbm, e * MOE_CAP + count + k, tail_sem).start()
        for e in range(N_EXPERTS):
            count = cnt_smem[e, 0]
            aligned = ((count + (SUBLANES - 1)) // SUBLANES) * SUBLANES
            for part in range(n_parts):
                _tile_wait(xs_hbm, zero_ref.at[part], tail_sem)
            for k in range(SUBLANES - 1):
                @pl.when(count + k < aligned)
                def _():
                    _row_copy(zero_ref.at[n_parts], k, xs_hbm, 0, tail_sem).wait()


def _post_b(h, o, wo, nf, wr_hi, wr_lo):
    tm = MOE_TM
    pair = 2 * tm
    n = TOKENS // pair
    tile = lambda width: pl.BlockSpec((pair, width), lambda i: (i, 0))
    tri = jnp.asarray(np.triu(np.ones((tm, tm), np.float32), k=1), dtype=BF16)
    return pl.pallas_call(
        _post_b_kernel,
        grid=(n,),
        in_specs=[tile(D_MODEL), tile(D_MODEL), _const_spec(wo.shape), _const_spec((1, D_MODEL)),
                  _const_spec(wr_hi.shape), _const_spec(wr_lo.shape), _const_spec(tri.shape)],
        out_specs=[tile(D_MODEL),
                   pl.BlockSpec((SUBLANES, pair), lambda i: (0, i)),
                   tile(LANES),
                   pl.BlockSpec((SUBLANES, LANES), lambda i: (0, 0)),
                   pl.BlockSpec(memory_space=pl.ANY)],
        out_shape=[jax.ShapeDtypeStruct((TOKENS, D_MODEL), F32),
                   jax.ShapeDtypeStruct((SUBLANES, TOKENS), jnp.int32),
                   jax.ShapeDtypeStruct((TOKENS, LANES), F32),
                   jax.ShapeDtypeStruct((SUBLANES, LANES), jnp.int32),
                   jax.ShapeDtypeStruct((MOE_ROWS, MOE_ROW_WIDTH), F32)],
        scratch_shapes=[pltpu.VMEM((TOP_K, tm, MOE_ROW_WIDTH), F32),
                        pltpu.VMEM((TOP_K, tm, MOE_ROW_WIDTH), F32),
                        pltpu.VMEM((SUBLANES, LANES), F32),
                        pltpu.SMEM((SUBLANES, tm), jnp.int32),
                        pltpu.SMEM((SUBLANES, tm), jnp.int32),
                        pltpu.SMEM((SUBLANES, LANES), jnp.int32),
                        pltpu.VMEM((MOE_SUPER // MOE_TM + 1, tm, MOE_ROW_WIDTH), F32),
                        pltpu.SemaphoreType.DMA((2,)),
                        pltpu.SemaphoreType.DMA(())],
        compiler_params=_params(("arbitrary",)),
        name="post_b",
    )(h, o, wo, nf, wr_hi, wr_lo, tri)


def _moe_kernel(blk_ref, exp_ref, halves_ref, nact_ref, x_ref, wg_ref, wu_ref, wd_ref, out_hbm,
                xb_ref, acc_ref, done_ref, ids_ref, ids_smem, sem):
    j = pl.program_id(0)
    f = pl.program_id(1)
    nf = pl.num_programs(1)
    nact = nact_ref[0]
    first = slice(0, MOE_TM)
    second = slice(MOE_TM, MOE_SUPER)
    share = MOE_SUPER // MOE_NF

    def half(rows):
        x = xb_ref[rows, :]
        a = (_silu(_dot(x, wg_ref[...])) * _dot(x, wu_ref[...])).astype(BF16)
        acc_ref[rows, :] += _dot(a, wd_ref[...])

    @pl.when((j == 0) & (f == 0))
    def _():
        done_ref[...] = jnp.zeros_like(done_ref)
        ids_ref[...] = MOE_DUMP_ROW + lax.broadcasted_iota(jnp.int32, ids_ref.shape, 1)
        pltpu.sync_copy(ids_ref, ids_smem)

    @pl.when(j <= nact)
    def _():
        @pl.when(f == 0)
        def _():
            xb_ref[...] = x_ref[:, :D_MODEL].astype(BF16)
            acc_ref[...] = jnp.zeros_like(acc_ref)
            ids = x_ref[:, D_MODEL:].T[0:1].astype(jnp.int32)
            ids_ref[...] = jnp.broadcast_to(ids, ids_ref.shape)

        base = f * share
        for r in range(share):
            _row_copy(done_ref, base + r, out_hbm, ids_smem[0, base + r], sem).start(priority=r % 2)
        half(first)

        @pl.when(halves_ref[j] == 2)
        def _():
            half(second)

        @pl.when(f == nf - 1)
        def _():
            for rows in (first, second):
                _tile_wait(out_hbm, done_ref.at[rows], sem)
            done_ref[...] = acc_ref[...]
            pltpu.sync_copy(ids_ref, ids_smem)


def _moe(xs, tile_block, tile_expert, tile_halves, n_active, wg, wu, wd):
    nf = MOE_NF
    tf = D_FF_EXPERT // nf
    fidx = lambda j, f, nact: jnp.where(j < nact[0], f, nf - 1)
    grid_spec = pltpu.PrefetchScalarGridSpec(
        num_scalar_prefetch=4,
        grid=(MOE_GRID_TILES, nf),
        in_specs=[pl.BlockSpec((MOE_SUPER, MOE_ROW_WIDTH), lambda j, f, blk, exp, hv, nact: (blk[j], 0)),
                  pl.BlockSpec((None, D_MODEL, tf), lambda j, f, blk, exp, hv, nact: (exp[j], 0, fidx(j, f, nact))),
                  pl.BlockSpec((None, D_MODEL, tf), lambda j, f, blk, exp, hv, nact: (exp[j], 0, fidx(j, f, nact))),
                  pl.BlockSpec((None, tf, D_MODEL), lambda j, f, blk, exp, hv, nact: (exp[j], fidx(j, f, nact), 0))],
        out_specs=pl.BlockSpec(memory_space=pl.ANY),
        scratch_shapes=[pltpu.VMEM((MOE_SUPER, D_MODEL), BF16),
                        pltpu.VMEM((MOE_SUPER, D_MODEL), F32),
                        pltpu.VMEM((MOE_SUPER, D_MODEL), F32),
                        pltpu.VMEM((SUBLANES, MOE_SUPER), jnp.int32),
                        pltpu.SMEM((SUBLANES, MOE_SUPER), jnp.int32),
                        pltpu.SemaphoreType.DMA(())],
    )
    return pl.pallas_call(
        _moe_kernel,
        grid_spec=grid_spec,
        out_shape=jax.ShapeDtypeStruct((MOE_OUT_ROWS, D_MODEL), F32),
        compiler_params=_params(("arbitrary", "arbitrary")),
        name="moe",
    )(tile_block, tile_expert, tile_halves, n_active, xs, wg, wu, wd)


def _tile_plan(counts):
    experts = jnp.arange(N_EXPERTS)[None, :]
    halves = (counts + (MOE_TM - 1)) // MOE_TM
    tiles = (halves + 1) // 2
    ends = jnp.cumsum(tiles)
    n_active = ends[-1]
    j = jnp.minimum(jnp.arange(MOE_GRID_TILES, dtype=jnp.int32), n_active - 1)
    expert = jnp.sum((j[:, None] >= ends[None, :]).astype(jnp.int32), axis=1)
    pick = lambda per_expert: jnp.sum(jnp.where(expert[:, None] == experts, per_expert[None, :], 0), axis=1)
    k = j - pick(ends - tiles)
    block = expert * MOE_REGION_TILES + k
    tile_halves = jnp.minimum(pick(halves) - 2 * k, 2)
    i32 = lambda a: a.astype(jnp.int32)
    return i32(block), i32(expert), i32(tile_halves), i32(n_active.reshape(1))


def _final_kernel(h_ref, p_ref, gw_ref, y0_ref, y1_ref, np_ref, wpu_ref, wpg_ref, nfin_ref, out_ref):
    gw = gw_ref[...]
    h = h_ref[...] + gw[:, 0:1] * y0_ref[...] + gw[:, 1:2] * y1_ref[...]
    out_ref[...] = _rms(_ple(h, p_ref, np_ref, wpu_ref, wpg_ref), nfin_ref[...])


def _final(h, p, gw, ys, npl, wpu, wpg, nfin, tm=512):
    n = TOKENS // tm
    tile = lambda width: pl.BlockSpec((tm, width), lambda i: (i, 0))
    consts = [npl, wpu, wpg, nfin]
    return pl.pallas_call(
        _final_kernel,
        grid=(n,),
        in_specs=[tile(D_MODEL), pl.BlockSpec((None, tm, PLE_DIM), lambda i: (1, i, 0)), tile(LANES),
                  tile(D_MODEL), pl.BlockSpec((tm, D_MODEL), lambda i: (n + i, 0))]
                 + [_const_spec(c.shape) for c in consts],
        out_specs=tile(D_MODEL),
        out_shape=jax.ShapeDtypeStruct((TOKENS, D_MODEL), F32),
        compiler_params=_params(("parallel",)),
        name="final",
    )(h, p, gw, ys, ys, *consts)


def kernel(x, p, positions, norm_mix, norm_ffn, norm_ple, w_in_a, ret_gn, w_out_a, norm_kv, w_kv, w_q_b, rel_bias, w_out_b, w_gate_dense, w_up_dense, w_down_dense, w_router, w_gate_moe, w_up_moe, w_down_moe, w_ple_up, w_ple_gate, norm_final):
    bf = lambda w: w.astype(BF16)
    row = lambda g: g.reshape(1, -1).astype(F32)
    h0 = x.reshape(TOKENS, D_MODEL)
    p2 = p.reshape(2, TOKENS, PLE_DIM)
    pos = positions.reshape(TOKENS, 1)
    inv_freq = (1.0 / (ROPE_BASE ** jnp.linspace(0.0, 1.0, ROPE_HALF, dtype=F32))).reshape(1, ROPE_HALF)

    og = _retention_layer(h0, pos, row(norm_mix[0]), inv_freq, bf(w_in_a[0]), row(ret_gn[0]))
    h1 = _post_a(h0, og, p2, bf(w_out_a[0]), row(norm_ffn[0]), bf(w_gate_dense[0]), bf(w_up_dense[0]),
                 bf(w_down_dense[0]), row(norm_ple[0]), bf(w_ple_up[0]), bf(w_ple_gate[0]))

    qt, kp, vt = _qkv(h1, row(norm_mix[1]), row(norm_kv), bf(w_q_b[0]), bf(w_kv))
    ob = _band_attention(qt, kp, vt, _rel_vectors(rel_bias[0]))
    w_router_pad = jnp.zeros((D_MODEL, LANES), F32).at[:, :N_EXPERTS].set(w_router[0])
    wr_hi = bf(w_router_pad)
    wr_lo = bf(w_router_pad - wr_hi.astype(F32))
    h2, route, gw, counts, xs = _post_b(h1, ob, bf(w_out_b[0]), row(norm_ffn[1]), wr_hi, wr_lo)
    tile_block, tile_expert, tile_halves, n_active = _tile_plan(counts[:, 0])
    ys = _moe(xs, tile_block, tile_expert, tile_halves, n_active,
              bf(w_gate_moe[0]), bf(w_up_moe[0]), bf(w_down_moe[0]))
    out = _final(h2, p2, gw, ys, row(norm_ple[1]), bf(w_ple_up[1]), bf(w_ple_gate[1]), row(norm_final))
    return out.reshape(BATCH, SEQ, D_MODEL)
```

```python
import functools

import numpy as np
import jax
import jax.numpy as jnp
from jax import lax
from jax.experimental import pallas as pl
from jax.experimental.pallas import tpu as pltpu

F32 = jnp.float32
BF16 = jnp.bfloat16

D_MODEL = 1024
BATCH = 2
SEQ = 8192
TOKENS = BATCH * SEQ
CHUNK = 64
PLE_DIM = 256

RET_HEADS = 4
RET_QK_DIM = 256
RET_V_DIM = 512
RET_QK_WIDTH = RET_HEADS * RET_QK_DIM
RET_V_WIDTH = RET_HEADS * RET_V_DIM
ROPE_BASE = 10000.0
ROPE_HALF = RET_QK_DIM // 2

ATT_HEADS = 16
ATT_HEAD_DIM = 64
LEFT_CHUNKS = 8
LEFT_PAD = LEFT_CHUNKS * CHUNK
REL_CLIP = 256

D_FF_DENSE = 2816
N_EXPERTS = 8
D_FF_EXPERT = 3584
EPS = 1e-6

LANES = 128
MXU_DIM = 256
VMEM_LIMIT = 56 * 1024 * 1024

RET_SUB = MXU_DIM
RET_BLOCK = 512
ATT_QB = MXU_DIM
ATT_KB = ATT_QB + LEFT_PAD
ATT_QS = SEQ
ATT_REL_WIDTH = 1024
ATT_PAD_BLOCKS = LEFT_PAD // ATT_QB
ATT_LIVE = tuple((slice(LANES * t, LANES * t + LEFT_PAD + LANES), slice(LANES * t, LANES * (t + 1)))
                 for t in range(ATT_QB // LANES))
LOG2E = 1.4426950408889634
ATT_HG = MXU_DIM // ATT_HEAD_DIM
NEG_BIG = -1e30

TOP_K = 2
MOE_TM = 512
MOE_SUPER = 2 * MOE_TM
MOE_CAP = TOKENS + 2 * MOE_SUPER
MOE_REGION_TILES = MOE_CAP // MOE_SUPER
MOE_MAX_TILES = TOP_K * TOKENS // MOE_SUPER + N_EXPERTS
MOE_GRID_TILES = MOE_MAX_TILES + 1
MOE_NF = 2
MOE_SPARE_ROW = N_EXPERTS * MOE_CAP
MOE_ROWS = MOE_SPARE_ROW + TOP_K * MOE_TM
MOE_ROW_WIDTH = D_MODEL + LANES
MOE_DUMP_ROW = TOP_K * TOKENS
MOE_N_DUMP = MOE_SUPER + MOE_TM
MOE_OUT_ROWS = MOE_DUMP_ROW + MOE_N_DUMP
SUBLANES = 8


def _dot(a, b):
    return jnp.dot(a, b, preferred_element_type=F32)


def _rms(x, g):
    return x * lax.rsqrt(jnp.mean(x * x, axis=-1, keepdims=True) + EPS) * g


def _silu(x):
    return x * jax.nn.sigmoid(x)


def _const_spec(shape):
    nd = len(shape)
    return pl.BlockSpec(shape, lambda *_: (0,) * nd, pipeline_mode=pl.Buffered(1))


def _params(sem):
    return pltpu.CompilerParams(dimension_semantics=sem, vmem_limit_bytes=VMEM_LIMIT)


def _inproj_tile(x_ref, pos_ref, g_ref, invf_ref, w_ref, q_ref, k_ref, v_ref, gate_ref):
    xn = _rms(x_ref[...], g_ref[...]).astype(BF16)
    v0 = 2 * RET_QK_WIDTH
    g0 = v0 + RET_V_WIDTH
    for c in range(RET_HEADS):
        lo, hi = c * RET_V_DIM, (c + 1) * RET_V_DIM
        v_ref[:, lo:hi] = _dot(xn, w_ref[:, v0 + lo:v0 + hi]).astype(BF16)
        gate_ref[:, lo:hi] = _silu(_dot(xn, w_ref[:, g0 + lo:g0 + hi])).astype(BF16)
    ang = pos_ref[...].astype(F32) * invf_ref[...]
    cos = jnp.cos(ang)
    sin = jnp.sin(ang)
    k_scale = RET_QK_DIM ** -0.5
    for h in range(RET_HEADS):
        lo = h * RET_QK_DIM
        mid = lo + ROPE_HALF
        hi = lo + RET_QK_DIM
        pq = _dot(xn, w_ref[:, lo:hi])
        x1, x2 = pq[:, :ROPE_HALF], pq[:, ROPE_HALF:]
        q_ref[:, lo:mid] = (x1 * cos - x2 * sin).astype(BF16)
        q_ref[:, mid:hi] = (x1 * sin + x2 * cos).astype(BF16)
        pk = _dot(xn, w_ref[:, RET_QK_WIDTH + lo:RET_QK_WIDTH + hi])
        x1, x2 = pk[:, :ROPE_HALF], pk[:, ROPE_HALF:]
        k_ref[:, lo:mid] = ((x1 * cos - x2 * sin) * k_scale).astype(BF16)
        k_ref[:, mid:hi] = ((x1 * sin + x2 * cos) * k_scale).astype(BF16)


def _retention_kernel(gc_ref, x_ref, pos_ref, g_ref, invf_ref, w_ref, dm_ref, xi_ref, zeta_ref, gn_ref,
                      o_ref, q_ref, k_ref, v_ref, gate_ref, state_ref):
    @pl.when(pl.program_id(1) == 0)
    def _():
        state_ref[...] = jnp.zeros_like(state_ref)

    _inproj_tile(x_ref, pos_ref, g_ref, invf_ref, w_ref, q_ref, k_ref, v_ref, gate_ref)

    for h in range(RET_HEADS):
        qk = slice(h * RET_QK_DIM, (h + 1) * RET_QK_DIM)
        vd = slice(h * RET_V_DIM, (h + 1) * RET_V_DIM)
        g_chunk = gc_ref[h]
        dm = dm_ref[h]
        xi = xi_ref[h]
        zeta = zeta_ref[h]
        gn = gn_ref[:, vd]
        for j in range(RET_BLOCK // RET_SUB):
            sl = slice(j * RET_SUB, (j + 1) * RET_SUB)
            qj = q_ref[sl, qk]
            kj = k_ref[sl, qk]
            vj = v_ref[sl, vd]
            st = state_ref[h]
            s = lax.dot_general(qj, kj, (((1,), (1,)), ((), ())), preferred_element_type=F32) * dm
            inner = _dot(s.astype(BF16), vj)
            cross = _dot(qj, st.astype(BF16)) * xi
            kz = (kj.astype(F32) * zeta).astype(BF16)
            upd = lax.dot_general(kz, vj, (((0,), (0,)), ((), ())), preferred_element_type=F32)
            state_ref[h] = st * g_chunk + upd
            o = inner + cross
            on = _rms(o, gn)
            o_ref[sl, vd] = (on * gate_ref[sl, vd].astype(F32)).astype(BF16)


def _retention_layer(x, pos, g, invf, w, gn):
    lg = jnp.log1p(-jnp.exp2(-5.0 - jnp.arange(RET_HEADS, dtype=F32)))
    idx = jnp.arange(RET_SUB, dtype=F32)
    diff = idx[:, None] - idx[None, :]
    causal = diff >= 0
    dmask = jnp.where(causal, jnp.exp(jnp.where(causal, diff, 0.0)[None] * lg[:, None, None]), 0.0)
    xi = jnp.exp((idx[None, :] + 1.0) * lg[:, None])[:, :, None]
    zeta = jnp.exp((RET_SUB - 1.0 - idx)[None, :] * lg[:, None])[:, :, None]
    g_chunk = jnp.exp(RET_SUB * lg)
    nb = SEQ // RET_BLOCK
    tile = lambda width: pl.BlockSpec((RET_BLOCK, width), lambda b, n: (b * nb + n, 0))
    scratch = lambda width: pltpu.VMEM((RET_BLOCK, width), BF16)
    return pl.pallas_call(
        _retention_kernel,
        grid=(BATCH, nb),
        in_specs=[pl.BlockSpec(memory_space=pltpu.SMEM),
                  tile(D_MODEL), tile(1), _const_spec((1, D_MODEL)), _const_spec((1, ROPE_HALF)),
                  _const_spec(w.shape), _const_spec(dmask.shape), _const_spec(xi.shape),
                  _const_spec(zeta.shape), _const_spec(gn.shape)],
        out_specs=tile(RET_V_WIDTH),
        out_shape=jax.ShapeDtypeStruct((TOKENS, RET_V_WIDTH), BF16),
        scratch_shapes=[scratch(RET_QK_WIDTH), scratch(RET_QK_WIDTH), scratch(RET_V_WIDTH),
                        scratch(RET_V_WIDTH), pltpu.VMEM((RET_HEADS, RET_QK_DIM, RET_V_DIM), F32)],
        compiler_params=_params(("parallel", "arbitrary")),
        name="retention",
    )(g_chunk, x, pos, g, invf, w, dmask, xi, zeta, gn)


def _ple(h, p_ref, np_ref, wpu_ref, wpg_ref):
    gate = jax.nn.sigmoid(_dot(_rms(h, np_ref[...]).astype(BF16), wpg_ref[...]))
    return h + _dot(p_ref[...].astype(BF16), wpu_ref[...]) * gate


def _post_a_kernel(h_ref, og_ref, p_ref, wo_ref, nf_ref, wg_ref, wu_ref, wd_ref, np_ref, wpu_ref,
                   wpg_ref, out_ref):
    h1 = h_ref[...] + _dot(og_ref[...], wo_ref[...])
    xn = _rms(h1, nf_ref[...]).astype(BF16)
    a = (_silu(_dot(xn, wg_ref[...])) * _dot(xn, wu_ref[...])).astype(BF16)
    h2 = h1 + _dot(a, wd_ref[...])
    out_ref[...] = _ple(h2, p_ref, np_ref, wpu_ref, wpg_ref)


def _post_a(h, og, p, wo, nf, wg, wu, wd, npl, wpu, wpg, tm=512):
    n = TOKENS // tm
    tile = lambda width: pl.BlockSpec((tm, width), lambda i: (i, 0))
    consts = [wo, nf, wg, wu, wd, npl, wpu, wpg]
    return pl.pallas_call(
        _post_a_kernel,
        grid=(n,),
        in_specs=[tile(D_MODEL), tile(RET_V_WIDTH), pl.BlockSpec((None, tm, PLE_DIM), lambda i: (0, i, 0))]
                 + [_const_spec(c.shape) for c in consts],
        out_specs=tile(D_MODEL),
        out_shape=jax.ShapeDtypeStruct((TOKENS, D_MODEL), F32),
        compiler_params=_params(("parallel",)),
        name="post_a",
    )(h, og, p, *consts)


def _qkv_kernel(h_ref, gq_ref, gkv_ref, wq_ref, wkv_ref, qt_ref, k_ref, vt_ref):
    h = h_ref[...]
    hn = h * lax.rsqrt(jnp.mean(h * h, axis=-1, keepdims=True) + EPS)
    q = _dot((hn * gq_ref[...]).astype(BF16), wq_ref[...]) * (ATT_HEAD_DIM ** -0.5 * LOG2E)
    qt_ref[...] = q.T.astype(BF16)
    kv = _dot((hn * gkv_ref[...]).astype(BF16), wkv_ref[...])
    keep = (pl.program_id(1) > 0).astype(F32)
    k_ref[...] = (kv[:, :D_MODEL] * keep).astype(BF16)
    vt_ref[...] = (kv[:, D_MODEL:] * keep).T.astype(BF16)


def _qkv(h, gq, gkv, wq, wkv):
    tm = LEFT_PAD
    nb = SEQ // tm
    src = lambda b, j: (b * nb + jnp.maximum(j - 1, 0), 0)
    return pl.pallas_call(
        _qkv_kernel,
        grid=(BATCH, nb + 1),
        in_specs=[pl.BlockSpec((tm, D_MODEL), src), _const_spec((1, D_MODEL)), _const_spec((1, D_MODEL)),
                  _const_spec(wq.shape), _const_spec(wkv.shape)],
        out_specs=[pl.BlockSpec((D_MODEL, tm), lambda b, j: (0, b * nb + jnp.maximum(j - 1, 0))),
                   pl.BlockSpec((None, tm, D_MODEL), lambda b, j: (b, j, 0)),
                   pl.BlockSpec((None, D_MODEL, tm), lambda b, j: (b, 0, j))],
        out_shape=[jax.ShapeDtypeStruct((D_MODEL, TOKENS), BF16),
                   jax.ShapeDtypeStruct((BATCH, SEQ + LEFT_PAD, D_MODEL), BF16),
                   jax.ShapeDtypeStruct((BATCH, D_MODEL, SEQ + LEFT_PAD), BF16)],
        compiler_params=_params(("parallel", "arbitrary")),
        name="qkv_b",
    )(h, gq, gkv, wq, wkv)


def _attn_kernel(rel_ref, qt_ref, k_ref, vt_ref, o_ref, bias_ref, s0_ref, s1_ref, p0_ref, p1_ref):
    n = pl.program_id(2)

    @pl.when(n == 0)
    def _():
        kj = lax.broadcasted_iota(jnp.int32, (ATT_KB, ATT_QB), 0)
        qi = lax.broadcasted_iota(jnp.int32, (ATT_KB, ATT_QB), 1)
        k_chunk = kj // CHUNK - LEFT_CHUNKS
        q_chunk = qi // CHUNK
        allowed = (k_chunk <= q_chunk) & (k_chunk >= q_chunk - LEFT_CHUNKS)
        for g in range(ATT_HG):
            base = jnp.broadcast_to(rel_ref[g], (ATT_KB, ATT_REL_WIDTH))
            rolled = pltpu.roll(base, 0, 1, stride=1, stride_axis=0)
            table = jnp.where(allowed, rolled[:, :ATT_QB], NEG_BIG)
            bias_ref[0, g] = table
            for v in range(1, ATT_PAD_BLOCKS + 1):
                bias_ref[v, g] = jnp.where(kj >= LEFT_PAD - (v - 1) * ATT_QB, table, NEG_BIG)
        p0_ref[...] = jnp.zeros_like(p0_ref)
        p1_ref[...] = jnp.zeros_like(p1_ref)

    head_of_row = lax.broadcasted_iota(jnp.int32, (MXU_DIM, ATT_QB), 0) // ATT_HEAD_DIM
    n_blocks = ATT_QS // ATT_QB

    def window(qb):
        qb = jnp.minimum(qb, n_blocks - 1)
        q0 = pl.multiple_of(qb * ATT_QB, ATT_QB)
        kstart = pl.multiple_of(n * ATT_QS + qb * ATT_QB, ATT_QB)
        return q0, kstart

    def scores(qb, g, s_ref):
        q0, kstart = window(qb)
        qt = qt_ref[:, pl.ds(q0, ATT_QB)]
        kb = k_ref[pl.ds(kstart, ATT_KB), :]
        qg = jnp.where(head_of_row == g, qt, jnp.zeros_like(qt))
        block_in_seq = kstart // ATT_QB
        variant = jnp.where(block_in_seq < ATT_PAD_BLOCKS, block_in_seq + 1, 0)
        s = _dot(kb, qg)
        maxes = []
        for rows, lanes in ATT_LIVE:
            part = s[rows, lanes] + bias_ref[variant, g, rows, lanes]
            s_ref[rows, lanes] = part
            maxes.append(jnp.max(part, axis=0, keepdims=True))
        return jnp.concatenate(maxes, axis=1)

    def probabilities(s_ref, p_ref, m):
        for rows, lanes in ATT_LIVE:
            p_ref[rows, lanes] = jnp.exp2(s_ref[rows, lanes] - m[:, lanes]).astype(BF16)

    ones_rows = jnp.ones((2 * SUBLANES, ATT_KB), BF16)

    def values(qb, g, p_ref):
        _, kstart = window(qb)
        vg = vt_ref[g * ATT_HEAD_DIM:(g + 1) * ATT_HEAD_DIM, pl.ds(kstart, ATT_KB)]
        pv = _dot(jnp.concatenate([vg, ones_rows], axis=0), p_ref[...])
        return pv[:ATT_HEAD_DIM] / pv[ATT_HEAD_DIM:ATT_HEAD_DIM + 1]

    s_bufs = (s0_ref, s1_ref)
    p_bufs = (p0_ref, p1_ref)
    m_first = scores(0, 0, s_bufs[0])
    m_second = scores(0, 1, s_bufs[1])
    probabilities(s_bufs[0], p_bufs[0], m_first)

    def body(qb, m_next):
        outs = []
        for g in range(ATT_HG):
            slot = g % 2
            outs.append(values(qb, g, p_bufs[slot]))
            probabilities(s_bufs[1 - slot], p_bufs[1 - slot], m_next)
            ahead = g + 2
            m_next = scores(qb + ahead // ATT_HG, ahead % ATT_HG, s_bufs[slot])
        q0, _ = window(qb)
        o_ref[pl.ds(q0, ATT_QB), :] = jnp.concatenate(outs, axis=0).T.astype(BF16)
        return m_next

    lax.fori_loop(0, n_blocks, body, m_second)


def _rel_vectors(rel_table):
    heads, n_rel = rel_table.shape
    far = rel_table[:, n_rel - 1:]
    near = rel_table[:, :1]
    n_far_front = ATT_QB + 1
    n_near = ATT_REL_WIDTH - LEFT_PAD - (CHUNK - 1) - n_far_front
    n_far_back = ATT_REL_WIDTH - n_far_front - n_near - n_rel
    g = jnp.concatenate([jnp.broadcast_to(far, (heads, n_far_front)),
                         jnp.broadcast_to(near, (heads, n_near)),
                         rel_table,
                         jnp.broadcast_to(far, (heads, n_far_back))], axis=1)
    return (g.astype(F32) * LOG2E).reshape(heads, 1, ATT_REL_WIDTH)


def _band_attention(qt, kp, vt, rel):
    ns = SEQ // ATT_QS
    return pl.pallas_call(
        _attn_kernel,
        grid=(BATCH, ATT_HEADS // ATT_HG, ns),
        in_specs=[pl.BlockSpec((ATT_HG, 1, ATT_REL_WIDTH), lambda b, g, n: (g, 0, 0)),
                  pl.BlockSpec((MXU_DIM, ATT_QS), lambda b, g, n: (g, b * ns + n)),
                  pl.BlockSpec((None, SEQ + LEFT_PAD, MXU_DIM), lambda b, g, n: (b, 0, g)),
                  pl.BlockSpec((None, MXU_DIM, SEQ + LEFT_PAD), lambda b, g, n: (b, g, 0))],
        out_specs=pl.BlockSpec((ATT_QS, MXU_DIM), lambda b, g, n: (b * ns + n, g)),
        out_shape=jax.ShapeDtypeStruct((TOKENS, D_MODEL), BF16),
        scratch_shapes=[pltpu.VMEM((ATT_PAD_BLOCKS + 1, ATT_HG, ATT_KB, ATT_QB), F32),
                        pltpu.VMEM((ATT_KB, ATT_QB), F32), pltpu.VMEM((ATT_KB, ATT_QB), F32),
                        pltpu.VMEM((ATT_KB, ATT_QB), BF16), pltpu.VMEM((ATT_KB, ATT_QB), BF16)],
        compiler_params=_params(("parallel", "parallel", "arbitrary")),
        name="band_attn",
    )(rel, qt, kp, vt)


def _row_copy(src, src_row, dst, dst_row, sem):
    return pltpu.make_async_copy(src.at[pl.ds(src_row, 1)], dst.at[pl.ds(dst_row, 1)], sem)


def _tile_wait(hbm, vmem_tile, sem):
    pltpu.make_async_copy(hbm.at[pl.ds(0, MOE_TM)], vmem_tile, sem).wait()


def _scatter_tile(xbuf, rs_smem, xs_hbm, sem):
    for t in range(MOE_TM):
        for which in range(TOP_K):
            _row_copy(xbuf.at[which], t, xs_hbm, rs_smem[which, t], sem).start(priority=which)


def _fill_dispatch_rows(xbuf, xn, first_token):
    token = first_token + lax.broadcasted_iota(jnp.int32, (MOE_TM, LANES), 0)
    for which in range(TOP_K):
        xbuf[which, :, :D_MODEL] = xn
        xbuf[which, :, D_MODEL:] = (which * TOKENS + token).astype(F32)


def _route_tile(rows, h_ref, o_ref, wo_ref, nf_ref, wrh_ref, wrl_ref, tri_ref,
                h_out_ref, route_ref, gw_ref, cnt_ref, carry_ref):
    cols = rows
    h = h_ref[rows, :] + _dot(o_ref[rows, :], wo_ref[...])
    h_out_ref[rows, :] = h
    xn = _rms(h, nf_ref[...])

    xh = xn.astype(BF16)
    xl = (xn - xh.astype(F32)).astype(BF16)
    logits = _dot(xh, wrh_ref[...]) + _dot(xl, wrh_ref[...]) + _dot(xh, wrl_ref[...])
    lt = logits.T[:N_EXPERTS]
    row = lax.broadcasted_iota(jnp.int32, lt.shape, 0)
    m1 = jnp.max(lt, axis=0, keepdims=True)
    i1 = jnp.min(jnp.where(lt == m1, row, N_EXPERTS), axis=0, keepdims=True)
    rest = jnp.where(row == i1, -jnp.inf, lt)
    m2 = jnp.max(rest, axis=0, keepdims=True)
    i2 = jnp.min(jnp.where(rest == m2, row, N_EXPERTS), axis=0, keepdims=True)
    e2 = jnp.exp(m2 - m1)
    w1 = 1.0 / (1.0 + e2)
    w2 = e2 / (1.0 + e2)

    member = ((row == i1) | (row == i2)).astype(F32)
    within = _dot(member.astype(BF16), tri_ref[...])
    carry = carry_ref[...]
    rank = within + carry[:, 0:1]
    r1 = jnp.sum(jnp.where(row == i1, rank, 0.0), axis=0, keepdims=True).astype(jnp.int32)
    r2 = jnp.sum(jnp.where(row == i2, rank, 0.0), axis=0, keepdims=True).astype(jnp.int32)
    carry_ref[...] = carry + jnp.sum(member, axis=1, keepdims=True)
    cnt_ref[...] = carry_ref[...].astype(jnp.int32)

    s1 = i1 * MOE_CAP + r1
    s2 = i2 * MOE_CAP + r2
    route_ref[:, cols] = jnp.where(row == 0, s1, jnp.where(row == 1, s2, 0))
    gwt = jnp.where(row == 0, w1, jnp.where(row == 1, w2, 0.0))
    pad = jnp.zeros((LANES - N_EXPERTS, gwt.shape[1]), F32)
    gw_ref[rows, :] = jnp.concatenate([gwt, pad], axis=0).T
    return xn


def _post_b_kernel(h_ref, o_ref, wo_ref, nf_ref, wrh_ref, wrl_ref, tri_ref,
                   h_out_ref, route_ref, gw_ref, cnt_ref, xs_hbm,
                   xbuf0, xbuf1, carry_ref, rs0, rs1, cnt_smem, zero_ref, sems, tail_sem):
    g = pl.program_id(0)
    tile_args = (h_ref, o_ref, wo_ref, nf_ref, wrh_ref, wrl_ref, tri_ref,
                 h_out_ref, route_ref, gw_ref, cnt_ref, carry_ref)
    first = slice(0, MOE_TM)
    second = slice(MOE_TM, 2 * MOE_TM)

    @pl.when(g == 0)
    def _():
        carry_ref[...] = jnp.zeros_like(carry_ref)
        xbuf1[...] = jnp.zeros_like(xbuf1)
        which = lax.broadcasted_iota(jnp.int32, (SUBLANES, MOE_TM), 0)
        token = lax.broadcasted_iota(jnp.int32, (SUBLANES, MOE_TM), 1)
        route_ref[:, first] = MOE_SPARE_ROW + jnp.minimum(which, TOP_K - 1) * MOE_TM + token
        pltpu.sync_copy(route_ref.at[:, first], rs1)

    _scatter_tile(xbuf1, rs1, xs_hbm, sems.at[1])
    xn = _route_tile(first, *tile_args)

    @pl.when(g >= 1)
    def _():
        for which in range(TOP_K):
            _tile_wait(xs_hbm, xbuf0.at[which], sems.at[0])

    _fill_dispatch_rows(xbuf0, xn, 2 * g * MOE_TM)
    pltpu.sync_copy(route_ref.at[:, first], rs0)

    _scatter_tile(xbuf0, rs0, xs_hbm, sems.at[0])
    xn = _route_tile(second, *tile_args)
    for which in range(TOP_K):
        _tile_wait(xs_hbm, xbuf1.at[which], sems.at[1])
    _fill_dispatch_rows(xbuf1, xn, (2 * g + 1) * MOE_TM)
    pltpu.sync_copy(route_ref.at[:, second], rs1)

    @pl.when(g == pl.num_programs(0) - 1)
    def _():
        _scatter_tile(xbuf1, rs1, xs_hbm, sems.at[1])
        for buf, sem in ((xbuf0, sems.at[0]), (xbuf1, sems.at[1])):
            for which in range(TOP_K):
                _tile_wait(xs_hbm, buf.at[which], sem)

        pltpu.sync_copy(cnt_ref, cnt_smem)
        n_parts = MOE_SUPER // MOE_TM
        spare = lax.broadcasted_iota(jnp.int32, (MOE_TM, LANES), 0) + MOE_DUMP_ROW
        for part in range(n_parts + 1):
            zero_ref[part, :, :D_MODEL] = jnp.zeros((MOE_TM, D_MODEL), F32)
            zero_ref[part, :, D_MODEL:] = (spare + part * MOE_TM).astype(F32)
        for e in range(N_EXPERTS):
            count = cnt_smem[e, 0]
            aligned = ((count + (SUBLANES - 1)) // SUBLANES) * SUBLANES
            for part in range(n_parts):
                start = pl.multiple_of(e * MOE_CAP + aligned + part * MOE_TM, SUBLANES)
                pltpu.make_async_copy(zero_ref.at[part], xs_hbm.at[pl.ds(start, MOE_TM)], tail_sem).start()
            for k in range(SUBLANES - 1):
                @pl.when(count + k < aligned)
                def _():
                    _row_copy(zero_ref.at[n_parts], k, xs_hbm, e * MOE_CAP + count + k, tail_sem).start()
        for e in range(N_EXPERTS):
            count = cnt_smem[e, 0]
            aligned = ((count + (SUBLANES - 1)) // SUBLANES) * SUBLANES
            for part in range(n_parts):
                _tile_wait(xs_hbm, zero_ref.at[part], tail_sem)
            for k in range(SUBLANES - 1):
                @pl.when(count + k < aligned)
                def _():
                    _row_copy(zero_ref.at[n_parts], k, xs_hbm, 0, tail_sem).wait()


def _post_b(h, o, wo, nf, wr_hi, wr_lo):
    tm = MOE_TM
    pair = 2 * tm
    n = TOKENS // pair
    tile = lambda width: pl.BlockSpec((pair, width), lambda i: (i, 0))
    tri = jnp.asarray(np.triu(np.ones((tm, tm), np.float32), k=1), dtype=BF16)
    return pl.pallas_call(
        _post_b_kernel,
        grid=(n,),
        in_specs=[tile(D_MODEL), tile(D_MODEL), _const_spec(wo.shape), _const_spec((1, D_MODEL)),
                  _const_spec(wr_hi.shape), _const_spec(wr_lo.shape), _const_spec(tri.shape)],
        out_specs=[tile(D_MODEL),
                   pl.BlockSpec((SUBLANES, pair), lambda i: (0, i)),
                   tile(LANES),
                   pl.BlockSpec((SUBLANES, LANES), lambda i: (0, 0)),
                   pl.BlockSpec(memory_space=pl.ANY)],
        out_shape=[jax.ShapeDtypeStruct((TOKENS, D_MODEL), F32),
                   jax.ShapeDtypeStruct((SUBLANES, TOKENS), jnp.int32),
                   jax.ShapeDtypeStruct((TOKENS, LANES), F32),
                   jax.ShapeDtypeStruct((SUBLANES, LANES), jnp.int32),
                   jax.ShapeDtypeStruct((MOE_ROWS, MOE_ROW_WIDTH), F32)],
        scratch_shapes=[pltpu.VMEM((TOP_K, tm, MOE_ROW_WIDTH), F32),
                        pltpu.VMEM((TOP_K, tm, MOE_ROW_WIDTH), F32),
                        pltpu.VMEM((SUBLANES, LANES), F32),
                        pltpu.SMEM((SUBLANES, tm), jnp.int32),
                        pltpu.SMEM((SUBLANES, tm), jnp.int32),
                        pltpu.SMEM((SUBLANES, LANES), jnp.int32),
                        pltpu.VMEM((MOE_SUPER // MOE_TM + 1, tm, MOE_ROW_WIDTH), F32),
                        pltpu.SemaphoreType.DMA((2,)),
                        pltpu.SemaphoreType.DMA(())],
        compiler_params=_params(("arbitrary",)),
        name="post_b",
    )(h, o, wo, nf, wr_hi, wr_lo, tri)


def _moe_kernel(blk_ref, exp_ref, halves_ref, nact_ref, x_ref, wg_ref, wu_ref, wd_ref, out_hbm,
                xb_ref, acc_ref, done_ref, ids_ref, ids_smem, sem):
    j = pl.program_id(0)
    f = pl.program_id(1)
    nf = pl.num_programs(1)
    nact = nact_ref[0]
    first = slice(0, MOE_TM)
    second = slice(MOE_TM, MOE_SUPER)
    share = MOE_SUPER // MOE_NF

    def half(rows):
        x = xb_ref[rows, :]
        a = (_silu(_dot(x, wg_ref[...])) * _dot(x, wu_ref[...])).astype(BF16)
        acc_ref[rows, :] += _dot(a, wd_ref[...])

    @pl.when((j == 0) & (f == 0))
    def _():
        done_ref[...] = jnp.zeros_like(done_ref)
        ids_ref[...] = MOE_DUMP_ROW + lax.broadcasted_iota(jnp.int32, ids_ref.shape, 1)
        pltpu.sync_copy(ids_ref, ids_smem)

    @pl.when(j <= nact)
    def _():
        @pl.when(f == 0)
        def _():
            xb_ref[...] = x_ref[:, :D_MODEL].astype(BF16)
            acc_ref[...] = jnp.zeros_like(acc_ref)
            ids = x_ref[:, D_MODEL:].T[0:1].astype(jnp.int32)
            ids_ref[...] = jnp.broadcast_to(ids, ids_ref.shape)

        base = f * share
        for r in range(share):
            _row_copy(done_ref, base + r, out_hbm, ids_smem[0, base + r], sem).start()
        half(first)

        @pl.when(halves_ref[j] == 2)
        def _():
            half(second)

        @pl.when(f == nf - 1)
        def _():
            for rows in (first, second):
                _tile_wait(out_hbm, done_ref.at[rows], sem)
            done_ref[...] = acc_ref[...]
            pltpu.sync_copy(ids_ref, ids_smem)


def _moe(xs, tile_block, tile_expert, tile_halves, n_active, wg, wu, wd):
    nf = MOE_NF
    tf = D_FF_EXPERT // nf
    fidx = lambda j, f, nact: jnp.where(j < nact[0], f, nf - 1)
    grid_spec = pltpu.PrefetchScalarGridSpec(
        num_scalar_prefetch=4,
        grid=(MOE_GRID_TILES, nf),
        in_specs=[pl.BlockSpec((MOE_SUPER, MOE_ROW_WIDTH), lambda j, f, blk, exp, hv, nact: (blk[j], 0)),
                  pl.BlockSpec((None, D_MODEL, tf), lambda j, f, blk, exp, hv, nact: (exp[j], 0, fidx(j, f, nact))),
                  pl.BlockSpec((None, D_MODEL, tf), lambda j, f, blk, exp, hv, nact: (exp[j], 0, fidx(j, f, nact))),
                  pl.BlockSpec((None, tf, D_MODEL), lambda j, f, blk, exp, hv, nact: (exp[j], fidx(j, f, nact), 0))],
        out_specs=pl.BlockSpec(memory_space=pl.ANY),
        scratch_shapes=[pltpu.VMEM((MOE_SUPER, D_MODEL), BF16),
                        pltpu.VMEM((MOE_SUPER, D_MODEL), F32),
                        pltpu.VMEM((MOE_SUPER, D_MODEL), F32),
                        pltpu.VMEM((SUBLANES, MOE_SUPER), jnp.int32),
                        pltpu.SMEM((SUBLANES, MOE_SUPER), jnp.int32),
                        pltpu.SemaphoreType.DMA(())],
    )
    return pl.pallas_call(
        _moe_kernel,
        grid_spec=grid_spec,
        out_shape=jax.ShapeDtypeStruct((MOE_OUT_ROWS, D_MODEL), F32),
        compiler_params=_params(("arbitrary", "arbitrary")),
        name="moe",
    )(tile_block, tile_expert, tile_halves, n_active, xs, wg, wu, wd)


def _tile_plan(counts):
    experts = jnp.arange(N_EXPERTS)[None, :]
    halves = (counts + (MOE_TM - 1)) // MOE_TM
    tiles = (halves + 1) // 2
    ends = jnp.cumsum(tiles)
    n_active = ends[-1]
    j = jnp.minimum(jnp.arange(MOE_GRID_TILES, dtype=jnp.int32), n_active - 1)
    expert = jnp.sum((j[:, None] >= ends[None, :]).astype(jnp.int32), axis=1)
    pick = lambda per_expert: jnp.sum(jnp.where(expert[:, None] == experts, per_expert[None, :], 0), axis=1)
    k = j - pick(ends - tiles)
    block = expert * MOE_REGION_TILES + k
    tile_halves = jnp.minimum(pick(halves) - 2 * k, 2)
    i32 = lambda a: a.astype(jnp.int32)
    return i32(block), i32(expert), i32(tile_halves), i32(n_active.reshape(1))


def _final_kernel(h_ref, p_ref, gw_ref, y0_ref, y1_ref, np_ref, wpu_ref, wpg_ref, nfin_ref, out_ref):
    gw = gw_ref[...]
    h = h_ref[...] + gw[:, 0:1] * y0_ref[...] + gw[:, 1:2] * y1_ref[...]
    out_ref[...] = _rms(_ple(h, p_ref, np_ref, wpu_ref, wpg_ref), nfin_ref[...])


def _final(h, p, gw, ys, npl, wpu, wpg, nfin, tm=512):
    n = TOKENS // tm
    tile = lambda width: pl.BlockSpec((tm, width), lambda i: (i, 0))
    consts = [npl, wpu, wpg, nfin]
    return pl.pallas_call(
        _final_kernel,
        grid=(n,),
        in_specs=[tile(D_MODEL), pl.BlockSpec((None, tm, PLE_DIM), lambda i: (1, i, 0)), tile(LANES),
                  tile(D_MODEL), pl.BlockSpec((tm, D_MODEL), lambda i: (n + i, 0))]
                 + [_const_spec(c.shape) for c in consts],
        out_specs=tile(D_MODEL),
        out_shape=jax.ShapeDtypeStruct((TOKENS, D_MODEL), F32),
        compiler_params=_params(("parallel",)),
        name="final",
    )(h, p, gw, ys, ys, *consts)


def kernel(x, p, positions, norm_mix, norm_ffn, norm_ple, w_in_a, ret_gn, w_out_a, norm_kv, w_kv, w_q_b, rel_bias, w_out_b, w_gate_dense, w_up_dense, w_down_dense, w_router, w_gate_moe, w_up_moe, w_down_moe, w_ple_up, w_ple_gate, norm_final):
    bf = lambda w: w.astype(BF16)
    row = lambda g: g.reshape(1, -1).astype(F32)
    h0 = x.reshape(TOKENS, D_MODEL)
    p2 = p.reshape(2, TOKENS, PLE_DIM)
    pos = positions.reshape(TOKENS, 1)
    inv_freq = (1.0 / (ROPE_BASE ** jnp.linspace(0.0, 1.0, ROPE_HALF, dtype=F32))).reshape(1, ROPE_HALF)

    og = _retention_layer(h0, pos, row(norm_mix[0]), inv_freq, bf(w_in_a[0]), row(ret_gn[0]))
    h1 = _post_a(h0, og, p2, bf(w_out_a[0]), row(norm_ffn[0]), bf(w_gate_dense[0]), bf(w_up_dense[0]),
                 bf(w_down_dense[0]), row(norm_ple[0]), bf(w_ple_up[0]), bf(w_ple_gate[0]))

    qt, kp, vt = _qkv(h1, row(norm_mix[1]), row(norm_kv), bf(w_q_b[0]), bf(w_kv))
    ob = _band_attention(qt, kp, vt, _rel_vectors(rel_bias[0]))
    w_router_pad = jnp.zeros((D_MODEL, LANES), F32).at[:, :N_EXPERTS].set(w_router[0])
    wr_hi = bf(w_router_pad)
    wr_lo = bf(w_router_pad - wr_hi.astype(F32))
    h2, route, gw, counts, xs = _post_b(h1, ob, bf(w_out_b[0]), row(norm_ffn[1]), wr_hi, wr_lo)
    tile_block, tile_expert, tile_halves, n_active = _tile_plan(counts[:, 0])
    ys = _moe(xs, tile_block, tile_expert, tile_halves, n_active,
              bf(w_gate_moe[0]), bf(w_up_moe[0]), bf(w_down_moe[0]))
    out = _final(h2, p2, gw, ys, row(norm_ple[1]), bf(w_ple_up[1]), bf(w_ple_gate[1]), row(norm_final))
    return out.reshape(BATCH, SEQ, D_MODEL)
```

```python
import functools

import numpy as np
import jax
import jax.numpy as jnp
from jax import lax
from jax.experimental import pallas as pl
from jax.experimental.pallas import tpu as pltpu

F32 = jnp.float32
BF16 = jnp.bfloat16

D_MODEL = 1024
BATCH = 2
SEQ = 8192
TOKENS = BATCH * SEQ
CHUNK = 64
PLE_DIM = 256

RET_HEADS = 4
RET_QK_DIM = 256
RET_V_DIM = 512
RET_QK_WIDTH = RET_HEADS * RET_QK_DIM
RET_V_WIDTH = RET_HEADS * RET_V_DIM
ROPE_BASE = 10000.0
ROPE_HALF = RET_QK_DIM // 2

ATT_HEADS = 16
ATT_HEAD_DIM = 64
LEFT_CHUNKS = 8
LEFT_PAD = LEFT_CHUNKS * CHUNK
REL_CLIP = 256

D_FF_DENSE = 2816
N_EXPERTS = 8
D_FF_EXPERT = 3584
EPS = 1e-6

LANES = 128
MXU_DIM = 256
VMEM_LIMIT = 56 * 1024 * 1024

RET_SUB = MXU_DIM
RET_BLOCK = 512
ATT_QB = MXU_DIM
ATT_KB = ATT_QB + LEFT_PAD
ATT_QS = 4096
ATT_REL_WIDTH = 1024
ATT_PAD_BLOCKS = LEFT_PAD // ATT_QB
ATT_LIVE = tuple((slice(LANES * t, LANES * t + LEFT_PAD + LANES), slice(LANES * t, LANES * (t + 1)))
                 for t in range(ATT_QB // LANES))
LOG2E = 1.4426950408889634
ATT_HG = MXU_DIM // ATT_HEAD_DIM
NEG_BIG = -1e30

TOP_K = 2
MOE_TM = 512
MOE_SUPER = 2 * MOE_TM
MOE_CAP = TOKENS + 2 * MOE_SUPER
MOE_REGION_TILES = MOE_CAP // MOE_SUPER
MOE_MAX_TILES = TOP_K * TOKENS // MOE_SUPER + N_EXPERTS
MOE_SPARE_ROW = N_EXPERTS * MOE_CAP
MOE_ROWS = MOE_SPARE_ROW + TOP_K * MOE_TM
SUBLANES = 8


def _dot(a, b):
    return jnp.dot(a, b, preferred_element_type=F32)


def _rms(x, g):
    return x * lax.rsqrt(jnp.mean(x * x, axis=-1, keepdims=True) + EPS) * g


def _silu(x):
    return x * jax.nn.sigmoid(x)


def _const_spec(shape):
    nd = len(shape)
    return pl.BlockSpec(shape, lambda *_: (0,) * nd, pipeline_mode=pl.Buffered(1))


def _params(sem):
    return pltpu.CompilerParams(dimension_semantics=sem, vmem_limit_bytes=VMEM_LIMIT)


def _inproj_tile(x_ref, pos_ref, g_ref, invf_ref, w_ref, q_ref, k_ref, v_ref, gate_ref):
    xn = _rms(x_ref[...], g_ref[...]).astype(BF16)
    v0 = 2 * RET_QK_WIDTH
    g0 = v0 + RET_V_WIDTH
    for c in range(RET_HEADS):
        lo, hi = c * RET_V_DIM, (c + 1) * RET_V_DIM
        v_ref[:, lo:hi] = _dot(xn, w_ref[:, v0 + lo:v0 + hi]).astype(BF16)
        gate_ref[:, lo:hi] = _silu(_dot(xn, w_ref[:, g0 + lo:g0 + hi])).astype(BF16)
    ang = pos_ref[...].astype(F32) * invf_ref[...]
    cos = jnp.cos(ang)
    sin = jnp.sin(ang)
    k_scale = RET_QK_DIM ** -0.5
    for h in range(RET_HEADS):
        lo = h * RET_QK_DIM
        mid = lo + ROPE_HALF
        hi = lo + RET_QK_DIM
        pq = _dot(xn, w_ref[:, lo:hi])
        x1, x2 = pq[:, :ROPE_HALF], pq[:, ROPE_HALF:]
        q_ref[:, lo:mid] = (x1 * cos - x2 * sin).astype(BF16)
        q_ref[:, mid:hi] = (x1 * sin + x2 * cos).astype(BF16)
        pk = _dot(xn, w_ref[:, RET_QK_WIDTH + lo:RET_QK_WIDTH + hi])
        x1, x2 = pk[:, :ROPE_HALF], pk[:, ROPE_HALF:]
        k_ref[:, lo:mid] = ((x1 * cos - x2 * sin) * k_scale).astype(BF16)
        k_ref[:, mid:hi] = ((x1 * sin + x2 * cos) * k_scale).astype(BF16)


def _retention_kernel(gc_ref, x_ref, pos_ref, g_ref, invf_ref, w_ref, dm_ref, xi_ref, zeta_ref, gn_ref,
                      o_ref, q_ref, k_ref, v_ref, gate_ref, state_ref):
    @pl.when(pl.program_id(1) == 0)
    def _():
        state_ref[...] = jnp.zeros_like(state_ref)

    _inproj_tile(x_ref, pos_ref, g_ref, invf_ref, w_ref, q_ref, k_ref, v_ref, gate_ref)

    for h in range(RET_HEADS):
        qk = slice(h * RET_QK_DIM, (h + 1) * RET_QK_DIM)
        vd = slice(h * RET_V_DIM, (h + 1) * RET_V_DIM)
        g_chunk = gc_ref[h]
        dm = dm_ref[h]
        xi = xi_ref[h]
        zeta = zeta_ref[h]
        gn = gn_ref[:, vd]
        for j in range(RET_BLOCK // RET_SUB):
            sl = slice(j * RET_SUB, (j + 1) * RET_SUB)
            qj = q_ref[sl, qk]
            kj = k_ref[sl, qk]
            vj = v_ref[sl, vd]
            st = state_ref[h]
            s = lax.dot_general(qj, kj, (((1,), (1,)), ((), ())), preferred_element_type=F32) * dm
            inner = _dot(s.astype(BF16), vj)
            cross = _dot(qj, st.astype(BF16)) * xi
            kz = (kj.astype(F32) * zeta).astype(BF16)
            upd = lax.dot_general(kz, vj, (((0,), (0,)), ((), ())), preferred_element_type=F32)
            state_ref[h] = st * g_chunk + upd
            o = inner + cross
            on = _rms(o, gn)
            o_ref[sl, vd] = (on * gate_ref[sl, vd].astype(F32)).astype(BF16)


def _retention_layer(x, pos, g, invf, w, gn):
    lg = jnp.log1p(-jnp.exp2(-5.0 - jnp.arange(RET_HEADS, dtype=F32)))
    idx = jnp.arange(RET_SUB, dtype=F32)
    diff = idx[:, None] - idx[None, :]
    causal = diff >= 0
    dmask = jnp.where(causal, jnp.exp(jnp.where(causal, diff, 0.0)[None] * lg[:, None, None]), 0.0)
    xi = jnp.exp((idx[None, :] + 1.0) * lg[:, None])[:, :, None]
    zeta = jnp.exp((RET_SUB - 1.0 - idx)[None, :] * lg[:, None])[:, :, None]
    g_chunk = jnp.exp(RET_SUB * lg)
    nb = SEQ // RET_BLOCK
    tile = lambda width: pl.BlockSpec((RET_BLOCK, width), lambda b, n: (b * nb + n, 0))
    scratch = lambda width: pltpu.VMEM((RET_BLOCK, width), BF16)
    return pl.pallas_call(
        _retention_kernel,
        grid=(BATCH, nb),
        in_specs=[pl.BlockSpec(memory_space=pltpu.SMEM),
                  tile(D_MODEL), tile(1), _const_spec((1, D_MODEL)), _const_spec((1, ROPE_HALF)),
                  _const_spec(w.shape), _const_spec(dmask.shape), _const_spec(xi.shape),
                  _const_spec(zeta.shape), _const_spec(gn.shape)],
        out_specs=tile(RET_V_WIDTH),
        out_shape=jax.ShapeDtypeStruct((TOKENS, RET_V_WIDTH), BF16),
        scratch_shapes=[scratch(RET_QK_WIDTH), scratch(RET_QK_WIDTH), scratch(RET_V_WIDTH),
                        scratch(RET_V_WIDTH), pltpu.VMEM((RET_HEADS, RET_QK_DIM, RET_V_DIM), F32)],
        compiler_params=_params(("parallel", "arbitrary")),
        name="retention",
    )(g_chunk, x, pos, g, invf, w, dmask, xi, zeta, gn)


def _ple(h, p_ref, np_ref, wpu_ref, wpg_ref):
    gate = jax.nn.sigmoid(_dot(_rms(h, np_ref[...]).astype(BF16), wpg_ref[...]))
    return h + _dot(p_ref[...].astype(BF16), wpu_ref[...]) * gate


def _post_a_kernel(h_ref, og_ref, p_ref, wo_ref, nf_ref, wg_ref, wu_ref, wd_ref, np_ref, wpu_ref,
                   wpg_ref, out_ref):
    h1 = h_ref[...] + _dot(og_ref[...], wo_ref[...])
    xn = _rms(h1, nf_ref[...]).astype(BF16)
    a = (_silu(_dot(xn, wg_ref[...])) * _dot(xn, wu_ref[...])).astype(BF16)
    h2 = h1 + _dot(a, wd_ref[...])
    out_ref[...] = _ple(h2, p_ref, np_ref, wpu_ref, wpg_ref)


def _post_a(h, og, p, wo, nf, wg, wu, wd, npl, wpu, wpg, tm=512):
    n = TOKENS // tm
    tile = lambda width: pl.BlockSpec((tm, width), lambda i: (i, 0))
    consts = [wo, nf, wg, wu, wd, npl, wpu, wpg]
    return pl.pallas_call(
        _post_a_kernel,
        grid=(n,),
        in_specs=[tile(D_MODEL), tile(RET_V_WIDTH), pl.BlockSpec((None, tm, PLE_DIM), lambda i: (0, i, 0))]
                 + [_const_spec(c.shape) for c in consts],
        out_specs=tile(D_MODEL),
        out_shape=jax.ShapeDtypeStruct((TOKENS, D_MODEL), F32),
        compiler_params=_params(("parallel",)),
        name="post_a",
    )(h, og, p, *consts)


def _qkv_kernel(h_ref, gq_ref, gkv_ref, wq_ref, wkv_ref, qt_ref, k_ref, vt_ref):
    h = h_ref[...]
    hn = h * lax.rsqrt(jnp.mean(h * h, axis=-1, keepdims=True) + EPS)
    q = _dot((hn * gq_ref[...]).astype(BF16), wq_ref[...]) * (ATT_HEAD_DIM ** -0.5 * LOG2E)
    qt_ref[...] = q.T.astype(BF16)
    kv = _dot((hn * gkv_ref[...]).astype(BF16), wkv_ref[...])
    keep = (pl.program_id(1) > 0).astype(F32)
    k_ref[...] = (kv[:, :D_MODEL] * keep).astype(BF16)
    vt_ref[...] = (kv[:, D_MODEL:] * keep).T.astype(BF16)


def _qkv(h, gq, gkv, wq, wkv):
    tm = LEFT_PAD
    nb = SEQ // tm
    src = lambda b, j: (b * nb + jnp.maximum(j - 1, 0), 0)
    return pl.pallas_call(
        _qkv_kernel,
        grid=(BATCH, nb + 1),
        in_specs=[pl.BlockSpec((tm, D_MODEL), src), _const_spec((1, D_MODEL)), _const_spec((1, D_MODEL)),
                  _const_spec(wq.shape), _const_spec(wkv.shape)],
        out_specs=[pl.BlockSpec((D_MODEL, tm), lambda b, j: (0, b * nb + jnp.maximum(j - 1, 0))),
                   pl.BlockSpec((None, tm, D_MODEL), lambda b, j: (b, j, 0)),
                   pl.BlockSpec((None, D_MODEL, tm), lambda b, j: (b, 0, j))],
        out_shape=[jax.ShapeDtypeStruct((D_MODEL, TOKENS), BF16),
                   jax.ShapeDtypeStruct((BATCH, SEQ + LEFT_PAD, D_MODEL), BF16),
                   jax.ShapeDtypeStruct((BATCH, D_MODEL, SEQ + LEFT_PAD), BF16)],
        compiler_params=_params(("parallel", "arbitrary")),
        name="qkv_b",
    )(h, gq, gkv, wq, wkv)


def _attn_kernel(rel_ref, qt_ref, k_ref, vt_ref, o_ref, bias_ref, s0_ref, s1_ref, p0_ref, p1_ref):
    n = pl.program_id(2)

    @pl.when(n == 0)
    def _():
        kj = lax.broadcasted_iota(jnp.int32, (ATT_KB, ATT_QB), 0)
        qi = lax.broadcasted_iota(jnp.int32, (ATT_KB, ATT_QB), 1)
        k_chunk = kj // CHUNK - LEFT_CHUNKS
        q_chunk = qi // CHUNK
        allowed = (k_chunk <= q_chunk) & (k_chunk >= q_chunk - LEFT_CHUNKS)
        for g in range(ATT_HG):
            base = jnp.broadcast_to(rel_ref[g], (ATT_KB, ATT_REL_WIDTH))
            rolled = pltpu.roll(base, 0, 1, stride=1, stride_axis=0)
            table = jnp.where(allowed, rolled[:, :ATT_QB], NEG_BIG)
            bias_ref[0, g] = table
            for v in range(1, ATT_PAD_BLOCKS + 1):
                bias_ref[v, g] = jnp.where(kj >= LEFT_PAD - (v - 1) * ATT_QB, table, NEG_BIG)
        p0_ref[...] = jnp.zeros_like(p0_ref)
        p1_ref[...] = jnp.zeros_like(p1_ref)

    head_of_row = lax.broadcasted_iota(jnp.int32, (MXU_DIM, ATT_QB), 0) // ATT_HEAD_DIM
    n_blocks = ATT_QS // ATT_QB

    def window(qb):
        qb = jnp.minimum(qb, n_blocks - 1)
        q0 = pl.multiple_of(qb * ATT_QB, ATT_QB)
        kstart = pl.multiple_of(n * ATT_QS + qb * ATT_QB, ATT_QB)
        return q0, kstart

    def scores(qb, g, s_ref):
        q0, kstart = window(qb)
        qt = qt_ref[:, pl.ds(q0, ATT_QB)]
        kb = k_ref[pl.ds(kstart, ATT_KB), :]
        qg = jnp.where(head_of_row == g, qt, jnp.zeros_like(qt))
        block_in_seq = kstart // ATT_QB
        variant = jnp.where(block_in_seq < ATT_PAD_BLOCKS, block_in_seq + 1, 0)
        s = _dot(kb, qg)
        maxes = []
        for rows, lanes in ATT_LIVE:
            part = s[rows, lanes] + bias_ref[variant, g, rows, lanes]
            s_ref[rows, lanes] = part
            maxes.append(jnp.max(part, axis=0, keepdims=True))
        return jnp.concatenate(maxes, axis=1)

    def probabilities(s_ref, p_ref, m):
        for rows, lanes in ATT_LIVE:
            p_ref[rows, lanes] = jnp.exp2(s_ref[rows, lanes] - m[:, lanes]).astype(BF16)

    ones_rows = jnp.ones((2 * SUBLANES, ATT_KB), BF16)

    def values(qb, g, p_ref):
        _, kstart = window(qb)
        vg = vt_ref[g * ATT_HEAD_DIM:(g + 1) * ATT_HEAD_DIM, pl.ds(kstart, ATT_KB)]
        pv = _dot(jnp.concatenate([vg, ones_rows], axis=0), p_ref[...])
        return pv[:ATT_HEAD_DIM] / pv[ATT_HEAD_DIM:ATT_HEAD_DIM + 1]

    s_bufs = (s0_ref, s1_ref)
    p_bufs = (p0_ref, p1_ref)
    m_first = scores(0, 0, s_bufs[0])
    m_second = scores(0, 1, s_bufs[1])
    probabilities(s_bufs[0], p_bufs[0], m_first)

    def body(qb, m_next):
        outs = []
        for g in range(ATT_HG):
            slot = g % 2
            outs.append(values(qb, g, p_bufs[slot]))
            probabilities(s_bufs[1 - slot], p_bufs[1 - slot], m_next)
            ahead = g + 2
            m_next = scores(qb + ahead // ATT_HG, ahead % ATT_HG, s_bufs[slot])
        q0, _ = window(qb)
        o_ref[pl.ds(q0, ATT_QB), :] = jnp.concatenate(outs, axis=0).T.astype(BF16)
        return m_next

    lax.fori_loop(0, n_blocks, body, m_second)


def _rel_vectors(rel_table):
    heads, n_rel = rel_table.shape
    far = rel_table[:, n_rel - 1:]
    near = rel_table[:, :1]
    n_far_front = ATT_QB + 1
    n_near = ATT_REL_WIDTH - LEFT_PAD - (CHUNK - 1) - n_far_front
    n_far_back = ATT_REL_WIDTH - n_far_front - n_near - n_rel
    g = jnp.concatenate([jnp.broadcast_to(far, (heads, n_far_front)),
                         jnp.broadcast_to(near, (heads, n_near)),
                         rel_table,
                         jnp.broadcast_to(far, (heads, n_far_back))], axis=1)
    return (g.astype(F32) * LOG2E).reshape(heads, 1, ATT_REL_WIDTH)


def _band_attention(qt, kp, vt, rel):
    ns = SEQ // ATT_QS
    return pl.pallas_call(
        _attn_kernel,
        grid=(BATCH, ATT_HEADS // ATT_HG, ns),
        in_specs=[pl.BlockSpec((ATT_HG, 1, ATT_REL_WIDTH), lambda b, g, n: (g, 0, 0)),
                  pl.BlockSpec((MXU_DIM, ATT_QS), lambda b, g, n: (g, b * ns + n)),
                  pl.BlockSpec((None, SEQ + LEFT_PAD, MXU_DIM), lambda b, g, n: (b, 0, g)),
                  pl.BlockSpec((None, MXU_DIM, SEQ + LEFT_PAD), lambda b, g, n: (b, g, 0))],
        out_specs=pl.BlockSpec((ATT_QS, MXU_DIM), lambda b, g, n: (b * ns + n, g)),
        out_shape=jax.ShapeDtypeStruct((TOKENS, D_MODEL), BF16),
        scratch_shapes=[pltpu.VMEM((ATT_PAD_BLOCKS + 1, ATT_HG, ATT_KB, ATT_QB), F32),
                        pltpu.VMEM((ATT_KB, ATT_QB), F32), pltpu.VMEM((ATT_KB, ATT_QB), F32),
                        pltpu.VMEM((ATT_KB, ATT_QB), BF16), pltpu.VMEM((ATT_KB, ATT_QB), BF16)],
        compiler_params=_params(("parallel", "parallel", "arbitrary")),
        name="band_attn",
    )(rel, qt, kp, vt)


def _row_copy(src, src_row, dst, dst_row, sem):
    return pltpu.make_async_copy(src.at[pl.ds(src_row, 1)], dst.at[pl.ds(dst_row, 1)], sem)


def _tile_wait(hbm, vmem_tile, sem):
    pltpu.make_async_copy(hbm.at[pl.ds(0, MOE_TM)], vmem_tile, sem).wait()


def _scatter_tile(xbuf, rs_smem, xs_hbm, sem):
    for t in range(MOE_TM):
        for which in range(TOP_K):
            _row_copy(xbuf, t, xs_hbm, rs_smem[which, t], sem).start()


def _route_tile(rows, h_ref, o_ref, wo_ref, nf_ref, wrh_ref, wrl_ref, tri_ref,
                h_out_ref, route_ref, gw_ref, cnt_ref, carry_ref):
    cols = rows
    h = h_ref[rows, :] + _dot(o_ref[rows, :], wo_ref[...])
    h_out_ref[rows, :] = h
    xn = _rms(h, nf_ref[...])

    xh = xn.astype(BF16)
    xl = (xn - xh.astype(F32)).astype(BF16)
    logits = _dot(xh, wrh_ref[...]) + _dot(xl, wrh_ref[...]) + _dot(xh, wrl_ref[...])
    lt = logits.T[:N_EXPERTS]
    row = lax.broadcasted_iota(jnp.int32, lt.shape, 0)
    m1 = jnp.max(lt, axis=0, keepdims=True)
    i1 = jnp.min(jnp.where(lt == m1, row, N_EXPERTS), axis=0, keepdims=True)
    rest = jnp.where(row == i1, -jnp.inf, lt)
    m2 = jnp.max(rest, axis=0, keepdims=True)
    i2 = jnp.min(jnp.where(rest == m2, row, N_EXPERTS), axis=0, keepdims=True)
    e2 = jnp.exp(m2 - m1)
    w1 = 1.0 / (1.0 + e2)
    w2 = e2 / (1.0 + e2)

    member = ((row == i1) | (row == i2)).astype(F32)
    within = _dot(member.astype(BF16), tri_ref[...])
    carry = carry_ref[...]
    rank = within + carry[:, 0:1]
    r1 = jnp.sum(jnp.where(row == i1, rank, 0.0), axis=0, keepdims=True).astype(jnp.int32)
    r2 = jnp.sum(jnp.where(row == i2, rank, 0.0), axis=0, keepdims=True).astype(jnp.int32)
    carry_ref[...] = carry + jnp.sum(member, axis=1, keepdims=True)
    cnt_ref[...] = carry_ref[...].astype(jnp.int32)

    s1 = i1 * MOE_CAP + r1
    s2 = i2 * MOE_CAP + r2
    route_ref[:, cols] = jnp.where(row == 0, s1, jnp.where(row == 1, s2, 0))
    gwt = jnp.where(row == 0, w1, jnp.where(row == 1, w2, 0.0))
    pad = jnp.zeros((LANES - N_EXPERTS, gwt.shape[1]), F32)
    gw_ref[rows, :] = jnp.concatenate([gwt, pad], axis=0).T
    return xn


def _post_b_kernel(h_ref, o_ref, wo_ref, nf_ref, wrh_ref, wrl_ref, tri_ref,
                   h_out_ref, route_ref, gw_ref, cnt_ref, xs_hbm,
                   xbuf0, xbuf1, carry_ref, rs0, rs1, cnt_smem, zero_ref, sems, tail_sem):
    g = pl.program_id(0)
    tile_args = (h_ref, o_ref, wo_ref, nf_ref, wrh_ref, wrl_ref, tri_ref,
                 h_out_ref, route_ref, gw_ref, cnt_ref, carry_ref)
    first = slice(0, MOE_TM)
    second = slice(MOE_TM, 2 * MOE_TM)

    @pl.when(g == 0)
    def _():
        carry_ref[...] = jnp.zeros_like(carry_ref)
        xbuf1[...] = jnp.zeros_like(xbuf1)
        which = lax.broadcasted_iota(jnp.int32, (SUBLANES, MOE_TM), 0)
        token = lax.broadcasted_iota(jnp.int32, (SUBLANES, MOE_TM), 1)
        route_ref[:, first] = MOE_SPARE_ROW + jnp.minimum(which, TOP_K - 1) * MOE_TM + token
        pltpu.sync_copy(route_ref.at[:, first], rs1)

    _scatter_tile(xbuf1, rs1, xs_hbm, sems.at[1])
    xn = _route_tile(first, *tile_args)

    @pl.when(g >= 1)
    def _():
        for _ in range(TOP_K):
            _tile_wait(xs_hbm, xbuf0, sems.at[0])

    xbuf0[...] = xn
    pltpu.sync_copy(route_ref.at[:, first], rs0)

    _scatter_tile(xbuf0, rs0, xs_hbm, sems.at[0])
    xn = _route_tile(second, *tile_args)
    for _ in range(TOP_K):
        _tile_wait(xs_hbm, xbuf1, sems.at[1])
    xbuf1[...] = xn
    pltpu.sync_copy(route_ref.at[:, second], rs1)

    @pl.when(g == pl.num_programs(0) - 1)
    def _():
        _scatter_tile(xbuf1, rs1, xs_hbm, sems.at[1])
        for buf, sem in ((xbuf0, sems.at[0]), (xbuf1, sems.at[1])):
            for _ in range(TOP_K):
                _tile_wait(xs_hbm, buf, sem)

        pltpu.sync_copy(cnt_ref, cnt_smem)
        zero_ref[...] = jnp.zeros_like(zero_ref)
        for e in range(N_EXPERTS):
            count = cnt_smem[e, 0]
            aligned = ((count + (SUBLANES - 1)) // SUBLANES) * SUBLANES
            for part in range(MOE_SUPER // MOE_TM):
                start = pl.multiple_of(e * MOE_CAP + aligned + part * MOE_TM, SUBLANES)
                pltpu.make_async_copy(zero_ref, xs_hbm.at[pl.ds(start, MOE_TM)], tail_sem).start()
            for k in range(SUBLANES - 1):
                @pl.when(count + k < aligned)
                def _():
                    _row_copy(zero_ref, 0, xs_hbm, e * MOE_CAP + count + k, tail_sem).start()
        for e in range(N_EXPERTS):
            count = cnt_smem[e, 0]
            aligned = ((count + (SUBLANES - 1)) // SUBLANES) * SUBLANES
            for part in range(MOE_SUPER // MOE_TM):
                _tile_wait(xs_hbm, zero_ref, tail_sem)
            for k in range(SUBLANES - 1):
                @pl.when(count + k < aligned)
                def _():
                    _row_copy(zero_ref, 0, xs_hbm, 0, tail_sem).wait()


def _post_b(h, o, wo, nf, wr_hi, wr_lo):
    tm = MOE_TM
    pair = 2 * tm
    n = TOKENS // pair
    tile = lambda width: pl.BlockSpec((pair, width), lambda i: (i, 0))
    tri = jnp.asarray(np.triu(np.ones((tm, tm), np.float32), k=1), dtype=BF16)
    return pl.pallas_call(
        _post_b_kernel,
        grid=(n,),
        in_specs=[tile(D_MODEL), tile(D_MODEL), _const_spec(wo.shape), _const_spec((1, D_MODEL)),
                  _const_spec(wr_hi.shape), _const_spec(wr_lo.shape), _const_spec(tri.shape)],
        out_specs=[tile(D_MODEL),
                   pl.BlockSpec((SUBLANES, pair), lambda i: (0, i)),
                   tile(LANES),
                   pl.BlockSpec((SUBLANES, LANES), lambda i: (0, 0)),
                   pl.BlockSpec(memory_space=pl.ANY)],
        out_shape=[jax.ShapeDtypeStruct((TOKENS, D_MODEL), F32),
                   jax.ShapeDtypeStruct((SUBLANES, TOKENS), jnp.int32),
                   jax.ShapeDtypeStruct((TOKENS, LANES), F32),
                   jax.ShapeDtypeStruct((SUBLANES, LANES), jnp.int32),
                   jax.ShapeDtypeStruct((MOE_ROWS, D_MODEL), F32)],
        scratch_shapes=[pltpu.VMEM((tm, D_MODEL), F32),
                        pltpu.VMEM((tm, D_MODEL), F32),
                        pltpu.VMEM((SUBLANES, LANES), F32),
                        pltpu.SMEM((SUBLANES, tm), jnp.int32),
                        pltpu.SMEM((SUBLANES, tm), jnp.int32),
                        pltpu.SMEM((SUBLANES, LANES), jnp.int32),
                        pltpu.VMEM((tm, D_MODEL), F32),
                        pltpu.SemaphoreType.DMA((2,)),
                        pltpu.SemaphoreType.DMA(())],
        compiler_params=_params(("arbitrary",)),
        name="post_b",
    )(h, o, wo, nf, wr_hi, wr_lo, tri)


def _moe_kernel(blk_ref, exp_ref, halves_ref, nact_ref, x_ref, wg_ref, wu_ref, wd_ref, y_ref, xb_ref):
    j = pl.program_id(0)
    f = pl.program_id(1)
    first = slice(0, MOE_TM)
    second = slice(MOE_TM, MOE_SUPER)

    @pl.when(j < nact_ref[0])
    def _():
        def half(rows):
            @pl.when(f == 0)
            def _():
                xb_ref[rows, :] = x_ref[rows, :].astype(BF16)
                y_ref[rows, :] = jnp.zeros((MOE_TM, D_MODEL), F32)

            x = xb_ref[rows, :]
            a = (_silu(_dot(x, wg_ref[...])) * _dot(x, wu_ref[...])).astype(BF16)
            y_ref[rows, :] += _dot(a, wd_ref[...])

        half(first)

        @pl.when(halves_ref[j] == 2)
        def _():
            half(second)

        @pl.when((halves_ref[j] == 1) & (f == 0))
        def _():
            y_ref[second, :] = jnp.zeros((MOE_TM, D_MODEL), F32)


def _moe(xs, tile_block, tile_expert, tile_halves, n_active, wg, wu, wd, tf=D_FF_EXPERT // 2):
    nf = D_FF_EXPERT // tf
    fidx = lambda j, f, nact: jnp.where(j < nact[0], f, nf - 1)
    rows = pl.BlockSpec((MOE_SUPER, D_MODEL), lambda j, f, blk, exp, hv, nact: (blk[j], 0))
    grid_spec = pltpu.PrefetchScalarGridSpec(
        num_scalar_prefetch=4,
        grid=(MOE_MAX_TILES, nf),
        in_specs=[rows,
                  pl.BlockSpec((None, D_MODEL, tf), lambda j, f, blk, exp, hv, nact: (exp[j], 0, fidx(j, f, nact))),
                  pl.BlockSpec((None, D_MODEL, tf), lambda j, f, blk, exp, hv, nact: (exp[j], 0, fidx(j, f, nact))),
                  pl.BlockSpec((None, tf, D_MODEL), lambda j, f, blk, exp, hv, nact: (exp[j], fidx(j, f, nact), 0))],
        out_specs=rows,
        scratch_shapes=[pltpu.VMEM((MOE_SUPER, D_MODEL), BF16)],
    )
    return pl.pallas_call(
        _moe_kernel,
        grid_spec=grid_spec,
        out_shape=jax.ShapeDtypeStruct((MOE_ROWS, D_MODEL), F32),
        compiler_params=_params(("arbitrary", "arbitrary")),
        name="moe",
    )(tile_block, tile_expert, tile_halves, n_active, xs, wg, wu, wd)


def _tile_plan(counts):
    experts = jnp.arange(N_EXPERTS)[None, :]
    halves = (counts + (MOE_TM - 1)) // MOE_TM
    tiles = (halves + 1) // 2
    ends = jnp.cumsum(tiles)
    n_active = ends[-1]
    j = jnp.minimum(jnp.arange(MOE_MAX_TILES, dtype=jnp.int32), n_active - 1)
    expert = jnp.sum((j[:, None] >= ends[None, :]).astype(jnp.int32), axis=1)
    pick = lambda per_expert: jnp.sum(jnp.where(expert[:, None] == experts, per_expert[None, :], 0), axis=1)
    k = j - pick(ends - tiles)
    block = expert * MOE_REGION_TILES + k
    tile_halves = jnp.minimum(pick(halves) - 2 * k, 2)
    i32 = lambda a: a.astype(jnp.int32)
    return i32(block), i32(expert), i32(tile_halves), i32(n_active.reshape(1))


def _gather_tile(ys_hbm, rs_smem, ybuf, sem):
    for t in range(MOE_TM):
        for which in range(TOP_K):
            _row_copy(ys_hbm, rs_smem[which, t], ybuf.at[which], t, sem).start(priority=which)


def _final_kernel(h_ref, p_ref, route_ref, route_next_ref, gw_ref, np_ref, wpu_ref, wpg_ref, nfin_ref,
                  ys_hbm, out_ref, ybuf0, ybuf1, rs_smem, sems):
    first = slice(0, MOE_TM)
    second = slice(MOE_TM, 2 * MOE_TM)

    def combine(rows, ybuf):
        gw = gw_ref[rows, :]
        y = gw[:, 0:1] * ybuf[0] + gw[:, 1:2] * ybuf[1]
        hp = h_ref[rows, :] + y
        gate = jax.nn.sigmoid(_dot(_rms(hp, np_ref[...]).astype(BF16), wpg_ref[...]))
        hp = hp + _dot(p_ref[rows, :].astype(BF16), wpu_ref[...]) * gate
        out_ref[rows, :] = _rms(hp, nfin_ref[...])

    @pl.when(pl.program_id(0) == 0)
    def _():
        pltpu.sync_copy(route_ref.at[:, first], rs_smem)
        _gather_tile(ys_hbm, rs_smem, ybuf0, sems.at[0])

    for which in range(TOP_K):
        _tile_wait(ys_hbm, ybuf0.at[which], sems.at[0])
    pltpu.sync_copy(route_ref.at[:, second], rs_smem)
    _gather_tile(ys_hbm, rs_smem, ybuf1, sems.at[1])
    combine(first, ybuf0)

    for which in range(TOP_K):
        _tile_wait(ys_hbm, ybuf1.at[which], sems.at[1])
    pltpu.sync_copy(route_next_ref, rs_smem)
    _gather_tile(ys_hbm, rs_smem, ybuf0, sems.at[0])
    combine(second, ybuf1)

    @pl.when(pl.program_id(0) == pl.num_programs(0) - 1)
    def _():
        for which in range(TOP_K):
            _tile_wait(ys_hbm, ybuf0.at[which], sems.at[0])


def _final(h, p, route, gw, npl, wpu, wpg, nfin, ys):
    tm = MOE_TM
    pair = 2 * tm
    n = TOKENS // pair
    tile = lambda width: pl.BlockSpec((pair, width), lambda i: (i, 0))
    consts = [npl, wpu, wpg, nfin]
    next_tile = lambda i: (0, jnp.minimum(2 * i + 2, TOKENS // tm - 1))
    return pl.pallas_call(
        _final_kernel,
        grid=(n,),
        in_specs=[tile(D_MODEL), pl.BlockSpec((None, pair, PLE_DIM), lambda i: (1, i, 0)),
                  pl.BlockSpec((SUBLANES, pair), lambda i: (0, i)),
                  pl.BlockSpec((SUBLANES, tm), next_tile),
                  tile(LANES)]
                 + [_const_spec(c.shape) for c in consts] + [pl.BlockSpec(memory_space=pl.ANY)],
        out_specs=tile(D_MODEL),
        out_shape=jax.ShapeDtypeStruct((TOKENS, D_MODEL), F32),
        scratch_shapes=[pltpu.VMEM((TOP_K, tm, D_MODEL), F32),
                        pltpu.VMEM((TOP_K, tm, D_MODEL), F32),
                        pltpu.SMEM((SUBLANES, tm), jnp.int32),
                        pltpu.SemaphoreType.DMA((2,))],
        compiler_params=_params(("arbitrary",)),
        name="final",
    )(h, p, route, route, gw, *consts, ys)


def kernel(x, p, positions, norm_mix, norm_ffn, norm_ple, w_in_a, ret_gn, w_out_a, norm_kv, w_kv, w_q_b, rel_bias, w_out_b, w_gate_dense, w_up_dense, w_down_dense, w_router, w_gate_moe, w_up_moe, w_down_moe, w_ple_up, w_ple_gate, norm_final):
    bf = lambda w: w.astype(BF16)
    row = lambda g: g.reshape(1, -1).astype(F32)
    h0 = x.reshape(TOKENS, D_MODEL)
    p2 = p.reshape(2, TOKENS, PLE_DIM)
    pos = positions.reshape(TOKENS, 1)
    inv_freq = (1.0 / (ROPE_BASE ** jnp.linspace(0.0, 1.0, ROPE_HALF, dtype=F32))).reshape(1, ROPE_HALF)

    og = _retention_layer(h0, pos, row(norm_mix[0]), inv_freq, bf(w_in_a[0]), row(ret_gn[0]))
    h1 = _post_a(h0, og, p2, bf(w_out_a[0]), row(norm_ffn[0]), bf(w_gate_dense[0]), bf(w_up_dense[0]),
                 bf(w_down_dense[0]), row(norm_ple[0]), bf(w_ple_up[0]), bf(w_ple_gate[0]))

    qt, kp, vt = _qkv(h1, row(norm_mix[1]), row(norm_kv), bf(w_q_b[0]), bf(w_kv))
    ob = _band_attention(qt, kp, vt, _rel_vectors(rel_bias[0]))
    w_router_pad = jnp.zeros((D_MODEL, LANES), F32).at[:, :N_EXPERTS].set(w_router[0])
    wr_hi = bf(w_router_pad)
    wr_lo = bf(w_router_pad - wr_hi.astype(F32))
    h2, route, gw, counts, xs = _post_b(h1, ob, bf(w_out_b[0]), row(norm_ffn[1]), wr_hi, wr_lo)
    tile_block, tile_expert, tile_halves, n_active = _tile_plan(counts[:, 0])
    ys = _moe(xs, tile_block, tile_expert, tile_halves, n_active,
              bf(w_gate_moe[0]), bf(w_up_moe[0]), bf(w_down_moe[0]))
    out = _final(h2, p2, route, gw, row(norm_ple[1]), bf(w_ple_up[1]), bf(w_ple_gate[1]), row(norm_final), ys)
    return out.reshape(BATCH, SEQ, D_MODEL)
```
